```python
import math
import jax, jax.numpy as jnp
from jax import lax
import numpy as np

D_MODEL = 1024
BATCH = 16
SEQ = 2048
DEPTH = 2

A_HEADS = 8
A_NOPE = 64
A_ROPE = 32
A_V = 64
A_Q_LORA = 256
A_KV_LORA = 128
ROPE_THETA = 10000.0
Q_BLOCK = 128
B_HEADS = 8
B_HEAD_DIM = 64
B_GROUPS = ((128, 1), (512, 4), (2048, 16))
C_HEADS = 8
C_KV_HEADS = 2
C_HEAD_DIM = 64
C_RADIUS = 128
N_BUCKETS = 32
MAX_DISTANCE = 1024
D_FF_DENSE = 2816
N_EXPERTS = 8
TOP_K = 2
D_FF_EXPERT = 3584
N_BRANCH = 3
EPS = 1e-6
NEG_INF = -1e30

A_COLS = A_Q_LORA + A_KV_LORA + A_ROPE
B_COLS = 3 * B_HEADS * B_HEAD_DIM
C_COLS = (C_HEADS + 2 * C_KV_HEADS) * C_HEAD_DIM
G_COLS = N_BRANCH * D_MODEL
IN_COLS = A_COLS + B_COLS + C_COLS + G_COLS

kernel_name = 'hybrid_gated_mla_dilated_swa_moe_encoder'


def rmsnorm(x, g):
    xf = x.astype(jnp.float32)
    y = xf * lax.rsqrt(jnp.mean(xf * xf, axis=-1, keepdims=True) + EPS)
    return (y * g.astype(jnp.float32)).astype(x.dtype)


def t5_bucket(rel):
    nb = N_BUCKETS // 2
    max_exact = nb // 2
    ret = jnp.where(rel > 0, nb, 0)
    n = jnp.abs(rel)
    nf = jnp.maximum(n, 1).astype(jnp.float32)
    large = max_exact + (jnp.log(nf / max_exact) / math.log(MAX_DISTANCE / max_exact) * (nb - max_exact)).astype(jnp.int32)
    large = jnp.minimum(large, nb - 1)
    return ret + jnp.where(n < max_exact, n, large)


def rope_tables(S):
    half = A_ROPE // 2
    inv = ROPE_THETA ** (-jnp.arange(half, dtype=jnp.float32) / half)
    ang = jnp.arange(S, dtype=jnp.float32)[:, None] * inv[None, :]
    return jnp.cos(ang), jnp.sin(ang)


def apply_rope(t, cos, sin):
    half = t.shape[-1] // 2
    cos = cos.astype(t.dtype)
    sin = sin.astype(t.dtype)
    t1, t2 = t[..., :half], t[..., half:]
    return jnp.concatenate([t1 * cos - t2 * sin, t1 * sin + t2 * cos], axis=-1)


def to_blocks(t, blk, nb):
    L = t.shape[-2]
    pad = [(0, 0)] * (t.ndim - 2) + [(0, nb * blk - L), (0, 0)]
    return jnp.pad(t, pad).reshape(t.shape[:-2] + (nb, blk, t.shape[-1]))


def to_windows(t, blk, nb):
    L = t.shape[-2]
    pad = [(0, 0)] * (t.ndim - 2) + [(blk, (nb + 1) * blk - L), (0, 0)]
    tb = jnp.pad(t, pad).reshape(t.shape[:-2] + (nb + 2, blk, t.shape[-1]))
    return jnp.concatenate([tb[..., :-2, :, :], tb[..., 1:-1, :, :], tb[..., 2:, :, :]], axis=-2)


def band_geometry(blk, nb, L):
    a = jnp.arange(blk)
    c = jnp.arange(3 * blk)
    i = jnp.arange(nb)
    rel = c[None, :] - blk - a[:, None]
    kpos = (i[:, None, None] - 1) * blk + c[None, None, :]
    valid = (jnp.abs(rel) <= blk)[None] & (kpos >= 0) & (kpos < L)
    return rel, valid


def mla_branch(cq, ckv, kpe, q_norm_g, w_uq, kv_norm_g, w_ukv):
    B, S, _ = cq.shape
    cos, sin = rope_tables(S)
    scale = (A_NOPE + A_ROPE) ** -0.5
    q = (rmsnorm(cq, q_norm_g) @ w_uq).reshape(B, S, A_HEADS, A_NOPE + A_ROPE) * scale
    q_nope = q[..., :A_NOPE]
    q_pe = apply_rope(q[..., A_NOPE:], cos[:, None, :], sin[:, None, :])
    kv = (rmsnorm(ckv, kv_norm_g) @ w_ukv).reshape(B, S, A_HEADS, A_NOPE + A_V)
    k_nope, v = kv[..., :A_NOPE], kv[..., A_NOPE:]
    k_pe = apply_rope(kpe, cos, sin)
    nb = S // Q_BLOCK
    qn_b = q_nope.reshape(B, nb, Q_BLOCK, A_HEADS, A_NOPE).transpose(1, 0, 2, 3, 4)
    qp_b = q_pe.reshape(B, nb, Q_BLOCK, A_HEADS, A_ROPE).transpose(1, 0, 2, 3, 4)

    def attend(args):
        qn, qp = args
        s = jnp.einsum('bqhd,bkhd->bhqk', qn, k_nope) + jnp.einsum('bqhd,bkd->bhqk', qp, k_pe)
        p = jax.nn.softmax(s.astype(jnp.float32), axis=-1).astype(v.dtype)
        return jnp.einsum('bhqk,bkhd->bqhd', p, v)

    o = lax.map(attend, (qn_b, qp_b))
    return o.transpose(1, 0, 2, 3, 4).reshape(B, S, A_HEADS * A_V)


def dilated_branch(q, k, v, bias_tab):
    B, H, S, dh = q.shape
    q = q * (dh ** -0.5)
    ms, ls, os_ = [], [], []
    for window, dil in B_GROUPS:
        r = window // (2 * dil)
        L = S // dil
        nb = -(-L // r)

        def strided(t):
            return t.reshape(B, H, L, dil, dh).transpose(0, 1, 3, 2, 4)

        def unstride(t):
            t = t.reshape((B, H, dil, nb * r) + t.shape[5:])[:, :, :, :L]
            t = jnp.moveaxis(t, 2, 3)
            return t.reshape((B, H, S) + t.shape[4:])

        qb = to_blocks(strided(q), r, nb)
        kw = to_windows(strided(k), r, nb)
        vw = to_windows(strided(v), r, nb)
        rel, valid = band_geometry(r, nb, L)
        bias = bias_tab[t5_bucket(rel * dil)].transpose(2, 0, 1).astype(jnp.float32)
        s = jnp.einsum('bhgnqd,bhgnkd->bhgnqk', qb, kw).astype(jnp.float32) + bias[None, :, None, None]
        s = jnp.where(valid, s, NEG_INF)
        m = jnp.max(s, axis=-1, keepdims=True)
        p = jnp.exp(s - m)
        l = jnp.sum(p, axis=-1)
        o = jnp.einsum('bhgnqk,bhgnkd->bhgnqd', p.astype(v.dtype), vw).astype(jnp.float32)
        ms.append(unstride(m[..., 0]))
        ls.append(unstride(l))
        os_.append(unstride(o))
    m_all = jnp.max(jnp.stack(ms), axis=0)
    wts = [jnp.exp(mg - m_all) for mg in ms]
    num = sum(w[..., None] * og for w, og in zip(wts, os_))
    den = sum(w * lg for w, lg in zip(wts, ls))
    out = (num / den[..., None]).astype(v.dtype)
    return out.transpose(0, 2, 1, 3).reshape(B, S, H * dh)


def window_gqa_branch(q, k, v, sink, bias_tab):
    B, G, R, S, dh = q.shape
    blk = C_RADIUS
    nb = -(-S // blk)
    qb = to_blocks(q * (dh ** -0.5), blk, nb)
    kw = to_windows(k, blk, nb)
    vw = to_windows(v, blk, nb)
    rel, valid = band_geometry(blk, nb, S)
    bias = bias_tab[t5_bucket(rel)].reshape(blk, 3 * blk, G, R).transpose(2, 3, 0, 1).astype(jnp.float32)
    s = jnp.einsum('bgrnqd,bgnkd->bgrnqk', qb, kw).astype(jnp.float32) + bias[None, :, :, None]
    s = jnp.where(valid, s, NEG_INF)
    sk = sink.astype(jnp.float32).reshape(G, R)[None, :, :, None, None, None]
    m = jnp.maximum(jnp.max(s, axis=-1, keepdims=True), sk)
    p = jnp.exp(s - m)
    den = jnp.sum(p, axis=-1) + jnp.exp(sk - m)[..., 0]
    o = jnp.einsum('bgrnqk,bgnkd->bgrnqd', p.astype(v.dtype), vw).astype(jnp.float32) / den[..., None]
    o = o.reshape(B, G, R, nb * blk, dh)[:, :, :, :S].astype(v.dtype)
    return o.transpose(0, 3, 1, 2, 4).reshape(B, S, G * R * dh)


def hybrid_mixer(h, w_in, q_norm_g, w_uq, kv_norm_g, w_ukv, sink, rel_bias, w_br_a, w_br_b, w_br_c, w_out):
    B, S, D = h.shape
    z = h @ w_in
    o = 0
    cq = z[..., o:o + A_Q_LORA]; o += A_Q_LORA
    ckv = z[..., o:o + A_KV_LORA]; o += A_KV_LORA
    kpe = z[..., o:o + A_ROPE]; o += A_ROPE
    zb = z[..., o:o + B_COLS]; o += B_COLS
    zc = z[..., o:o + C_COLS]; o += C_COLS
    zg = z[..., o:o + G_COLS]
    y_a = mla_branch(cq, ckv, kpe, q_norm_g, w_uq, kv_norm_g, w_ukv)
    qkv_b = zb.reshape(B, S, 3, B_HEADS, B_HEAD_DIM).transpose(2, 0, 3, 1, 4)
    y_b = dilated_branch(qkv_b[0], qkv_b[1], qkv_b[2], rel_bias[:, :B_HEADS])
    R = C_HEADS // C_KV_HEADS
    nq = C_HEADS * C_HEAD_DIM
    nkv = C_KV_HEADS * C_HEAD_DIM
    q_c = zc[..., :nq].reshape(B, S, C_KV_HEADS, R, C_HEAD_DIM).transpose(0, 2, 3, 1, 4)
    k_c = zc[..., nq:nq + nkv].reshape(B, S, C_KV_HEADS, C_HEAD_DIM).transpose(0, 2, 1, 3)
    v_c = zc[..., nq + nkv:].reshape(B, S, C_KV_HEADS, C_HEAD_DIM).transpose(0, 2, 1, 3)
    y_c = window_gqa_branch(q_c, k_c, v_c, sink, rel_bias[:, B_HEADS:])
    gates = jax.nn.sigmoid(zg).reshape(B, S, N_BRANCH, D)
    merged = (gates[:, :, 0] * (y_a @ w_br_a) + gates[:, :, 1] * (y_b @ w_br_b)
              + gates[:, :, 2] * (y_c @ w_br_c))
    return merged @ w_out


def swiglu(h, wg, wu, wd):
    return (jax.nn.silu(h @ wg) * (h @ wu)) @ wd


def moe_swiglu(h, w_router, wg, wu, wd):
    logits = (h @ w_router).astype(jnp.float32)
    top_v, top_i = lax.top_k(logits, TOP_K)
    gate = jax.nn.softmax(top_v, axis=-1)
    combine = jnp.sum(jax.nn.one_hot(top_i, N_EXPERTS, dtype=jnp.float32) * gate[..., None], axis=-2)
    out = jnp.zeros_like(h)
    for e in range(N_EXPERTS):
        out = out + combine[..., e:e + 1].astype(h.dtype) * swiglu(h, wg[e], wu[e], wd[e])
    return out


def setup_inputs(seed: int = 0) -> dict:
    key = jax.random.key(seed)
    ks = iter(jax.random.split(key, 32))
    n_dense = (DEPTH + 1) // 2
    n_moe = DEPTH // 2
    L = DEPTH

    def nrm(shape, scale):
        return jax.random.normal(next(ks), shape, jnp.float32) * scale

    def gain(shape):
        return 1.0 + nrm(shape, 0.05)

    return {
        'x': nrm((BATCH, SEQ, D_MODEL), 1.0),
        'norm1_g': gain((L, D_MODEL)),
        'w_in': nrm((L, D_MODEL, IN_COLS), D_MODEL ** -0.5),
        'q_norm_g': gain((L, A_Q_LORA)),
        'w_uq': nrm((L, A_Q_LORA, A_HEADS * (A_NOPE + A_ROPE)), A_Q_LORA ** -0.5),
        'kv_norm_g': gain((L, A_KV_LORA)),
        'w_ukv': nrm((L, A_KV_LORA, A_HEADS * (A_NOPE + A_V)), A_KV_LORA ** -0.5),
        'sink_logit': nrm((L, C_HEADS), 0.5),
        'rel_bias': nrm((N_BUCKETS, B_HEADS + C_HEADS), 0.5),
        'w_branch_a': nrm((L, A_HEADS * A_V, D_MODEL), (A_HEADS * A_V) ** -0.5),
        'w_branch_b': nrm((L, B_HEADS * B_HEAD_DIM, D_MODEL), (B_HEADS * B_HEAD_DIM) ** -0.5),
        'w_branch_c': nrm((L, C_HEADS * C_HEAD_DIM, D_MODEL), (C_HEADS * C_HEAD_DIM) ** -0.5),
        'w_out': nrm((L, D_MODEL, D_MODEL), D_MODEL ** -0.5),
        'norm2_g': gain((L, D_MODEL)),
        'ffn_w_gate': nrm((n_dense, D_MODEL, D_FF_DENSE), D_MODEL ** -0.5),
        'ffn_w_up': nrm((n_dense, D_MODEL, D_FF_DENSE), D_MODEL ** -0.5),
        'ffn_w_down': nrm((n_dense, D_FF_DENSE, D_MODEL), D_FF_DENSE ** -0.5),
        'router_w': nrm((n_moe, D_MODEL, N_EXPERTS), D_MODEL ** -0.5),
        'exp_w_gate': nrm((n_moe, N_EXPERTS, D_MODEL, D_FF_EXPERT), D_MODEL ** -0.5),
        'exp_w_up': nrm((n_moe, N_EXPERTS, D_MODEL, D_FF_EXPERT), D_MODEL ** -0.5),
        'exp_w_down': nrm((n_moe, N_EXPERTS, D_FF_EXPERT, D_MODEL), D_FF_EXPERT ** -0.5),
        'final_g': gain((D_MODEL,)),
    }


def reference(x, norm1_g, w_in, q_norm_g, w_uq, kv_norm_g, w_ukv, sink_logit, rel_bias,
              w_branch_a, w_branch_b, w_branch_c, w_out, norm2_g, ffn_w_gate, ffn_w_up,
              ffn_w_down, router_w, exp_w_gate, exp_w_up, exp_w_down, final_g):
    for l in range(DEPTH):
        h = rmsnorm(x, norm1_g[l])
        x = x + hybrid_mixer(h, w_in[l], q_norm_g[l], w_uq[l], kv_norm_g[l], w_ukv[l], sink_logit[l],
                             rel_bias, w_branch_a[l], w_branch_b[l], w_branch_c[l], w_out[l])
        h = rmsnorm(x, norm2_g[l])
        i = l // 2
        if l % 2 == 0:
            x = x + swiglu(h, ffn_w_gate[i], ffn_w_up[i], ffn_w_down[i])
        else:
            x = x + moe_swiglu(h, router_w[i], exp_w_gate[i], exp_w_up[i], exp_w_down[i])
    return rmsnorm(x, final_g)
```

```python
import functools
import math

import numpy as np
import jax
import jax.numpy as jnp
from jax import lax
from jax.experimental import pallas as pl
from jax.experimental.pallas import tpu as pltpu

F32 = jnp.float32
BF16 = jnp.bfloat16

D_MODEL = 1024
SEQ = 2048
DEPTH = 2
A_HEADS = 8
A_NOPE = 64
A_ROPE = 32
A_V = 64
A_Q_LORA = 256
A_KV_LORA = 128
ROPE_THETA = 10000.0
B_HEADS = 8
B_HEAD_DIM = 64
B_GROUPS = ((128, 1), (512, 4), (2048, 16))
C_HEADS = 8
C_KV_HEADS = 2
C_HEAD_DIM = 64
C_RADIUS = 128
N_BUCKETS = 32
MAX_DISTANCE = 1024
D_FF_DENSE = 2816
N_EXPERTS = 8
D_FF_EXPERT = 3584
EPS = 1e-6
NEG_INF = -1e30

LANES = 128
HALF = 64
QBLK = 128
B_RADIUS = 64
VMEM_LIMIT = 56 * 1024 * 1024

A_SCALE = (A_NOPE + A_ROPE) ** -0.5
ZA_COLS = A_Q_LORA + A_KV_LORA + 2 * LANES


def _cparams(sem):
    return pltpu.CompilerParams(dimension_semantics=sem, vmem_limit_bytes=VMEM_LIMIT)


def _rms(x, g):
    return x * lax.rsqrt(jnp.mean(x * x, axis=-1, keepdims=True) + EPS) * g


def _dot(a, b):
    return jnp.dot(a, b, preferred_element_type=F32)


def _dot_nt(a, b):
    return lax.dot_general(a, b, (((1,), (1,)), ((), ())), preferred_element_type=F32)


def _t5_bucket_np(rel):
    nb = N_BUCKETS // 2
    max_exact = nb // 2
    ret = np.where(rel > 0, nb, 0)
    n = np.abs(rel)
    nf = np.maximum(n, 1).astype(np.float64)
    t = np.log(nf / max_exact) / math.log(MAX_DISTANCE / max_exact) * (nb - max_exact)
    frac = np.abs(t - np.round(t))
    assert np.all((frac > 1e-5) | (n <= max_exact) | (n >= MAX_DISTANCE))
    large = max_exact + np.floor(np.round(t, 9)).astype(np.int64)
    large = np.minimum(large, nb - 1)
    return (ret + np.where(n < max_exact, n, large)).astype(np.int32)


def _band_index(tq, tk, shift, radius, dil):
    rel = np.arange(tk)[None, :] - np.arange(tq)[:, None] + shift
    idx = _t5_bucket_np(rel * dil)
    return np.where(np.abs(rel) <= radius, idx, -1).astype(np.int32)


def _dilated_index_tables():
    wide = []
    for window, dil in B_GROUPS[:2]:
        for shift in (0, -B_RADIUS, -2 * B_RADIUS):
            wide.append(_band_index(QBLK, 2 * QBLK, shift, B_RADIUS, dil))
    narrow = [_band_index(QBLK, QBLK, 0, B_RADIUS, B_GROUPS[2][1])]
    return np.stack(wide), np.stack(narrow)


def _window_index_tables():
    return np.stack([_band_index(QBLK, 3 * QBLK, shift, C_RADIUS, 1) for shift in (0, -QBLK, -2 * QBLK)])


def _bias_kernel(tab_ref, idx_ref, o_ref, *, col0):
    col = pl.program_id(0) + col0
    idx = idx_ref[...]
    acc = jnp.full(idx.shape, NEG_INF, F32)
    for b in range(N_BUCKETS):
        acc = jnp.where(idx == b, tab_ref[b, col], acc)
    o_ref[...] = acc


def _bias_tables(rel_bias, idx_np, col0, n_heads):
    nv, r, c = idx_np.shape
    return pl.pallas_call(
        functools.partial(_bias_kernel, col0=col0),
        out_shape=jax.ShapeDtypeStruct((n_heads, nv, r, c), F32),
        grid=(n_heads, nv),
        in_specs=[pl.BlockSpec(memory_space=pltpu.SMEM),
                  pl.BlockSpec((None, r, c), lambda h, v: (v, 0, 0))],
        out_specs=pl.BlockSpec((None, None, r, c), lambda h, v: (h, v, 0, 0)),
        compiler_params=_cparams(("arbitrary", "arbitrary")),
        name="bias_tables",
    )(rel_bias, jnp.asarray(idx_np))


IN_TM = 512


def _in_proj_kernel(x_ref, g1_ref, wa_ref, wb_ref, wc_ref, qg_ref, kvg_ref, wq_ref, wqr_ref, wk_ref, wv_ref,
                    place_ref, cq_ref, sq_ref, ck_ref, sk_ref, qa_ref, ka_ref, va_ref, zb_ref, zc_ref):
    h = _rms(x_ref[...], g1_ref[...]).astype(BF16)
    zb_ref[...] = _dot(h, wb_ref[...]).astype(BF16)
    zc_ref[...] = _dot(h, wc_ref[...]).astype(BF16)
    za = _dot(h, wa_ref[...])
    hq = _rms(za[:, :A_Q_LORA], qg_ref[...]).astype(BF16)
    hkv = _rms(za[:, A_Q_LORA:A_Q_LORA + A_KV_LORA], kvg_ref[...]).astype(BF16)
    kpe = za[:, A_Q_LORA + A_KV_LORA:A_Q_LORA + A_KV_LORA + LANES]
    kpe_rot = za[:, A_Q_LORA + A_KV_LORA + LANES:]
    q = _dot(hq, wq_ref[...]) * A_SCALE
    q_rot = _dot(hq, wqr_ref[...]) * A_SCALE
    k_rope = (kpe * ck_ref[...] + kpe_rot * sk_ref[...]).astype(BF16)
    k = _dot(hkv, wk_ref[...]) + _dot(k_rope, place_ref[...])
    ka_ref[...] = k.astype(BF16)
    va_ref[...] = _dot(hkv, wv_ref[...]).astype(BF16)
    cq = cq_ref[...]
    sq = sq_ref[...]
    for hh in range(A_HEADS):
        sl = slice(hh * LANES, (hh + 1) * LANES)
        qa_ref[:, sl] = (q[:, sl] * cq + q_rot[:, sl] * sq).astype(BF16)


def _in_proj(x2, g1, wa, wb, wc, qg, kvg, wq, wqr, wk, wv, place, cq, sq, ck, sk):
    t = x2.shape[0]
    tm = IN_TM
    n_pos = SEQ // tm
    full = lambda a: pl.BlockSpec(a.shape, lambda i: (0,) * a.ndim)
    row = lambda w: pl.BlockSpec((tm, w), lambda i: (i, 0))
    pos = pl.BlockSpec((tm, LANES), lambda i: (i % n_pos, 0))
    return pl.pallas_call(
        _in_proj_kernel,
        out_shape=(jax.ShapeDtypeStruct((t, A_HEADS * LANES), BF16),
                   jax.ShapeDtypeStruct((t, A_HEADS * LANES), BF16),
                   jax.ShapeDtypeStruct((t, A_HEADS * A_V), BF16),
                   jax.ShapeDtypeStruct((t, wb.shape[1]), BF16),
                   jax.ShapeDtypeStruct((t, wc.shape[1]), BF16)),
        grid=(t // tm,),
        in_specs=[row(D_MODEL), full(g1), full(wa), full(wb), full(wc), full(qg), full(kvg), full(wq), full(wqr),
                  full(wk), full(wv), full(place), pos, pos, pos, pos],
        out_specs=(row(A_HEADS * LANES), row(A_HEADS * LANES), row(A_HEADS * A_V), row(wb.shape[1]),
                   row(wc.shape[1])),
        compiler_params=_cparams(("parallel",)),
        name="in_proj",
    )(x2, g1, wa, wb, wc, qg, kvg, wq, wqr, wk, wv, place, cq, sq, ck, sk)


A_TQ = 512


def _mla_kernel(q_ref, k_ref, v_ref, o_ref):
    v = v_ref[...]
    outs = []
    for hh in range(2):
        sl = slice(hh * LANES, (hh + 1) * LANES)
        s = _dot_nt(q_ref[:, sl], k_ref[:, sl])
        m = jnp.max(s, axis=-1, keepdims=True)
        p = jnp.exp(s - m)
        l = jnp.sum(p, axis=-1, keepdims=True)
        outs.append(_dot(p.astype(BF16), v) / l)
    lane = lax.broadcasted_iota(jnp.int32, (1, LANES), 1)
    o_ref[...] = jnp.where(lane < HALF, outs[0], outs[1]).astype(BF16)


def _mla_attention(qa, ka, va, batch):
    t = qa.shape[0]
    nq = SEQ // A_TQ
    return pl.pallas_call(
        _mla_kernel,
        out_shape=jax.ShapeDtypeStruct((t, A_HEADS * A_V), BF16),
        grid=(batch, A_HEADS // 2, nq),
        in_specs=[pl.BlockSpec((A_TQ, 2 * LANES), lambda b, p, i: (b * nq + i, p)),
                  pl.BlockSpec((SEQ, 2 * LANES), lambda b, p, i: (b, p)),
                  pl.BlockSpec((SEQ, LANES), lambda b, p, i: (b, p))],
        out_specs=pl.BlockSpec((A_TQ, LANES), lambda b, p, i: (b * nq + i, p)),
        compiler_params=_cparams(("parallel", "parallel", "arbitrary")),
        name="mla_attention",
    )(qa, ka, va)


def _pair_block(q2, k2, v2, biases, sinks=None):
    lane = lax.broadcasted_iota(jnp.int32, (1, LANES), 1)
    first = lane < HALF
    zero = jnp.zeros_like(q2)
    res = []
    for hh in range(2):
        qh = jnp.where(first if hh == 0 else jnp.logical_not(first), q2, zero)
        s = _dot_nt(qh, k2) + biases[hh]
        m = jnp.max(s, axis=-1, keepdims=True)
        if sinks is not None:
            m = jnp.maximum(m, sinks[hh])
        p = jnp.exp(s - m)
        l = jnp.sum(p, axis=-1, keepdims=True)
        if sinks is not None:
            l = l + jnp.exp(sinks[hh] - m)
        res.append((m, l, _dot(p.astype(BF16), v2)))
    return tuple(jnp.where(first, res[0][i], res[1][i]) for i in range(3))


def _dilated_kernel(q_ref, k_ref, v_ref, bw_ref, bn_ref, o_ref,
                    qf, kf, vf, qs, ks, vs, ms, ls, os_, m_all, l_all, o_all):
    qf[...] = q_ref[...].astype(F32)
    kf[...] = k_ref[...].astype(F32)
    vf[...] = v_ref[...].astype(F32)
    n_blocks = SEQ // QBLK
    for gi, (_, dil) in enumerate(B_GROUPS):
        length = SEQ // dil
        per_stream = length // QBLK
        for g in range(dil):
            rows = pl.ds(g, length, stride=dil) if dil > 1 else slice(None)
            dst = slice(g * length, (g + 1) * length)
            qs[dst, :] = qf[rows, :].astype(BF16)
            ks[dst, :] = kf[rows, :].astype(BF16)
            vs[dst, :] = vf[rows, :].astype(BF16)

        def block(j, carry, gi=gi, length=length, per_stream=per_stream):
            q2 = qs[pl.ds(pl.multiple_of(j * QBLK, QBLK), QBLK), :]
            if per_stream == 1:
                start = pl.multiple_of(j * QBLK, QBLK)
                k2 = ks[pl.ds(start, QBLK), :]
                v2 = vs[pl.ds(start, QBLK), :]
                biases = (bn_ref[0, 0], bn_ref[1, 0])
            else:
                i = j % per_stream
                base = (j - i) * QBLK
                start = pl.multiple_of(base + jnp.clip(i * QBLK - B_RADIUS, 0, length - 2 * QBLK), B_RADIUS)
                k2 = ks[pl.ds(start, 2 * QBLK), :]
                v2 = vs[pl.ds(start, 2 * QBLK), :]
                var = gi * 3 + jnp.where(i == 0, 0, jnp.where(i == per_stream - 1, 2, 1))
                biases = (bw_ref[0, var], bw_ref[1, var])
            m, l, o = _pair_block(q2, k2, v2, biases)
            dst = pl.ds(pl.multiple_of(j * QBLK, QBLK), QBLK)
            ms[dst, :] = m
            ls[dst, :] = l
            os_[dst, :] = o
            return carry

        lax.fori_loop(0, n_blocks, block, 0)
        for g in range(dil):
            rows = pl.ds(g, length, stride=dil) if dil > 1 else slice(None)
            src = slice(g * length, (g + 1) * length)
            m_all[gi, rows, :] = ms[src, :]
            l_all[gi, rows, :] = ls[src, :]
            o_all[gi, rows, :] = os_[src, :]

    def merge(c, carry):
        rows = pl.ds(pl.multiple_of(c * QBLK, QBLK), QBLK)
        m = [m_all[gi, rows, :] for gi in range(3)]
        top = jnp.maximum(jnp.maximum(m[0], m[1]), m[2])
        w = [jnp.exp(mg - top) for mg in m]
        num = w[0] * o_all[0, rows, :] + w[1] * o_all[1, rows, :] + w[2] * o_all[2, rows, :]
        den = w[0] * l_all[0, rows, :] + w[1] * l_all[1, rows, :] + w[2] * l_all[2, rows, :]
        o_ref[rows, :] = (num / den).astype(BF16)
        return carry

    lax.fori_loop(0, n_blocks, merge, 0)


def _dilated_attention(zb, bias_wide, bias_narrow, batch):
    t = zb.shape[0]
    n_pairs = B_HEADS // 2
    seq_block = lambda off: pl.BlockSpec((SEQ, LANES), lambda b, p: (b, off + p))
    nw = bias_wide.shape[1]
    return pl.pallas_call(
        _dilated_kernel,
        out_shape=jax.ShapeDtypeStruct((t, B_HEADS * B_HEAD_DIM), BF16),
        grid=(batch, n_pairs),
        in_specs=[seq_block(0), seq_block(n_pairs), seq_block(2 * n_pairs),
                  pl.BlockSpec((2, nw, QBLK, 2 * QBLK), lambda b, p: (p, 0, 0, 0)),
                  pl.BlockSpec((2, 1, QBLK, QBLK), lambda b, p: (p, 0, 0, 0))],
        out_specs=pl.BlockSpec((SEQ, LANES), lambda b, p: (b, p)),
        scratch_shapes=[pltpu.VMEM((SEQ, LANES), F32)] * 3 + [pltpu.VMEM((SEQ, LANES), BF16)] * 3
        + [pltpu.VMEM((SEQ, LANES), F32)] * 3 + [pltpu.VMEM((3, SEQ, LANES), F32)] * 3,
        compiler_params=_cparams(("parallel", "arbitrary")),
        name="dilated_attention",
    )(zb, zb, zb, bias_wide, bias_narrow)


def _window_kernel(sink_ref, q_ref, k_ref, v_ref, bias_ref, o_ref):
    pair = pl.program_id(1)
    sinks = (sink_ref[pair], sink_ref[pair + C_HEADS // 2])
    n_blocks = SEQ // QBLK

    def block(i, carry):
        rows = pl.ds(pl.multiple_of(i * QBLK, QBLK), QBLK)
        start = pl.multiple_of(jnp.clip((i - 1) * QBLK, 0, SEQ - 3 * QBLK), QBLK)
        var = jnp.where(i == 0, 0, jnp.where(i == n_blocks - 1, 2, 1))
        keys = pl.ds(start, 3 * QBLK)
        _, l, o = _pair_block(q_ref[rows, :], k_ref[keys, :], v_ref[keys, :],
                              (bias_ref[0, var], bias_ref[1, var]), sinks)
        o_ref[rows, :] = (o / l).astype(BF16)
        return carry

    lax.fori_loop(0, n_blocks, block, 0)


def _window_attention(zc, sink, bias, batch):
    t = zc.shape[0]
    n_pairs = C_HEADS // 2
    return pl.pallas_call(
        _window_kernel,
        out_shape=jax.ShapeDtypeStruct((t, C_HEADS * C_HEAD_DIM), BF16),
        grid=(batch, n_pairs),
        in_specs=[pl.BlockSpec(memory_space=pltpu.SMEM),
                  pl.BlockSpec((SEQ, LANES), lambda b, p: (b, p)),
                  pl.BlockSpec((SEQ, LANES), lambda b, p: (b, n_pairs)),
                  pl.BlockSpec((SEQ, LANES), lambda b, p: (b, n_pairs + 1)),
                  pl.BlockSpec((None, 2, 3, QBLK, 3 * QBLK), lambda b, p: (p, 0, 0, 0, 0))],
        out_specs=pl.BlockSpec((SEQ, LANES), lambda b, p: (b, p)),
        compiler_params=_cparams(("parallel", "arbitrary")),
        name="window_attention",
    )(sink, zc, zc, zc, bias)


MERGE_TM = 512


def _merge_kernel(x_ref, g1_ref, ya_ref, yb_ref, yc_ref, wg_ref, wbr_ref, wo_ref, o_ref):
    x = x_ref[...]
    h = _rms(x, g1_ref[...]).astype(BF16)
    merged = None
    for b, y_ref in enumerate((ya_ref, yb_ref, yc_ref)):
        gate = jax.nn.sigmoid(_dot(h, wg_ref[:, b * D_MODEL:(b + 1) * D_MODEL]))
        term = gate * _dot(y_ref[...], wbr_ref[b])
        merged = term if merged is None else merged + term
    o_ref[...] = x + _dot(merged.astype(BF16), wo_ref[...])


def _merge(x2, g1, ya, yb, yc, wg, wbr, wo):
    t = x2.shape[0]
    tm = MERGE_TM
    full = lambda a: pl.BlockSpec(a.shape, lambda i: (0,) * a.ndim)
    row = lambda w: pl.BlockSpec((tm, w), lambda i: (i, 0))
    return pl.pallas_call(
        _merge_kernel,
        out_shape=jax.ShapeDtypeStruct((t, D_MODEL), F32),
        grid=(t // tm,),
        in_specs=[row(D_MODEL), full(g1), row(ya.shape[1]), row(yb.shape[1]), row(yc.shape[1]),
                  full(wg), full(wbr), full(wo)],
        out_specs=row(D_MODEL),
        compiler_params=_cparams(("parallel",)),
        name="gate_merge",
    )(x2, g1, ya, yb, yc, wg, wbr, wo)


ROUTER_TM = 512


def _router_kernel(x_ref, g_ref, wr_ref, cw_ref):
    h = _rms(x_ref[...], g_ref[...])
    logits = jnp.dot(h, wr_ref[...], preferred_element_type=F32, precision=lax.Precision.HIGHEST)
    idx = lax.broadcasted_iota(jnp.int32, logits.shape, 1)
    m1 = jnp.max(logits, axis=-1, keepdims=True)
    i1 = jnp.min(jnp.where(logits == m1, idx, N_EXPERTS), axis=-1, keepdims=True)
    first = idx == i1
    rest = jnp.where(first, -jnp.inf, logits)
    m2 = jnp.max(rest, axis=-1, keepdims=True)
    i2 = jnp.min(jnp.where(rest == m2, idx, N_EXPERTS), axis=-1, keepdims=True)
    second = idx == i2
    e = jnp.exp(m2 - m1)
    den = 1.0 + e
    cw_ref[...] = jnp.where(first, 1.0 / den, 0.0) + jnp.where(second, e / den, 0.0)


def _router(x2, g, wr):
    t = x2.shape[0]
    tm = ROUTER_TM
    return pl.pallas_call(
        _router_kernel,
        out_shape=jax.ShapeDtypeStruct((t, N_EXPERTS), F32),
        grid=(t // tm,),
        in_specs=[pl.BlockSpec((tm, D_MODEL), lambda i: (i, 0)),
                  pl.BlockSpec(g.shape, lambda i: (0, 0)),
                  pl.BlockSpec(wr.shape, lambda i: (0, 0))],
        out_specs=pl.BlockSpec((tm, N_EXPERTS), lambda i: (i, 0)),
        compiler_params=_cparams(("parallel",)),
        name="router",
    )(x2, g, wr)


FFN_TM = 512


def _ffn_kernel(x_ref, g_ref, cw_ref, wg_ref, wu_ref, wd_ref, gf_ref, o_ref, h_scr, acc_scr, exp_scr, *, final):
    e = pl.program_id(1)
    f = pl.program_id(2)
    last_f = f == pl.num_programs(2) - 1

    @pl.when(jnp.logical_and(e == 0, f == 0))
    def _():
        h_scr[...] = _rms(x_ref[...], g_ref[...]).astype(BF16)
        acc_scr[...] = jnp.zeros_like(acc_scr)

    h = h_scr[...]
    act = jax.nn.silu(_dot(h, wg_ref[...])) * _dot(h, wu_ref[...])
    part = _dot(act.astype(BF16), wd_ref[...])

    @pl.when(f == 0)
    def _():
        exp_scr[...] = part

    @pl.when(f != 0)
    def _():
        exp_scr[...] += part

    @pl.when(last_f)
    def _():
        cw = cw_ref[...]
        col = lax.broadcasted_iota(jnp.int32, cw.shape, 1)
        c = jnp.sum(jnp.where(col == e, cw, 0.0), axis=-1, keepdims=True)
        acc_scr[...] += c * exp_scr[...]

    @pl.when(jnp.logical_and(last_f, e == pl.num_programs(1) - 1))
    def _():
        y = x_ref[...] + acc_scr[...]
        if final:
            y = _rms(y, gf_ref[...])
        o_ref[...] = y


def _ffn(x2, g, cw, wg, wu, wd, gf, *, tf, final):
    t = x2.shape[0]
    tm = FFN_TM
    n_exp, _, d_ff = wg.shape
    return pl.pallas_call(
        functools.partial(_ffn_kernel, final=final),
        out_shape=jax.ShapeDtypeStruct((t, D_MODEL), F32),
        grid=(t // tm, n_exp, d_ff // tf),
        in_specs=[pl.BlockSpec((tm, D_MODEL), lambda i, e, f: (i, 0)),
                  pl.BlockSpec(g.shape, lambda i, e, f: (0, 0)),
                  pl.BlockSpec((tm, N_EXPERTS), lambda i, e, f: (i, 0)),
                  pl.BlockSpec((None, D_MODEL, tf), lambda i, e, f: (e, 0, f)),
                  pl.BlockSpec((None, D_MODEL, tf), lambda i, e, f: (e, 0, f)),
                  pl.BlockSpec((None, tf, D_MODEL), lambda i, e, f: (e, f, 0)),
                  pl.BlockSpec(gf.shape, lambda i, e, f: (0, 0))],
        out_specs=pl.BlockSpec((tm, D_MODEL), lambda i, e, f: (i, 0)),
        scratch_shapes=[pltpu.VMEM((tm, D_MODEL), BF16), pltpu.VMEM((tm, D_MODEL), F32),
                        pltpu.VMEM((tm, D_MODEL), F32)],
        compiler_params=_cparams(("parallel", "arbitrary", "arbitrary")),
        name="swiglu_final" if final else "swiglu",
    )(x2, g, cw, wg, wu, wd, gf)


def _rot_cols(w):
    half = A_ROPE // 2
    return jnp.concatenate([-w[..., half:], w[..., :half]], axis=-1)


def _prep_layer(w_in, w_uq, w_ukv, w_br_a, w_br_b, w_br_c):
    o = 0
    w_cq = w_in[:, o:o + A_Q_LORA]; o += A_Q_LORA
    w_ckv = w_in[:, o:o + A_KV_LORA]; o += A_KV_LORA
    w_kpe = w_in[:, o:o + A_ROPE]; o += A_ROPE
    nb = B_HEADS * B_HEAD_DIM
    w_b = w_in[:, o:o + 3 * nb]; o += 3 * nb
    nq = C_HEADS * C_HEAD_DIM
    nkv = C_KV_HEADS * C_HEAD_DIM
    w_c = w_in[:, o:o + nq + 2 * nkv]; o += nq + 2 * nkv
    w_g = w_in[:, o:]
    pad = jnp.zeros((D_MODEL, LANES - A_ROPE), F32)
    wa = jnp.concatenate([w_cq, w_ckv, w_kpe, pad, _rot_cols(w_kpe), pad], axis=1)
    wb = jnp.concatenate([w_b[:, :nb] * (B_HEAD_DIM ** -0.5), w_b[:, nb:]], axis=1)
    q_c = w_c[:, :nq].reshape(D_MODEL, 2, C_HEADS // 2, C_HEAD_DIM).transpose(0, 2, 1, 3).reshape(D_MODEL, nq)
    wc = jnp.concatenate([q_c * (C_HEAD_DIM ** -0.5), w_c[:, nq:]], axis=1)
    uq = w_uq.reshape(A_Q_LORA, A_HEADS, A_NOPE + A_ROPE)
    z32 = jnp.zeros((A_Q_LORA, A_HEADS, LANES - A_NOPE - A_ROPE), F32)
    wq = jnp.concatenate([uq, z32], axis=-1).reshape(A_Q_LORA, A_HEADS * LANES)
    wqr = jnp.concatenate([jnp.zeros((A_Q_LORA, A_HEADS, A_NOPE), F32), _rot_cols(uq[..., A_NOPE:]), z32],
                          axis=-1).reshape(A_Q_LORA, A_HEADS * LANES)
    ukv = w_ukv.reshape(A_KV_LORA, A_HEADS, A_NOPE + A_V)
    wk = jnp.concatenate([ukv[..., :A_NOPE], jnp.zeros((A_KV_LORA, A_HEADS, LANES - A_NOPE), F32)],
                         axis=-1).reshape(A_KV_LORA, A_HEADS * LANES)
    wv = ukv[..., A_NOPE:].reshape(A_KV_LORA, A_HEADS * A_V)
    w_br_c = w_br_c.reshape(2, C_HEADS // 2, C_HEAD_DIM, D_MODEL).transpose(1, 0, 2, 3).reshape(nq, D_MODEL)
    wbr = jnp.stack([w_br_a, w_br_b, w_br_c])
    cast = lambda a: a.astype(BF16)
    return dict(wa=cast(wa), wb=cast(wb), wc=cast(wc), wg=cast(w_g), wq=cast(wq), wqr=cast(wqr), wk=cast(wk),
                wv=cast(wv), wbr=cast(wbr))


def _rope_tables():
    half = A_ROPE // 2
    inv = ROPE_THETA ** (-jnp.arange(half, dtype=F32) / half)
    ang = jnp.arange(SEQ, dtype=F32)[:, None] * inv[None, :]
    cos2 = jnp.tile(jnp.cos(ang), (1, 2))
    sin2 = jnp.tile(jnp.sin(ang), (1, 2))
    ones = jnp.ones((SEQ, A_NOPE), F32)
    z = lambda w: jnp.zeros((SEQ, w), F32)
    cq = jnp.concatenate([ones, cos2, z(LANES - A_NOPE - A_ROPE)], axis=1)
    sq = jnp.concatenate([z(A_NOPE), sin2, z(LANES - A_NOPE - A_ROPE)], axis=1)
    ck = jnp.concatenate([cos2, z(LANES - A_ROPE)], axis=1)
    sk = jnp.concatenate([sin2, z(LANES - A_ROPE)], axis=1)
    return cq, sq, ck, sk


def _rope_placement():
    place = np.zeros((LANES, A_HEADS * LANES), np.float32)
    for h in range(A_HEADS):
        for j in range(A_ROPE):
            place[j, h * LANES + A_NOPE + j] = 1.0
    return jnp.asarray(place, dtype=BF16)


def kernel(x, norm1_g, w_in, q_norm_g, w_uq, kv_norm_g, w_ukv, sink_logit, rel_bias, w_branch_a, w_branch_b,
           w_branch_c, w_out, norm2_g, ffn_w_gate, ffn_w_up, ffn_w_down, router_w, exp_w_gate, exp_w_up,
           exp_w_down, final_g):
    batch, seq, d = x.shape
    assert seq == SEQ and d == D_MODEL
    t = batch * seq
    x2 = x.reshape(t, d)

    wide_idx, narrow_idx = _dilated_index_tables()
    bias_wide = _bias_tables(rel_bias, wide_idx, 0, B_HEADS)
    bias_narrow = _bias_tables(rel_bias, narrow_idx, 0, B_HEADS)
    bias_c = _bias_tables(rel_bias, _window_index_tables(), B_HEADS, C_HEADS)
    bias_c = bias_c.reshape(2, C_HEADS // 2, *bias_c.shape[1:]).transpose(1, 0, 2, 3, 4)

    cq, sq, ck, sk = _rope_tables()
    place = _rope_placement()
    row = lambda v: v.reshape(1, -1)
    ones_cw = jnp.ones((t, N_EXPERTS), F32)

    for l in range(DEPTH):
        w = _prep_layer(w_in[l], w_uq[l], w_ukv[l], w_branch_a[l], w_branch_b[l], w_branch_c[l])
        g1 = row(norm1_g[l])
        qa, ka, va, zb, zc = _in_proj(x2, g1, w["wa"], w["wb"], w["wc"], row(q_norm_g[l]), row(kv_norm_g[l]),
                                      w["wq"], w["wqr"], w["wk"], w["wv"], place, cq, sq, ck, sk)
        ya = _mla_attention(qa, ka, va, batch)
        yb = _dilated_attention(zb, bias_wide, bias_narrow, batch)
        yc = _window_attention(zc, sink_logit[l], bias_c, batch)
        x2 = _merge(x2, g1, ya, yb, yc, w["wg"], w["wbr"], w_out[l].astype(BF16))
        g2 = row(norm2_g[l])
        final = l == DEPTH - 1
        i = l // 2
        if l % 2 == 0:
            x2 = _ffn(x2, g2, ones_cw, ffn_w_gate[i].astype(BF16)[None], ffn_w_up[i].astype(BF16)[None],
                      ffn_w_down[i].astype(BF16)[None], row(final_g), tf=D_FF_DENSE // 2, final=final)
        else:
            cw = _router(x2, g2, router_w[i])
            x2 = _ffn(x2, g2, cw, exp_w_gate[i].astype(BF16), exp_w_up[i].astype(BF16),
                      exp_w_down[i].astype(BF16), row(final_g), tf=D_FF_EXPERT // 4, final=final)
    return x2.reshape(batch, seq, d)
```

```python
import functools
import math

import numpy as np
import jax
import jax.numpy as jnp
from jax import lax
from jax.experimental import pallas as pl
from jax.experimental.pallas import tpu as pltpu

F32 = jnp.float32
BF16 = jnp.bfloat16

D_MODEL = 1024
SEQ = 2048
DEPTH = 2
A_HEADS = 8
A_NOPE = 64
A_ROPE = 32
A_V = 64
A_Q_LORA = 256
A_KV_LORA = 128
ROPE_THETA = 10000.0
B_HEADS = 8
B_HEAD_DIM = 64
B_GROUPS = ((128, 1), (512, 4), (2048, 16))
C_HEADS = 8
C_KV_HEADS = 2
C_HEAD_DIM = 64
C_RADIUS = 128
N_BUCKETS = 32
MAX_DISTANCE = 1024
D_FF_DENSE = 2816
N_EXPERTS = 8
D_FF_EXPERT = 3584
EPS = 1e-6
NEG_INF = -1e30

LANES = 128
HALF = 64
QBLK = 128
B_RADIUS = 64
VMEM_LIMIT = 56 * 1024 * 1024

A_SCALE = (A_NOPE + A_ROPE) ** -0.5
ZA_COLS = A_Q_LORA + A_KV_LORA + 2 * LANES


def _cparams(sem):
    return pltpu.CompilerParams(dimension_semantics=sem, vmem_limit_bytes=VMEM_LIMIT)


def _rms(x, g):
    return x * lax.rsqrt(jnp.mean(x * x, axis=-1, keepdims=True) + EPS) * g


def _dot(a, b):
    return jnp.dot(a, b, preferred_element_type=F32)


def _dot_nt(a, b):
    return lax.dot_general(a, b, (((1,), (1,)), ((), ())), preferred_element_type=F32)


def _t5_bucket_np(rel):
    nb = N_BUCKETS // 2
    max_exact = nb // 2
    ret = np.where(rel > 0, nb, 0)
    n = np.abs(rel)
    nf = np.maximum(n, 1).astype(np.float64)
    t = np.log(nf / max_exact) / math.log(MAX_DISTANCE / max_exact) * (nb - max_exact)
    frac = np.abs(t - np.round(t))
    assert np.all((frac > 1e-5) | (n <= max_exact) | (n >= MAX_DISTANCE))
    large = max_exact + np.floor(np.round(t, 9)).astype(np.int64)
    large = np.minimum(large, nb - 1)
    return (ret + np.where(n < max_exact, n, large)).astype(np.int32)


def _band_index(tq, tk, shift, radius, dil):
    rel = np.arange(tk)[None, :] - np.arange(tq)[:, None] + shift
    idx = _t5_bucket_np(rel * dil)
    return np.where(np.abs(rel) <= radius, idx, -1).astype(np.int32)


def _dilated_index_tables():
    wide = []
    for window, dil in B_GROUPS[:2]:
        for shift in (0, -B_RADIUS, -2 * B_RADIUS):
            wide.append(_band_index(QBLK, 2 * QBLK, shift, B_RADIUS, dil))
    narrow = [_band_index(QBLK, QBLK, 0, B_RADIUS, B_GROUPS[2][1])]
    return np.stack(wide), np.stack(narrow)


def _window_index_tables():
    return np.stack([_band_index(QBLK, 3 * QBLK, shift, C_RADIUS, 1) for shift in (0, -QBLK, -2 * QBLK)])


def _bias_kernel(tab_ref, idx_ref, o_ref, *, col0):
    col = pl.program_id(0) + col0
    idx = idx_ref[...]
    acc = jnp.full(idx.shape, NEG_INF, F32)
    for b in range(N_BUCKETS):
        acc = jnp.where(idx == b, tab_ref[b, col], acc)
    o_ref[...] = acc


def _bias_tables(rel_bias, idx_np, col0, n_heads):
    nv, r, c = idx_np.shape
    return pl.pallas_call(
        functools.partial(_bias_kernel, col0=col0),
        out_shape=jax.ShapeDtypeStruct((n_heads, nv, r, c), F32),
        grid=(n_heads, nv),
        in_specs=[pl.BlockSpec(memory_space=pltpu.SMEM),
                  pl.BlockSpec((None, r, c), lambda h, v: (v, 0, 0))],
        out_specs=pl.BlockSpec((None, None, r, c), lambda h, v: (h, v, 0, 0)),
        compiler_params=_cparams(("arbitrary", "arbitrary")),
        name="bias_tables",
    )(rel_bias, jnp.asarray(idx_np))


IN_TM = 512


def _in_proj_kernel(x_ref, g1_ref, wa_ref, wb_ref, wc_ref, qg_ref, kvg_ref, wq_ref, wqr_ref, wk_ref, wv_ref,
                    place_ref, cq_ref, sq_ref, ck_ref, sk_ref, qa_ref, ka_ref, va_ref, zb_ref, zc_ref):
    h = _rms(x_ref[...], g1_ref[...]).astype(BF16)
    zb_ref[...] = _dot(h, wb_ref[...]).astype(BF16)
    zc_ref[...] = _dot(h, wc_ref[...]).astype(BF16)
    za = _dot(h, wa_ref[...])
    hq = _rms(za[:, :A_Q_LORA], qg_ref[...]).astype(BF16)
    hkv = _rms(za[:, A_Q_LORA:A_Q_LORA + A_KV_LORA], kvg_ref[...]).astype(BF16)
    kpe = za[:, A_Q_LORA + A_KV_LORA:A_Q_LORA + A_KV_LORA + LANES]
    kpe_rot = za[:, A_Q_LORA + A_KV_LORA + LANES:]
    q = _dot(hq, wq_ref[...]) * A_SCALE
    q_rot = _dot(hq, wqr_ref[...]) * A_SCALE
    k_rope = (kpe * ck_ref[...] + kpe_rot * sk_ref[...]).astype(BF16)
    k = _dot(hkv, wk_ref[...]) + _dot(k_rope, place_ref[...])
    ka_ref[...] = k.astype(BF16)
    va_ref[...] = _dot(hkv, wv_ref[...]).astype(BF16)
    cq = cq_ref[...]
    sq = sq_ref[...]
    for hh in range(A_HEADS):
        sl = slice(hh * LANES, (hh + 1) * LANES)
        qa_ref[:, sl] = (q[:, sl] * cq + q_rot[:, sl] * sq).astype(BF16)


def _in_proj(x2, g1, wa, wb, wc, qg, kvg, wq, wqr, wk, wv, place, cq, sq, ck, sk):
    t = x2.shape[0]
    tm = IN_TM
    n_pos = SEQ // tm
    full = lambda a: pl.BlockSpec(a.shape, lambda i: (0,) * a.ndim)
    row = lambda w: pl.BlockSpec((tm, w), lambda i: (i, 0))
    pos = pl.BlockSpec((tm, LANES), lambda i: (i % n_pos, 0))
    return pl.pallas_call(
        _in_proj_kernel,
        out_shape=(jax.ShapeDtypeStruct((t, A_HEADS * LANES), BF16),
                   jax.ShapeDtypeStruct((t, A_HEADS * LANES), BF16),
                   jax.ShapeDtypeStruct((t, A_HEADS * A_V), BF16),
                   jax.ShapeDtypeStruct((t, wb.shape[1]), BF16),
                   jax.ShapeDtypeStruct((t, wc.shape[1]), BF16)),
        grid=(t // tm,),
        in_specs=[row(D_MODEL), full(g1), full(wa), full(wb), full(wc), full(qg), full(kvg), full(wq), full(wqr),
                  full(wk), full(wv), full(place), pos, pos, pos, pos],
        out_specs=(row(A_HEADS * LANES), row(A_HEADS * LANES), row(A_HEADS * A_V), row(wb.shape[1]),
                   row(wc.shape[1])),
        compiler_params=_cparams(("parallel",)),
        name="in_proj",
    )(x2, g1, wa, wb, wc, qg, kvg, wq, wqr, wk, wv, place, cq, sq, ck, sk)


A_TQ = 512


def _mla_kernel(q_ref, k_ref, v_ref, o_ref):
    v = v_ref[...]
    outs = []
    for hh in range(2):
        sl = slice(hh * LANES, (hh + 1) * LANES)
        s = _dot_nt(q_ref[:, sl], k_ref[:, sl])
        m = jnp.max(s, axis=-1, keepdims=True)
        p = jnp.exp(s - m)
        l = jnp.sum(p, axis=-1, keepdims=True)
        outs.append(_dot(p.astype(BF16), v) / l)
    lane = lax.broadcasted_iota(jnp.int32, (1, LANES), 1)
    o_ref[...] = jnp.where(lane < HALF, outs[0], outs[1]).astype(BF16)


def _mla_attention(qa, ka, va, batch):
    t = qa.shape[0]
    nq = SEQ // A_TQ
    return pl.pallas_call(
        _mla_kernel,
        out_shape=jax.ShapeDtypeStruct((t, A_HEADS * A_V), BF16),
        grid=(batch, A_HEADS // 2, nq),
        in_specs=[pl.BlockSpec((A_TQ, 2 * LANES), lambda b, p, i: (b * nq + i, p)),
                  pl.BlockSpec((SEQ, 2 * LANES), lambda b, p, i: (b, p)),
                  pl.BlockSpec((SEQ, LANES), lambda b, p, i: (b, p))],
        out_specs=pl.BlockSpec((A_TQ, LANES), lambda b, p, i: (b * nq + i, p)),
        compiler_params=_cparams(("parallel", "parallel", "arbitrary")),
        name="mla_attention",
    )(qa, ka, va)


def _pair_block(q2, k2, v2, biases, sinks=None):
    lane = lax.broadcasted_iota(jnp.int32, (1, LANES), 1)
    first = lane < HALF
    zero = jnp.zeros_like(q2)
    res = []
    for hh in range(2):
        qh = jnp.where(first if hh == 0 else jnp.logical_not(first), q2, zero)
        s = _dot_nt(qh, k2) + biases[hh]
        m = jnp.max(s, axis=-1, keepdims=True)
        if sinks is not None:
            m = jnp.maximum(m, sinks[hh])
        p = jnp.exp(s - m)
        l = jnp.sum(p, axis=-1, keepdims=True)
        if sinks is not None:
            l = l + jnp.exp(sinks[hh] - m)
        res.append((m, l, _dot(p.astype(BF16), v2)))
    return tuple(jnp.where(first, res[0][i], res[1][i]) for i in range(3))


def _dilated_kernel(q_ref, k_ref, v_ref, bw_ref, bn_ref, o_ref,
                    qf, kf, vf, qs, ks, vs, ms, ls, os_, m_all, l_all, o_all):
    qf[...] = q_ref[...].astype(F32)
    kf[...] = k_ref[...].astype(F32)
    vf[...] = v_ref[...].astype(F32)
    n_blocks = SEQ // QBLK
    for gi, (_, dil) in enumerate(B_GROUPS):
        length = SEQ // dil
        per_stream = length // QBLK
        for g in range(dil):
            rows = pl.ds(g, length, stride=dil) if dil > 1 else slice(None)
            dst = slice(g * length, (g + 1) * length)
            qs[dst, :] = qf[rows, :].astype(BF16)
            ks[dst, :] = kf[rows, :].astype(BF16)
            vs[dst, :] = vf[rows, :].astype(BF16)

        for j in range(n_blocks):
            rows = slice(j * QBLK, (j + 1) * QBLK)
            if per_stream == 1:
                keys = rows
                biases = (bn_ref[0, 0], bn_ref[1, 0])
            else:
                i = j % per_stream
                start = (j - i) * QBLK + min(max(i * QBLK - B_RADIUS, 0), length - 2 * QBLK)
                keys = slice(start, start + 2 * QBLK)
                var = gi * 3 + (0 if i == 0 else 2 if i == per_stream - 1 else 1)
                biases = (bw_ref[0, var], bw_ref[1, var])
            m, l, o = _pair_block(qs[rows, :], ks[keys, :], vs[keys, :], biases)
            ms[rows, :] = m
            ls[rows, :] = l
            os_[rows, :] = o
        for g in range(dil):
            rows = pl.ds(g, length, stride=dil) if dil > 1 else slice(None)
            src = slice(g * length, (g + 1) * length)
            m_all[gi, rows, :] = ms[src, :]
            l_all[gi, rows, :] = ls[src, :]
            o_all[gi, rows, :] = os_[src, :]

    def merge(c, carry):
        rows = pl.ds(pl.multiple_of(c * QBLK, QBLK), QBLK)
        m = [m_all[gi, rows, :] for gi in range(3)]
        top = jnp.maximum(jnp.maximum(m[0], m[1]), m[2])
        w = [jnp.exp(mg - top) for mg in m]
        num = w[0] * o_all[0, rows, :] + w[1] * o_all[1, rows, :] + w[2] * o_all[2, rows, :]
        den = w[0] * l_all[0, rows, :] + w[1] * l_all[1, rows, :] + w[2] * l_all[2, rows, :]
        o_ref[rows, :] = (num / den).astype(BF16)
        return carry

    lax.fori_loop(0, n_blocks, merge, 0)


def _dilated_attention(zb, bias_wide, bias_narrow, batch):
    t = zb.shape[0]
    n_pairs = B_HEADS // 2
    seq_block = lambda off: pl.BlockSpec((SEQ, LANES), lambda b, p: (b, off + p))
    nw = bias_wide.shape[1]
    return pl.pallas_call(
        _dilated_kernel,
        out_shape=jax.ShapeDtypeStruct((t, B_HEADS * B_HEAD_DIM), BF16),
        grid=(batch, n_pairs),
        in_specs=[seq_block(0), seq_block(n_pairs), seq_block(2 * n_pairs),
                  pl.BlockSpec((2, nw, QBLK, 2 * QBLK), lambda b, p: (p, 0, 0, 0)),
                  pl.BlockSpec((2, 1, QBLK, QBLK), lambda b, p: (p, 0, 0, 0))],
        out_specs=pl.BlockSpec((SEQ, LANES), lambda b, p: (b, p)),
        scratch_shapes=[pltpu.VMEM((SEQ, LANES), F32)] * 3 + [pltpu.VMEM((SEQ, LANES), BF16)] * 3
        + [pltpu.VMEM((SEQ, LANES), F32)] * 3 + [pltpu.VMEM((3, SEQ, LANES), F32)] * 3,
        compiler_params=_cparams(("parallel", "arbitrary")),
        name="dilated_attention",
    )(zb, zb, zb, bias_wide, bias_narrow)


def _window_kernel(sink_ref, q_ref, k_ref, v_ref, bias_ref, o_ref):
    pair = pl.program_id(1)
    sinks = (sink_ref[pair], sink_ref[pair + C_HEADS // 2])
    n_blocks = SEQ // QBLK

    for i in range(n_blocks):
        rows = slice(i * QBLK, (i + 1) * QBLK)
        start = min(max((i - 1) * QBLK, 0), SEQ - 3 * QBLK)
        var = 0 if i == 0 else 2 if i == n_blocks - 1 else 1
        keys = slice(start, start + 3 * QBLK)
        _, l, o = _pair_block(q_ref[rows, :], k_ref[keys, :], v_ref[keys, :],
                              (bias_ref[0, var], bias_ref[1, var]), sinks)
        o_ref[rows, :] = (o / l).astype(BF16)


def _window_attention(zc, sink, bias, batch):
    t = zc.shape[0]
    n_pairs = C_HEADS // 2
    return pl.pallas_call(
        _window_kernel,
        out_shape=jax.ShapeDtypeStruct((t, C_HEADS * C_HEAD_DIM), BF16),
        grid=(batch, n_pairs),
        in_specs=[pl.BlockSpec(memory_space=pltpu.SMEM),
                  pl.BlockSpec((SEQ, LANES), lambda b, p: (b, p)),
                  pl.BlockSpec((SEQ, LANES), lambda b, p: (b, n_pairs)),
                  pl.BlockSpec((SEQ, LANES), lambda b, p: (b, n_pairs + 1)),
                  pl.BlockSpec((None, 2, 3, QBLK, 3 * QBLK), lambda b, p: (p, 0, 0, 0, 0))],
        out_specs=pl.BlockSpec((SEQ, LANES), lambda b, p: (b, p)),
        compiler_params=_cparams(("parallel", "arbitrary")),
        name="window_attention",
    )(sink, zc, zc, zc, bias)


MERGE_TM = 512


def _merge_kernel(x_ref, g1_ref, ya_ref, yb_ref, yc_ref, wg_ref, wbr_ref, wo_ref, o_ref):
    x = x_ref[...]
    h = _rms(x, g1_ref[...]).astype(BF16)
    merged = None
    for b, y_ref in enumerate((ya_ref, yb_ref, yc_ref)):
        gate = jax.nn.sigmoid(_dot(h, wg_ref[:, b * D_MODEL:(b + 1) * D_MODEL]))
        term = gate * _dot(y_ref[...], wbr_ref[b])
        merged = term if merged is None else merged + term
    o_ref[...] = x + _dot(merged.astype(BF16), wo_ref[...])


def _merge(x2, g1, ya, yb, yc, wg, wbr, wo):
    t = x2.shape[0]
    tm = MERGE_TM
    full = lambda a: pl.BlockSpec(a.shape, lambda i: (0,) * a.ndim)
    row = lambda w: pl.BlockSpec((tm, w), lambda i: (i, 0))
    return pl.pallas_call(
        _merge_kernel,
        out_shape=jax.ShapeDtypeStruct((t, D_MODEL), F32),
        grid=(t // tm,),
        in_specs=[row(D_MODEL), full(g1), row(ya.shape[1]), row(yb.shape[1]), row(yc.shape[1]),
                  full(wg), full(wbr), full(wo)],
        out_specs=row(D_MODEL),
        compiler_params=_cparams(("parallel",)),
        name="gate_merge",
    )(x2, g1, ya, yb, yc, wg, wbr, wo)


ROUTER_TM = 512


def _router_kernel(x_ref, g_ref, wr_ref, cw_ref):
    h = _rms(x_ref[...], g_ref[...])
    logits = jnp.dot(h, wr_ref[...], preferred_element_type=F32, precision=lax.Precision.HIGHEST)
    idx = lax.broadcasted_iota(jnp.int32, logits.shape, 1)
    m1 = jnp.max(logits, axis=-1, keepdims=True)
    i1 = jnp.min(jnp.where(logits == m1, idx, N_EXPERTS), axis=-1, keepdims=True)
    first = idx == i1
    rest = jnp.where(first, -jnp.inf, logits)
    m2 = jnp.max(rest, axis=-1, keepdims=True)
    i2 = jnp.min(jnp.where(rest == m2, idx, N_EXPERTS), axis=-1, keepdims=True)
    second = idx == i2
    e = jnp.exp(m2 - m1)
    den = 1.0 + e
    cw_ref[...] = jnp.where(first, 1.0 / den, 0.0) + jnp.where(second, e / den, 0.0)


def _router(x2, g, wr):
    t = x2.shape[0]
    tm = ROUTER_TM
    return pl.pallas_call(
        _router_kernel,
        out_shape=jax.ShapeDtypeStruct((t, N_EXPERTS), F32),
        grid=(t // tm,),
        in_specs=[pl.BlockSpec((tm, D_MODEL), lambda i: (i, 0)),
                  pl.BlockSpec(g.shape, lambda i: (0, 0)),
                  pl.BlockSpec(wr.shape, lambda i: (0, 0))],
        out_specs=pl.BlockSpec((tm, N_EXPERTS), lambda i: (i, 0)),
        compiler_params=_cparams(("parallel",)),
        name="router",
    )(x2, g, wr)


FFN_TM = 512


def _ffn_kernel(x_ref, g_ref, cw_ref, wg_ref, wu_ref, wd_ref, gf_ref, o_ref, h_scr, acc_scr, exp_scr, *, final):
    e = pl.program_id(1)
    f = pl.program_id(2)
    last_f = f == pl.num_programs(2) - 1

    @pl.when(jnp.logical_and(e == 0, f == 0))
    def _():
        h_scr[...] = _rms(x_ref[...], g_ref[...]).astype(BF16)
        acc_scr[...] = jnp.zeros_like(acc_scr)

    h = h_scr[...]
    act = jax.nn.silu(_dot(h, wg_ref[...])) * _dot(h, wu_ref[...])
    part = _dot(act.astype(BF16), wd_ref[...])

    @pl.when(f == 0)
    def _():
        exp_scr[...] = part

    @pl.when(f != 0)
    def _():
        exp_scr[...] += part

    @pl.when(last_f)
    def _():
        cw = cw_ref[...]
        col = lax.broadcasted_iota(jnp.int32, cw.shape, 1)
        c = jnp.sum(jnp.where(col == e, cw, 0.0), axis=-1, keepdims=True)
        acc_scr[...] += c * exp_scr[...]

    @pl.when(jnp.logical_and(last_f, e == pl.num_programs(1) - 1))
    def _():
        y = x_ref[...] + acc_scr[...]
        if final:
            y = _rms(y, gf_ref[...])
        o_ref[...] = y


def _ffn(x2, g, cw, wg, wu, wd, gf, *, tf, final):
    t = x2.shape[0]
    tm = FFN_TM
    n_exp, _, d_ff = wg.shape
    return pl.pallas_call(
        functools.partial(_ffn_kernel, final=final),
        out_shape=jax.ShapeDtypeStruct((t, D_MODEL), F32),
        grid=(t // tm, n_exp, d_ff // tf),
        in_specs=[pl.BlockSpec((tm, D_MODEL), lambda i, e, f: (i, 0)),
                  pl.BlockSpec(g.shape, lambda i, e, f: (0, 0)),
                  pl.BlockSpec((tm, N_EXPERTS), lambda i, e, f: (i, 0)),
                  pl.BlockSpec((None, D_MODEL, tf), lambda i, e, f: (e, 0, f)),
                  pl.BlockSpec((None, D_MODEL, tf), lambda i, e, f: (e, 0, f)),
                  pl.BlockSpec((None, tf, D_MODEL), lambda i, e, f: (e, f, 0)),
                  pl.BlockSpec(gf.shape, lambda i, e, f: (0, 0))],
        out_specs=pl.BlockSpec((tm, D_MODEL), lambda i, e, f: (i, 0)),
        scratch_shapes=[pltpu.VMEM((tm, D_MODEL), BF16), pltpu.VMEM((tm, D_MODEL), F32),
                        pltpu.VMEM((tm, D_MODEL), F32)],
        compiler_params=_cparams(("parallel", "arbitrary", "arbitrary")),
        name="swiglu_final" if final else "swiglu",
    )(x2, g, cw, wg, wu, wd, gf)


def _rot_cols(w):
    half = A_ROPE // 2
    return jnp.concatenate([-w[..., half:], w[..., :half]], axis=-1)


def _prep_layer(w_in, w_uq, w_ukv, w_br_a, w_br_b, w_br_c):
    o = 0
    w_cq = w_in[:, o:o + A_Q_LORA]; o += A_Q_LORA
    w_ckv = w_in[:, o:o + A_KV_LORA]; o += A_KV_LORA
    w_kpe = w_in[:, o:o + A_ROPE]; o += A_ROPE
    nb = B_HEADS * B_HEAD_DIM
    w_b = w_in[:, o:o + 3 * nb]; o += 3 * nb
    nq = C_HEADS * C_HEAD_DIM
    nkv = C_KV_HEADS * C_HEAD_DIM
    w_c = w_in[:, o:o + nq + 2 * nkv]; o += nq + 2 * nkv
    w_g = w_in[:, o:]
    pad = jnp.zeros((D_MODEL, LANES - A_ROPE), F32)
    wa = jnp.concatenate([w_cq, w_ckv, w_kpe, pad, _rot_cols(w_kpe), pad], axis=1)
    wb = jnp.concatenate([w_b[:, :nb] * (B_HEAD_DIM ** -0.5), w_b[:, nb:]], axis=1)
    q_c = w_c[:, :nq].reshape(D_MODEL, 2, C_HEADS // 2, C_HEAD_DIM).transpose(0, 2, 1, 3).reshape(D_MODEL, nq)
    wc = jnp.concatenate([q_c * (C_HEAD_DIM ** -0.5), w_c[:, nq:]], axis=1)
    uq = w_uq.reshape(A_Q_LORA, A_HEADS, A_NOPE + A_ROPE)
    z32 = jnp.zeros((A_Q_LORA, A_HEADS, LANES - A_NOPE - A_ROPE), F32)
    wq = jnp.concatenate([uq, z32], axis=-1).reshape(A_Q_LORA, A_HEADS * LANES)
    wqr = jnp.concatenate([jnp.zeros((A_Q_LORA, A_HEADS, A_NOPE), F32), _rot_cols(uq[..., A_NOPE:]), z32],
                          axis=-1).reshape(A_Q_LORA, A_HEADS * LANES)
    ukv = w_ukv.reshape(A_KV_LORA, A_HEADS, A_NOPE + A_V)
    wk = jnp.concatenate([ukv[..., :A_NOPE], jnp.zeros((A_KV_LORA, A_HEADS, LANES - A_NOPE), F32)],
                         axis=-1).reshape(A_KV_LORA, A_HEADS * LANES)
    wv = ukv[..., A_NOPE:].reshape(A_KV_LORA, A_HEADS * A_V)
    w_br_c = w_br_c.reshape(2, C_HEADS // 2, C_HEAD_DIM, D_MODEL).transpose(1, 0, 2, 3).reshape(nq, D_MODEL)
    wbr = jnp.stack([w_br_a, w_br_b, w_br_c])
    cast = lambda a: a.astype(BF16)
    return dict(wa=cast(wa), wb=cast(wb), wc=cast(wc), wg=cast(w_g), wq=cast(wq), wqr=cast(wqr), wk=cast(wk),
                wv=cast(wv), wbr=cast(wbr))


def _rope_tables():
    half = A_ROPE // 2
    inv = ROPE_THETA ** (-jnp.arange(half, dtype=F32) / half)
    ang = jnp.arange(SEQ, dtype=F32)[:, None] * inv[None, :]
    cos2 = jnp.tile(jnp.cos(ang), (1, 2))
    sin2 = jnp.tile(jnp.sin(ang), (1, 2))
    ones = jnp.ones((SEQ, A_NOPE), F32)
    z = lambda w: jnp.zeros((SEQ, w), F32)
    cq = jnp.concatenate([ones, cos2, z(LANES - A_NOPE - A_ROPE)], axis=1)
    sq = jnp.concatenate([z(A_NOPE), sin2, z(LANES - A_NOPE - A_ROPE)], axis=1)
    ck = jnp.concatenate([cos2, z(LANES - A_ROPE)], axis=1)
    sk = jnp.concatenate([sin2, z(LANES - A_ROPE)], axis=1)
    return cq, sq, ck, sk


def _rope_placement():
    place = np.zeros((LANES, A_HEADS * LANES), np.float32)
    for h in range(A_HEADS):
        for j in range(A_ROPE):
            place[j, h * LANES + A_NOPE + j] = 1.0
    return jnp.asarray(place, dtype=BF16)


def kernel(x, norm1_g, w_in, q_norm_g, w_uq, kv_norm_g, w_ukv, sink_logit, rel_bias, w_branch_a, w_branch_b,
           w_branch_c, w_out, norm2_g, ffn_w_gate, ffn_w_up, ffn_w_down, router_w, exp_w_gate, exp_w_up,
           exp_w_down, final_g):
    batch, seq, d = x.shape
    assert seq == SEQ and d == D_MODEL
    t = batch * seq
    x2 = x.reshape(t, d)

    wide_idx, narrow_idx = _dilated_index_tables()
    bias_wide = _bias_tables(rel_bias, wide_idx, 0, B_HEADS)
    bias_narrow = _bias_tables(rel_bias, narrow_idx, 0, B_HEADS)
    bias_c = _bias_tables(rel_bias, _window_index_tables(), B_HEADS, C_HEADS)
    bias_c = bias_c.reshape(2, C_HEADS // 2, *bias_c.shape[1:]).transpose(1, 0, 2, 3, 4)

    cq, sq, ck, sk = _rope_tables()
    place = _rope_placement()
    row = lambda v: v.reshape(1, -1)
    ones_cw = jnp.ones((t, N_EXPERTS), F32)

    for l in range(DEPTH):
        w = _prep_layer(w_in[l], w_uq[l], w_ukv[l], w_branch_a[l], w_branch_b[l], w_branch_c[l])
        g1 = row(norm1_g[l])
        qa, ka, va, zb, zc = _in_proj(x2, g1, w["wa"], w["wb"], w["wc"], row(q_norm_g[l]), row(kv_norm_g[l]),
                                      w["wq"], w["wqr"], w["wk"], w["wv"], place, cq, sq, ck, sk)
        ya = _mla_attention(qa, ka, va, batch)
        yb = _dilated_attention(zb, bias_wide, bias_narrow, batch)
        yc = _window_attention(zc, sink_logit[l], bias_c, batch)
        x2 = _merge(x2, g1, ya, yb, yc, w["wg"], w["wbr"], w_out[l].astype(BF16))
        g2 = row(norm2_g[l])
        final = l == DEPTH - 1
        i = l // 2
        if l % 2 == 0:
            x2 = _ffn(x2, g2, ones_cw, ffn_w_gate[i].astype(BF16)[None], ffn_w_up[i].astype(BF16)[None],
                      ffn_w_down[i].astype(BF16)[None], row(final_g), tf=D_FF_DENSE // 2, final=final)
        else:
            cw = _router(x2, g2, router_w[i])
            x2 = _ffn(x2, g2, cw, exp_w_gate[i].astype(BF16), exp_w_up[i].astype(BF16),
                      exp_w_down[i].astype(BF16), row(final_g), tf=D_FF_EXPERT // 4, final=final)
    return x2.reshape(batch, seq, d)
```

```python
import functools
import math

import numpy as np
import jax
import jax.numpy as jnp
from jax import lax
from jax.experimental import pallas as pl
from jax.experimental.pallas import tpu as pltpu

F32 = jnp.float32
BF16 = jnp.bfloat16

D_MODEL = 1024
SEQ = 2048
DEPTH = 2
A_HEADS = 8
A_NOPE = 64
A_ROPE = 32
A_V = 64
A_Q_LORA = 256
A_KV_LORA = 128
ROPE_THETA = 10000.0
B_HEADS = 8
B_HEAD_DIM = 64
B_GROUPS = ((128, 1), (512, 4), (2048, 16))
C_HEADS = 8
C_KV_HEADS = 2
C_HEAD_DIM = 64
C_RADIUS = 128
N_BUCKETS = 32
MAX_DISTANCE = 1024
D_FF_DENSE = 2816
N_EXPERTS = 8
D_FF_EXPERT = 3584
EPS = 1e-6
NEG_INF = -1e30

LANES = 128
HALF = 64
QBLK = 128
B_RADIUS = 64
VMEM_LIMIT = 56 * 1024 * 1024

A_SCALE = (A_NOPE + A_ROPE) ** -0.5
ZA_COLS = A_Q_LORA + A_KV_LORA + 2 * LANES


def _cparams(sem):
    return pltpu.CompilerParams(dimension_semantics=sem, vmem_limit_bytes=VMEM_LIMIT)


def _rms(x, g):
    return x * lax.rsqrt(jnp.mean(x * x, axis=-1, keepdims=True) + EPS) * g


def _dot(a, b):
    return jnp.dot(a, b, preferred_element_type=F32)


def _dot_nt(a, b):
    return lax.dot_general(a, b, (((1,), (1,)), ((), ())), preferred_element_type=F32)


def _t5_bucket_np(rel):
    nb = N_BUCKETS // 2
    max_exact = nb // 2
    ret = np.where(rel > 0, nb, 0)
    n = np.abs(rel)
    nf = np.maximum(n, 1).astype(np.float64)
    t = np.log(nf / max_exact) / math.log(MAX_DISTANCE / max_exact) * (nb - max_exact)
    frac = np.abs(t - np.round(t))
    assert np.all((frac > 1e-5) | (n <= max_exact) | (n >= MAX_DISTANCE))
    large = max_exact + np.floor(np.round(t, 9)).astype(np.int64)
    large = np.minimum(large, nb - 1)
    return (ret + np.where(n < max_exact, n, large)).astype(np.int32)


def _band_index(tq, tk, shift, radius, dil):
    rel = np.arange(tk)[None, :] - np.arange(tq)[:, None] + shift
    idx = _t5_bucket_np(rel * dil)
    return np.where(np.abs(rel) <= radius, idx, -1).astype(np.int32)


def _dilated_index_tables():
    wide = []
    for window, dil in B_GROUPS[:2]:
        for shift in (0, -B_RADIUS, -2 * B_RADIUS):
            wide.append(_band_index(QBLK, 2 * QBLK, shift, B_RADIUS, dil))
    narrow = [_band_index(QBLK, QBLK, 0, B_RADIUS, B_GROUPS[2][1])]
    return np.stack(wide), np.stack(narrow)


def _window_index_tables():
    return np.stack([_band_index(QBLK, 3 * QBLK, shift, C_RADIUS, 1) for shift in (0, -QBLK, -2 * QBLK)])


def _bias_kernel(tab_ref, idx_ref, o_ref, *, col0):
    col = pl.program_id(0) + col0
    idx = idx_ref[...]
    acc = jnp.full(idx.shape, NEG_INF, F32)
    for b in range(N_BUCKETS):
        acc = jnp.where(idx == b, tab_ref[b, col], acc)
    o_ref[...] = acc


def _bias_tables(rel_bias, idx_np, col0, n_heads):
    nv, r, c = idx_np.shape
    return pl.pallas_call(
        functools.partial(_bias_kernel, col0=col0),
        out_shape=jax.ShapeDtypeStruct((n_heads, nv, r, c), F32),
        grid=(n_heads, nv),
        in_specs=[pl.BlockSpec(memory_space=pltpu.SMEM),
                  pl.BlockSpec((None, r, c), lambda h, v: (v, 0, 0))],
        out_specs=pl.BlockSpec((None, None, r, c), lambda h, v: (h, v, 0, 0)),
        compiler_params=_cparams(("arbitrary", "arbitrary")),
        name="bias_tables",
    )(rel_bias, jnp.asarray(idx_np))


IN_TM = 512


def _in_proj_kernel(x_ref, g1_ref, wa_ref, wb_ref, wc_ref, qg_ref, kvg_ref, wq_ref, wqr_ref, wk_ref, wv_ref,
                    place_ref, cq_ref, sq_ref, ck_ref, sk_ref, qa_ref, ka_ref, va_ref, zb_ref, zc_ref):
    h = _rms(x_ref[...], g1_ref[...]).astype(BF16)
    zb_ref[...] = _dot(h, wb_ref[...]).astype(BF16)
    zc_ref[...] = _dot(h, wc_ref[...]).astype(BF16)
    za = _dot(h, wa_ref[...])
    hq = _rms(za[:, :A_Q_LORA], qg_ref[...]).astype(BF16)
    hkv = _rms(za[:, A_Q_LORA:A_Q_LORA + A_KV_LORA], kvg_ref[...]).astype(BF16)
    kpe = za[:, A_Q_LORA + A_KV_LORA:A_Q_LORA + A_KV_LORA + LANES]
    kpe_rot = za[:, A_Q_LORA + A_KV_LORA + LANES:]
    q = _dot(hq, wq_ref[...]) * A_SCALE
    q_rot = _dot(hq, wqr_ref[...]) * A_SCALE
    k_rope = (kpe * ck_ref[...] + kpe_rot * sk_ref[...]).astype(BF16)
    k = _dot(hkv, wk_ref[...]) + _dot(k_rope, place_ref[...])
    ka_ref[...] = k.astype(BF16)
    va_ref[...] = _dot(hkv, wv_ref[...]).astype(BF16)
    cq = cq_ref[...]
    sq = sq_ref[...]
    for hh in range(A_HEADS):
        sl = slice(hh * LANES, (hh + 1) * LANES)
        qa_ref[:, sl] = (q[:, sl] * cq + q_rot[:, sl] * sq).astype(BF16)


def _in_proj(x2, g1, wa, wb, wc, qg, kvg, wq, wqr, wk, wv, place, cq, sq, ck, sk):
    t = x2.shape[0]
    tm = IN_TM
    n_pos = SEQ // tm
    full = lambda a: pl.BlockSpec(a.shape, lambda i: (0,) * a.ndim)
    row = lambda w: pl.BlockSpec((tm, w), lambda i: (i, 0))
    pos = pl.BlockSpec((tm, LANES), lambda i: (i % n_pos, 0))
    return pl.pallas_call(
        _in_proj_kernel,
        out_shape=(jax.ShapeDtypeStruct((t, A_HEADS * LANES), BF16),
                   jax.ShapeDtypeStruct((t, A_HEADS * LANES), BF16),
                   jax.ShapeDtypeStruct((t, A_HEADS * A_V), BF16),
                   jax.ShapeDtypeStruct((t, wb.shape[1]), BF16),
                   jax.ShapeDtypeStruct((t, wc.shape[1]), BF16)),
        grid=(t // tm,),
        in_specs=[row(D_MODEL), full(g1), full(wa), full(wb), full(wc), full(qg), full(kvg), full(wq), full(wqr),
                  full(wk), full(wv), full(place), pos, pos, pos, pos],
        out_specs=(row(A_HEADS * LANES), row(A_HEADS * LANES), row(A_HEADS * A_V), row(wb.shape[1]),
                   row(wc.shape[1])),
        compiler_params=_cparams(("parallel",)),
        name="in_proj",
    )(x2, g1, wa, wb, wc, qg, kvg, wq, wqr, wk, wv, place, cq, sq, ck, sk)


A_TQ = 512


def _mla_kernel(q_ref, k_ref, v_ref, o_ref):
    v = v_ref[...]
    outs = []
    for hh in range(2):
        sl = slice(hh * LANES, (hh + 1) * LANES)
        s = _dot_nt(q_ref[:, sl], k_ref[:, sl])
        m = jnp.max(s, axis=-1, keepdims=True)
        p = jnp.exp(s - m)
        l = jnp.sum(p, axis=-1, keepdims=True)
        outs.append(_dot(p.astype(BF16), v) / l)
    lane = lax.broadcasted_iota(jnp.int32, (1, LANES), 1)
    o_ref[...] = jnp.where(lane < HALF, outs[0], outs[1]).astype(BF16)


def _mla_attention(qa, ka, va, batch):
    t = qa.shape[0]
    nq = SEQ // A_TQ
    return pl.pallas_call(
        _mla_kernel,
        out_shape=jax.ShapeDtypeStruct((t, A_HEADS * A_V), BF16),
        grid=(batch, A_HEADS // 2, nq),
        in_specs=[pl.BlockSpec((A_TQ, 2 * LANES), lambda b, p, i: (b * nq + i, p)),
                  pl.BlockSpec((SEQ, 2 * LANES), lambda b, p, i: (b, p)),
                  pl.BlockSpec((SEQ, LANES), lambda b, p, i: (b, p))],
        out_specs=pl.BlockSpec((A_TQ, LANES), lambda b, p, i: (b * nq + i, p)),
        compiler_params=_cparams(("parallel", "parallel", "arbitrary")),
        name="mla_attention",
    )(qa, ka, va)


def _pair_block(q2, k2, v2, biases, sinks=None):
    lane = lax.broadcasted_iota(jnp.int32, (1, LANES), 1)
    first = lane < HALF
    zero = jnp.zeros_like(q2)
    res = []
    for hh in range(2):
        qh = jnp.where(first if hh == 0 else jnp.logical_not(first), q2, zero)
        s = _dot_nt(qh, k2) + biases[hh]
        m = jnp.max(s, axis=-1, keepdims=True)
        if sinks is not None:
            m = jnp.maximum(m, sinks[hh])
        p = jnp.exp(s - m)
        l = jnp.sum(p, axis=-1, keepdims=True)
        if sinks is not None:
            l = l + jnp.exp(sinks[hh] - m)
        res.append((m, l, _dot(p.astype(BF16), v2)))
    return tuple(jnp.where(first, res[0][i], res[1][i]) for i in range(3))


def _dilated_kernel(q_ref, k_ref, v_ref, bw_ref, bn_ref, o_ref,
                    qf, kf, vf, qs, ks, vs, ms, ls, os_, m_all, l_all, o_all):
    qf[...] = q_ref[...].astype(F32)
    kf[...] = k_ref[...].astype(F32)
    vf[...] = v_ref[...].astype(F32)
    n_blocks = SEQ // QBLK
    for gi, (_, dil) in enumerate(B_GROUPS):
        length = SEQ // dil
        per_stream = length // QBLK
        for g in range(dil):
            rows = pl.ds(g, length, stride=dil) if dil > 1 else slice(None)
            dst = slice(g * length, (g + 1) * length)
            qs[dst, :] = qf[rows, :].astype(BF16)
            ks[dst, :] = kf[rows, :].astype(BF16)
            vs[dst, :] = vf[rows, :].astype(BF16)

        for j in range(n_blocks):
            rows = slice(j * QBLK, (j + 1) * QBLK)
            if per_stream == 1:
                keys = rows
                biases = (bn_ref[0, 0], bn_ref[1, 0])
            else:
                i = j % per_stream
                start = (j - i) * QBLK + min(max(i * QBLK - B_RADIUS, 0), length - 2 * QBLK)
                keys = slice(start, start + 2 * QBLK)
                var = gi * 3 + (0 if i == 0 else 2 if i == per_stream - 1 else 1)
                biases = (bw_ref[0, var], bw_ref[1, var])
            m, l, o = _pair_block(qs[rows, :], ks[keys, :], vs[keys, :], biases)
            ms[rows, :] = m
            ls[rows, :] = l
            os_[rows, :] = o
        for g in range(dil):
            rows = pl.ds(g, length, stride=dil) if dil > 1 else slice(None)
            src = slice(g * length, (g + 1) * length)
            m_all[gi, rows, :] = ms[src, :]
            l_all[gi, rows, :] = ls[src, :]
            o_all[gi, rows, :] = os_[src, :]

    def merge(c, carry):
        rows = pl.ds(pl.multiple_of(c * QBLK, QBLK), QBLK)
        m = [m_all[gi, rows, :] for gi in range(3)]
        top = jnp.maximum(jnp.maximum(m[0], m[1]), m[2])
        w = [jnp.exp(mg - top) for mg in m]
        num = w[0] * o_all[0, rows, :] + w[1] * o_all[1, rows, :] + w[2] * o_all[2, rows, :]
        den = w[0] * l_all[0, rows, :] + w[1] * l_all[1, rows, :] + w[2] * l_all[2, rows, :]
        o_ref[rows, :] = (num / den).astype(BF16)
        return carry

    lax.fori_loop(0, n_blocks, merge, 0)


def _dilated_attention(zb, bias_wide, bias_narrow, batch):
    t = zb.shape[0]
    n_pairs = B_HEADS // 2
    seq_block = lambda off: pl.BlockSpec((SEQ, LANES), lambda b, p: (b, off + p))
    nw = bias_wide.shape[1]
    return pl.pallas_call(
        _dilated_kernel,
        out_shape=jax.ShapeDtypeStruct((t, B_HEADS * B_HEAD_DIM), BF16),
        grid=(batch, n_pairs),
        in_specs=[seq_block(0), seq_block(n_pairs), seq_block(2 * n_pairs),
                  pl.BlockSpec((2, nw, QBLK, 2 * QBLK), lambda b, p: (p, 0, 0, 0)),
                  pl.BlockSpec((2, 1, QBLK, QBLK), lambda b, p: (p, 0, 0, 0))],
        out_specs=pl.BlockSpec((SEQ, LANES), lambda b, p: (b, p)),
        scratch_shapes=[pltpu.VMEM((SEQ, LANES), F32)] * 3 + [pltpu.VMEM((SEQ, LANES), BF16)] * 3
        + [pltpu.VMEM((SEQ, LANES), F32)] * 3 + [pltpu.VMEM((3, SEQ, LANES), F32)] * 3,
        compiler_params=_cparams(("parallel", "arbitrary")),
        name="dilated_attention",
    )(zb, zb, zb, bias_wide, bias_narrow)


def _window_kernel(sink_ref, q_ref, k_ref, v_ref, bias_ref, o_ref):
    pair = pl.program_id(1)
    sinks = (sink_ref[pair], sink_ref[pair + C_HEADS // 2])
    n_blocks = SEQ // QBLK

    for i in range(n_blocks):
        rows = slice(i * QBLK, (i + 1) * QBLK)
        start = min(max((i - 1) * QBLK, 0), SEQ - 3 * QBLK)
        var = 0 if i == 0 else 2 if i == n_blocks - 1 else 1
        keys = slice(start, start + 3 * QBLK)
        _, l, o = _pair_block(q_ref[rows, :], k_ref[keys, :], v_ref[keys, :],
                              (bias_ref[0, var], bias_ref[1, var]), sinks)
        o_ref[rows, :] = (o / l).astype(BF16)


def _window_attention(zc, sink, bias, batch):
    t = zc.shape[0]
    n_pairs = C_HEADS // 2
    return pl.pallas_call(
        _window_kernel,
        out_shape=jax.ShapeDtypeStruct((t, C_HEADS * C_HEAD_DIM), BF16),
        grid=(batch, n_pairs),
        in_specs=[pl.BlockSpec(memory_space=pltpu.SMEM),
                  pl.BlockSpec((SEQ, LANES), lambda b, p: (b, p)),
                  pl.BlockSpec((SEQ, LANES), lambda b, p: (b, n_pairs)),
                  pl.BlockSpec((SEQ, LANES), lambda b, p: (b, n_pairs + 1)),
                  pl.BlockSpec((None, 2, 3, QBLK, 3 * QBLK), lambda b, p: (p, 0, 0, 0, 0))],
        out_specs=pl.BlockSpec((SEQ, LANES), lambda b, p: (b, p)),
        compiler_params=_cparams(("parallel", "arbitrary")),
        name="window_attention",
    )(sink, zc, zc, zc, bias)


MERGE_TM = 512


def _merge_kernel(x_ref, g1_ref, ya_ref, yb_ref, yc_ref, wg_ref, wbr_ref, wo_ref, o_ref):
    x = x_ref[...]
    h = _rms(x, g1_ref[...]).astype(BF16)
    merged = None
    for b, y_ref in enumerate((ya_ref, yb_ref, yc_ref)):
        gate = jax.nn.sigmoid(_dot(h, wg_ref[:, b * D_MODEL:(b + 1) * D_MODEL]))
        term = gate * _dot(y_ref[...], wbr_ref[b])
        merged = term if merged is None else merged + term
    o_ref[...] = x + _dot(merged.astype(BF16), wo_ref[...])


def _merge(x2, g1, ya, yb, yc, wg, wbr, wo):
    t = x2.shape[0]
    tm = MERGE_TM
    full = lambda a: pl.BlockSpec(a.shape, lambda i: (0,) * a.ndim)
    row = lambda w: pl.BlockSpec((tm, w), lambda i: (i, 0))
    return pl.pallas_call(
        _merge_kernel,
        out_shape=jax.ShapeDtypeStruct((t, D_MODEL), F32),
        grid=(t // tm,),
        in_specs=[row(D_MODEL), full(g1), row(ya.shape[1]), row(yb.shape[1]), row(yc.shape[1]),
                  full(wg), full(wbr), full(wo)],
        out_specs=row(D_MODEL),
        compiler_params=_cparams(("parallel",)),
        name="gate_merge",
    )(x2, g1, ya, yb, yc, wg, wbr, wo)


TOP_K = 2
ROW_SUBLANES = D_MODEL // LANES
MOE_TM = 512
EXPERT_TF = 896


def _router_kernel(x_ref, g_ref, wr_ref, eid_ref, rank_ref, gate_ref, cnt_ref, carry):
    @pl.when(pl.program_id(0) == 0)
    def _():
        carry[...] = jnp.zeros_like(carry)

    h = _rms(x_ref[...], g_ref[...])
    logits = jnp.dot(h, wr_ref[...], preferred_element_type=F32, precision=lax.Precision.HIGHEST)
    idx = lax.broadcasted_iota(jnp.int32, logits.shape, 1)
    m1 = jnp.max(logits, axis=-1, keepdims=True)
    i1 = jnp.min(jnp.where(logits == m1, idx, N_EXPERTS), axis=-1, keepdims=True)
    first = idx == i1
    rest = jnp.where(first, -jnp.inf, logits)
    m2 = jnp.max(rest, axis=-1, keepdims=True)
    i2 = jnp.min(jnp.where(rest == m2, idx, N_EXPERTS), axis=-1, keepdims=True)
    second = idx == i2
    e = jnp.exp(m2 - m1)
    den = 1.0 + e
    chosen = jnp.where(jnp.logical_or(first, second), 1.0, 0.0)
    tm = chosen.shape[0]
    earlier = (lax.broadcasted_iota(jnp.int32, (tm, tm), 0) > lax.broadcasted_iota(jnp.int32, (tm, tm), 1))
    before = _dot(jnp.where(earlier, 1.0, 0.0).astype(BF16), chosen.astype(BF16)) + carry[...]
    r1 = jnp.sum(jnp.where(first, before, 0.0), axis=-1, keepdims=True)
    r2 = jnp.sum(jnp.where(second, before, 0.0), axis=-1, keepdims=True)
    carry[...] += jnp.sum(chosen, axis=0, keepdims=True)
    slot0 = lax.broadcasted_iota(jnp.int32, (tm, TOP_K), 1) == 0
    eid_ref[...] = jnp.where(slot0, i1, i2)
    rank_ref[...] = jnp.where(slot0, r1, r2).astype(jnp.int32)
    gate_ref[...] = jnp.where(slot0, 1.0 / den, e / den)
    cnt_ref[...] = carry[...].astype(jnp.int32)


def _router(x2, g, wr):
    t = x2.shape[0]
    tm = MOE_TM
    pair = pl.BlockSpec((tm, TOP_K), lambda i: (i, 0))
    return pl.pallas_call(
        _router_kernel,
        out_shape=(jax.ShapeDtypeStruct((t, TOP_K), jnp.int32), jax.ShapeDtypeStruct((t, TOP_K), jnp.int32),
                   jax.ShapeDtypeStruct((t, TOP_K), F32), jax.ShapeDtypeStruct((1, N_EXPERTS), jnp.int32)),
        grid=(t // tm,),
        in_specs=[pl.BlockSpec((tm, D_MODEL), lambda i: (i, 0)),
                  pl.BlockSpec(g.shape, lambda i: (0, 0)),
                  pl.BlockSpec(wr.shape, lambda i: (0, 0))],
        out_specs=(pair, pair, pair, pl.BlockSpec((1, N_EXPERTS), lambda i: (0, 0))),
        scratch_shapes=[pltpu.VMEM((1, N_EXPERTS), F32)],
        compiler_params=_cparams(("arbitrary",)),
        name="router",
    )(x2, g, wr)


def _row_tile(ref, row):
    return ref.at[pl.ds(pl.multiple_of(row * ROW_SUBLANES, ROW_SUBLANES), ROW_SUBLANES), :]


def _dispatch_kernel(tail_start_ref, tail_len_ref, na_ref, dest_ref, x_ref, g_ref, hs_ref, stage, sem):
    tm = x_ref.shape[0]
    h = _rms(x_ref[...], g_ref[...])
    for s in range(ROW_SUBLANES):
        stage[pl.ds(s, tm, stride=ROW_SUBLANES), :] = h[:, s * LANES:(s + 1) * LANES]

    def copy(t, k):
        return pltpu.make_async_copy(_row_tile(stage, t), _row_tile(hs_ref, dest_ref[TOP_K * t + k]), sem)

    def start(t, carry):
        for k in range(TOP_K):
            copy(t, k).start()
        return carry

    def wait(t, carry):
        for k in range(TOP_K):
            copy(t, k).wait()
        return carry

    lax.fori_loop(0, tm, start, 0, unroll=8)
    lax.fori_loop(0, tm, wait, 0, unroll=8)

    @pl.when(pl.program_id(0) == pl.num_programs(0) - 1)
    def _():
        stage[...] = jnp.zeros_like(stage)
        for e in range(N_EXPERTS):
            def tail(r, e=e):
                return pltpu.make_async_copy(_row_tile(stage, 0), _row_tile(hs_ref, tail_start_ref[e] + r), sem)

            lax.fori_loop(0, tail_len_ref[e], lambda r, c: (tail(r).start(), c)[1], 0)
            lax.fori_loop(0, tail_len_ref[e], lambda r, c: (tail(r).wait(), c)[1], 0)
        n_tiles = hs_ref.shape[0] // stage.shape[0]

        def spare(i):
            rows = pl.ds(pl.multiple_of(i * stage.shape[0], stage.shape[0]), stage.shape[0])
            return pltpu.make_async_copy(stage, hs_ref.at[rows, :], sem)

        lax.fori_loop(na_ref[0], n_tiles, lambda i, c: (spare(i).start(), c)[1], 0)
        lax.fori_loop(na_ref[0], n_tiles, lambda i, c: (spare(i).wait(), c)[1], 0)


def _dispatch(x2, g, dest, tail_start, tail_len, n_active, n_rows):
    t = x2.shape[0]
    tm = MOE_TM
    return pl.pallas_call(
        _dispatch_kernel,
        out_shape=jax.ShapeDtypeStruct((n_rows * ROW_SUBLANES, LANES), F32),
        grid_spec=pltpu.PrefetchScalarGridSpec(
            num_scalar_prefetch=3,
            grid=(t // tm,),
            in_specs=[pl.BlockSpec((TOP_K * tm,), lambda i, ts, tl, na: (i,), memory_space=pltpu.SMEM),
                      pl.BlockSpec((tm, D_MODEL), lambda i, ts, tl, na: (i, 0)),
                      pl.BlockSpec(g.shape, lambda i, ts, tl, na: (0, 0))],
            out_specs=pl.BlockSpec(memory_space=pl.ANY),
            scratch_shapes=[pltpu.VMEM((tm * ROW_SUBLANES, LANES), F32), pltpu.SemaphoreType.DMA]),
        compiler_params=_cparams(("arbitrary",)),
        name="dispatch",
    )(tail_start, tail_len, n_active, dest, x2, g)


def _experts_kernel(te_ref, na_ref, hs_ref, wg_ref, wu_ref, wd_ref, ys_ref, h_scr, acc_scr):
    i = pl.program_id(0)
    f = pl.program_id(1)
    active = i < na_ref[0]
    tm = h_scr.shape[0]

    @pl.when(jnp.logical_and(active, f == 0))
    def _():
        for s in range(ROW_SUBLANES):
            h_scr[:, s * LANES:(s + 1) * LANES] = hs_ref[pl.ds(s, tm, stride=ROW_SUBLANES), :].astype(BF16)

    @pl.when(active)
    def _():
        h = h_scr[...]
        act = jax.nn.silu(_dot(h, wg_ref[...])) * _dot(h, wu_ref[...])
        part = _dot(act.astype(BF16), wd_ref[...])

        @pl.when(f == 0)
        def _():
            acc_scr[...] = part

        @pl.when(f != 0)
        def _():
            acc_scr[...] += part

    @pl.when(jnp.logical_and(active, f == pl.num_programs(1) - 1))
    def _():
        for s in range(ROW_SUBLANES):
            ys_ref[pl.ds(s, tm, stride=ROW_SUBLANES), :] = acc_scr[:, s * LANES:(s + 1) * LANES]

    @pl.when(jnp.logical_and(jnp.logical_not(active), f == pl.num_programs(1) - 1))
    def _():
        ys_ref[...] = jnp.zeros_like(ys_ref)


def _experts(hs, tile_expert, n_active, wg, wu, wd):
    tm = MOE_TM
    tf = EXPERT_TF
    n_tiles = hs.shape[0] // (tm * ROW_SUBLANES)
    n_f = wg.shape[2] // tf
    rows = lambda i, f, te, na: (jnp.minimum(i, na[0] - 1), 0)
    col = lambda i, f, na: jnp.where(i < na[0], f, n_f - 1)
    return pl.pallas_call(
        _experts_kernel,
        out_shape=jax.ShapeDtypeStruct(hs.shape, F32),
        grid_spec=pltpu.PrefetchScalarGridSpec(
            num_scalar_prefetch=2,
            grid=(n_tiles, n_f),
            in_specs=[pl.BlockSpec((tm * ROW_SUBLANES, LANES), rows),
                      pl.BlockSpec((None, D_MODEL, tf), lambda i, f, te, na: (te[i], 0, col(i, f, na))),
                      pl.BlockSpec((None, D_MODEL, tf), lambda i, f, te, na: (te[i], 0, col(i, f, na))),
                      pl.BlockSpec((None, tf, D_MODEL), lambda i, f, te, na: (te[i], col(i, f, na), 0))],
            out_specs=pl.BlockSpec((tm * ROW_SUBLANES, LANES), lambda i, f, te, na: (i, 0)),
            scratch_shapes=[pltpu.VMEM((tm, D_MODEL), BF16), pltpu.VMEM((tm, D_MODEL), F32)]),
        compiler_params=_cparams(("arbitrary", "arbitrary")),
        name="experts",
    )(tile_expert, n_active, hs, wg, wu, wd)


def _combine_kernel(dest_ref, x_ref, gate_ref, gf_ref, ys_ref, o_ref, buf0, buf1, sem, *, final):
    tm = x_ref.shape[0]
    bufs = (buf0, buf1)

    def copy(t, k):
        return pltpu.make_async_copy(_row_tile(ys_ref, dest_ref[TOP_K * t + k]), _row_tile(bufs[k], t), sem)

    def start(t, carry):
        for k in range(TOP_K):
            copy(t, k).start()
        return carry

    def wait(t, carry):
        for k in range(TOP_K):
            copy(t, k).wait()
        return carry

    lax.fori_loop(0, tm, start, 0, unroll=8)
    lax.fori_loop(0, tm, wait, 0, unroll=8)
    gate = gate_ref[...]
    g0 = gate[:, 0:1]
    g1 = gate[:, 1:2]
    ss = jnp.zeros((tm, 1), F32)
    for s in range(ROW_SUBLANES):
        sl = slice(s * LANES, (s + 1) * LANES)
        rows = pl.ds(s, tm, stride=ROW_SUBLANES)
        y = x_ref[:, sl] + (g0 * buf0[rows, :] + g1 * buf1[rows, :])
        o_ref[:, sl] = y
        ss = ss + jnp.sum(y * y, axis=-1, keepdims=True)
    if final:
        o_ref[...] = o_ref[...] * lax.rsqrt(ss / D_MODEL + EPS) * gf_ref[...]


def _combine(x2, dest, gates, gf, ys, *, final):
    t = x2.shape[0]
    tm = MOE_TM
    return pl.pallas_call(
        functools.partial(_combine_kernel, final=final),
        out_shape=jax.ShapeDtypeStruct((t, D_MODEL), F32),
        grid=(t // tm,),
        in_specs=[pl.BlockSpec((TOP_K * tm,), lambda i: (i,), memory_space=pltpu.SMEM),
                  pl.BlockSpec((tm, D_MODEL), lambda i: (i, 0)),
                  pl.BlockSpec((tm, TOP_K), lambda i: (i, 0)),
                  pl.BlockSpec(gf.shape, lambda i: (0, 0)),
                  pl.BlockSpec(memory_space=pl.ANY)],
        out_specs=pl.BlockSpec((tm, D_MODEL), lambda i: (i, 0)),
        scratch_shapes=[pltpu.VMEM((tm * ROW_SUBLANES, LANES), F32), pltpu.VMEM((tm * ROW_SUBLANES, LANES), F32),
                        pltpu.SemaphoreType.DMA],
        compiler_params=_cparams(("arbitrary",)),
        name="combine",
    )(dest, x2, gates, gf, ys)


def _moe(x2, g, wr, wg, wu, wd, gf, *, final):
    t = x2.shape[0]
    tm = MOE_TM
    eid, rank, gates, counts = _router(x2, g, wr)
    counts = counts[0]
    padded = (counts + tm - 1) // tm * tm
    group_end = jnp.cumsum(padded)
    group_start = group_end - padded
    dest = (group_start[eid] + rank).reshape(-1)
    n_tiles = TOP_K * t // tm + N_EXPERTS
    n_active = (group_end[-1:] // tm).astype(jnp.int32)
    tile_row = jnp.arange(n_tiles, dtype=jnp.int32) * tm
    tile_row = jnp.minimum(tile_row, group_end[-1] - tm)
    tile_expert = jnp.sum(tile_row[:, None] >= group_end[None, :], axis=1).astype(jnp.int32)
    hs = _dispatch(x2, g, dest, group_start + counts, padded - counts, n_active, n_tiles * tm)
    ys = _experts(hs, tile_expert, n_active, wg, wu, wd)
    return _combine(x2, dest, gates, gf, ys, final=final)


FFN_TM = 512
FFN_TF = D_FF_DENSE // 2


def _ffn_kernel(x_ref, g_ref, wg_ref, wu_ref, wd_ref, gf_ref, o_ref, h_scr, acc_scr, *, final):
    f = pl.program_id(1)

    @pl.when(f == 0)
    def _():
        h_scr[...] = _rms(x_ref[...], g_ref[...]).astype(BF16)

    h = h_scr[...]
    act = jax.nn.silu(_dot(h, wg_ref[...])) * _dot(h, wu_ref[...])
    part = _dot(act.astype(BF16), wd_ref[...])

    @pl.when(f == 0)
    def _():
        acc_scr[...] = part

    @pl.when(f != 0)
    def _():
        acc_scr[...] += part

    @pl.when(f == pl.num_programs(1) - 1)
    def _():
        y = x_ref[...] + acc_scr[...]
        if final:
            y = _rms(y, gf_ref[...])
        o_ref[...] = y


def _ffn(x2, g, wg, wu, wd, gf, *, final):
    t = x2.shape[0]
    tm = FFN_TM
    tf = FFN_TF
    d_ff = wg.shape[1]
    return pl.pallas_call(
        functools.partial(_ffn_kernel, final=final),
        out_shape=jax.ShapeDtypeStruct((t, D_MODEL), F32),
        grid=(t // tm, d_ff // tf),
        in_specs=[pl.BlockSpec((tm, D_MODEL), lambda i, f: (i, 0)),
                  pl.BlockSpec(g.shape, lambda i, f: (0, 0)),
                  pl.BlockSpec((D_MODEL, tf), lambda i, f: (0, f)),
                  pl.BlockSpec((D_MODEL, tf), lambda i, f: (0, f)),
                  pl.BlockSpec((tf, D_MODEL), lambda i, f: (f, 0)),
                  pl.BlockSpec(gf.shape, lambda i, f: (0, 0))],
        out_specs=pl.BlockSpec((tm, D_MODEL), lambda i, f: (i, 0)),
        scratch_shapes=[pltpu.VMEM((tm, D_MODEL), BF16), pltpu.VMEM((tm, D_MODEL), F32)],
        compiler_params=_cparams(("parallel", "arbitrary")),
        name="swiglu",
    )(x2, g, wg, wu, wd, gf)


def _rot_cols(w):
    half = A_ROPE // 2
    return jnp.concatenate([-w[..., half:], w[..., :half]], axis=-1)


def _prep_layer(w_in, w_uq, w_ukv, w_br_a, w_br_b, w_br_c):
    o = 0
    w_cq = w_in[:, o:o + A_Q_LORA]; o += A_Q_LORA
    w_ckv = w_in[:, o:o + A_KV_LORA]; o += A_KV_LORA
    w_kpe = w_in[:, o:o + A_ROPE]; o += A_ROPE
    nb = B_HEADS * B_HEAD_DIM
    w_b = w_in[:, o:o + 3 * nb]; o += 3 * nb
    nq = C_HEADS * C_HEAD_DIM
    nkv = C_KV_HEADS * C_HEAD_DIM
    w_c = w_in[:, o:o + nq + 2 * nkv]; o += nq + 2 * nkv
    w_g = w_in[:, o:]
    pad = jnp.zeros((D_MODEL, LANES - A_ROPE), F32)
    wa = jnp.concatenate([w_cq, w_ckv, w_kpe, pad, _rot_cols(w_kpe), pad], axis=1)
    wb = jnp.concatenate([w_b[:, :nb] * (B_HEAD_DIM ** -0.5), w_b[:, nb:]], axis=1)
    q_c = w_c[:, :nq].reshape(D_MODEL, 2, C_HEADS // 2, C_HEAD_DIM).transpose(0, 2, 1, 3).reshape(D_MODEL, nq)
    wc = jnp.concatenate([q_c * (C_HEAD_DIM ** -0.5), w_c[:, nq:]], axis=1)
    uq = w_uq.reshape(A_Q_LORA, A_HEADS, A_NOPE + A_ROPE)
    z32 = jnp.zeros((A_Q_LORA, A_HEADS, LANES - A_NOPE - A_ROPE), F32)
    wq = jnp.concatenate([uq, z32], axis=-1).reshape(A_Q_LORA, A_HEADS * LANES)
    wqr = jnp.concatenate([jnp.zeros((A_Q_LORA, A_HEADS, A_NOPE), F32), _rot_cols(uq[..., A_NOPE:]), z32],
                          axis=-1).reshape(A_Q_LORA, A_HEADS * LANES)
    ukv = w_ukv.reshape(A_KV_LORA, A_HEADS, A_NOPE + A_V)
    wk = jnp.concatenate([ukv[..., :A_NOPE], jnp.zeros((A_KV_LORA, A_HEADS, LANES - A_NOPE), F32)],
                         axis=-1).reshape(A_KV_LORA, A_HEADS * LANES)
    wv = ukv[..., A_NOPE:].reshape(A_KV_LORA, A_HEADS * A_V)
    w_br_c = w_br_c.reshape(2, C_HEADS // 2, C_HEAD_DIM, D_MODEL).transpose(1, 0, 2, 3).reshape(nq, D_MODEL)
    wbr = jnp.stack([w_br_a, w_br_b, w_br_c])
    cast = lambda a: a.astype(BF16)
    return dict(wa=cast(wa), wb=cast(wb), wc=cast(wc), wg=cast(w_g), wq=cast(wq), wqr=cast(wqr), wk=cast(wk),
                wv=cast(wv), wbr=cast(wbr))


def _rope_tables():
    half = A_ROPE // 2
    inv = ROPE_THETA ** (-jnp.arange(half, dtype=F32) / half)
    ang = jnp.arange(SEQ, dtype=F32)[:, None] * inv[None, :]
    cos2 = jnp.tile(jnp.cos(ang), (1, 2))
    sin2 = jnp.tile(jnp.sin(ang), (1, 2))
    ones = jnp.ones((SEQ, A_NOPE), F32)
    z = lambda w: jnp.zeros((SEQ, w), F32)
    cq = jnp.concatenate([ones, cos2, z(LANES - A_NOPE - A_ROPE)], axis=1)
    sq = jnp.concatenate([z(A_NOPE), sin2, z(LANES - A_NOPE - A_ROPE)], axis=1)
    ck = jnp.concatenate([cos2, z(LANES - A_ROPE)], axis=1)
    sk = jnp.concatenate([sin2, z(LANES - A_ROPE)], axis=1)
    return cq, sq, ck, sk


def _rope_placement():
    place = np.zeros((LANES, A_HEADS * LANES), np.float32)
    for h in range(A_HEADS):
        for j in range(A_ROPE):
            place[j, h * LANES + A_NOPE + j] = 1.0
    return jnp.asarray(place, dtype=BF16)


def kernel(x, norm1_g, w_in, q_norm_g, w_uq, kv_norm_g, w_ukv, sink_logit, rel_bias, w_branch_a, w_branch_b,
           w_branch_c, w_out, norm2_g, ffn_w_gate, ffn_w_up, ffn_w_down, router_w, exp_w_gate, exp_w_up,
           exp_w_down, final_g):
    batch, seq, d = x.shape
    assert seq == SEQ and d == D_MODEL
    t = batch * seq
    x2 = x.reshape(t, d)

    wide_idx, narrow_idx = _dilated_index_tables()
    bias_wide = _bias_tables(rel_bias, wide_idx, 0, B_HEADS)
    bias_narrow = _bias_tables(rel_bias, narrow_idx, 0, B_HEADS)
    bias_c = _bias_tables(rel_bias, _window_index_tables(), B_HEADS, C_HEADS)
    bias_c = bias_c.reshape(2, C_HEADS // 2, *bias_c.shape[1:]).transpose(1, 0, 2, 3, 4)

    cq, sq, ck, sk = _rope_tables()
    place = _rope_placement()
    row = lambda v: v.reshape(1, -1)

    for l in range(DEPTH):
        w = _prep_layer(w_in[l], w_uq[l], w_ukv[l], w_branch_a[l], w_branch_b[l], w_branch_c[l])
        g1 = row(norm1_g[l])
        qa, ka, va, zb, zc = _in_proj(x2, g1, w["wa"], w["wb"], w["wc"], row(q_norm_g[l]), row(kv_norm_g[l]),
                                      w["wq"], w["wqr"], w["wk"], w["wv"], place, cq, sq, ck, sk)
        ya = _mla_attention(qa, ka, va, batch)
        yb = _dilated_attention(zb, bias_wide, bias_narrow, batch)
        yc = _window_attention(zc, sink_logit[l], bias_c, batch)
        x2 = _merge(x2, g1, ya, yb, yc, w["wg"], w["wbr"], w_out[l].astype(BF16))
        g2 = row(norm2_g[l])
        final = l == DEPTH - 1
        i = l // 2
        if l % 2 == 0:
            x2 = _ffn(x2, g2, ffn_w_gate[i].astype(BF16), ffn_w_up[i].astype(BF16), ffn_w_down[i].astype(BF16),
                      row(final_g), final=final)
        else:
            x2 = _moe(x2, g2, router_w[i], exp_w_gate[i].astype(BF16), exp_w_up[i].astype(BF16),
                      exp_w_down[i].astype(BF16), row(final_g), final=final)
    return x2.reshape(batch, seq, d)
```

```python
import functools
import math

import numpy as np
import jax
import jax.numpy as jnp
from jax import lax
from jax.experimental import pallas as pl
from jax.experimental.pallas import tpu as pltpu

F32 = jnp.float32
BF16 = jnp.bfloat16

D_MODEL = 1024
SEQ = 2048
DEPTH = 2
A_HEADS = 8
A_NOPE = 64
A_ROPE = 32
A_V = 64
A_Q_LORA = 256
A_KV_LORA = 128
ROPE_THETA = 10000.0
B_HEADS = 8
B_HEAD_DIM = 64
B_GROUPS = ((128, 1), (512, 4), (2048, 16))
C_HEADS = 8
C_KV_HEADS = 2
C_HEAD_DIM = 64
C_RADIUS = 128
N_BUCKETS = 32
MAX_DISTANCE = 1024
D_FF_DENSE = 2816
N_EXPERTS = 8
D_FF_EXPERT = 3584
EPS = 1e-6
NEG_INF = -1e30

LANES = 128
HALF = 64
QBLK = 128
B_RADIUS = 64
VMEM_LIMIT = 56 * 1024 * 1024

A_SCALE = (A_NOPE + A_ROPE) ** -0.5
LOG2E = math.log2(math.e)
ZA_COLS = A_Q_LORA + A_KV_LORA + 2 * LANES


def _cparams(sem):
    return pltpu.CompilerParams(dimension_semantics=sem, vmem_limit_bytes=VMEM_LIMIT)


def _rms(x, g):
    return x * lax.rsqrt(jnp.mean(x * x, axis=-1, keepdims=True) + EPS) * g


def _dot(a, b):
    return jnp.dot(a, b, preferred_element_type=F32)


def _dot_nt(a, b):
    return lax.dot_general(a, b, (((1,), (1,)), ((), ())), preferred_element_type=F32)


def _t5_bucket_np(rel):
    nb = N_BUCKETS // 2
    max_exact = nb // 2
    ret = np.where(rel > 0, nb, 0)
    n = np.abs(rel)
    nf = np.maximum(n, 1).astype(np.float64)
    t = np.log(nf / max_exact) / math.log(MAX_DISTANCE / max_exact) * (nb - max_exact)
    frac = np.abs(t - np.round(t))
    assert np.all((frac > 1e-5) | (n <= max_exact) | (n >= MAX_DISTANCE))
    large = max_exact + np.floor(np.round(t, 9)).astype(np.int64)
    large = np.minimum(large, nb - 1)
    return (ret + np.where(n < max_exact, n, large)).astype(np.int32)


def _band_index(tq, tk, shift, radius, dil):
    rel = np.arange(tk)[None, :] - np.arange(tq)[:, None] + shift
    idx = _t5_bucket_np(rel * dil)
    return np.where(np.abs(rel) <= radius, idx, -1).astype(np.int32)


def _dilated_index_tables():
    wide = []
    for window, dil in B_GROUPS[:2]:
        for shift in (0, -B_RADIUS, -2 * B_RADIUS):
            wide.append(_band_index(QBLK, 2 * QBLK, shift, B_RADIUS, dil))
    narrow = [_band_index(QBLK, QBLK, 0, B_RADIUS, B_GROUPS[2][1])]
    return np.stack(wide), np.stack(narrow)


def _window_index_tables():
    return np.stack([_band_index(QBLK, 3 * QBLK, shift, C_RADIUS, 1) for shift in (0, -QBLK, -2 * QBLK)])


def _bias_kernel(tab_ref, idx_ref, o_ref, *, col0):
    col = pl.program_id(0) + col0
    idx = idx_ref[...]
    acc = jnp.full(idx.shape, NEG_INF, F32)
    for b in range(N_BUCKETS):
        acc = jnp.where(idx == b, tab_ref[b, col] * LOG2E, acc)
    o_ref[...] = acc


def _bias_tables(rel_bias, idx_np, col0, n_heads):
    nv, r, c = idx_np.shape
    return pl.pallas_call(
        functools.partial(_bias_kernel, col0=col0),
        out_shape=jax.ShapeDtypeStruct((n_heads, nv, r, c), F32),
        grid=(n_heads, nv),
        in_specs=[pl.BlockSpec(memory_space=pltpu.SMEM),
                  pl.BlockSpec((None, r, c), lambda h, v: (v, 0, 0))],
        out_specs=pl.BlockSpec((None, None, r, c), lambda h, v: (h, v, 0, 0)),
        compiler_params=_cparams(("arbitrary", "arbitrary")),
        name="bias_tables",
    )(rel_bias, jnp.asarray(idx_np))


IN_TM = 512


def _in_proj_kernel(x_ref, g1_ref, wa_ref, wb_ref, wc_ref, qg_ref, kvg_ref, wq_ref, wqr_ref, wk_ref, wv_ref,
                    place_ref, bscale_ref, cscale_ref, vones_ref, cq_ref, sq_ref, ck_ref, sk_ref,
                    qa_ref, ka_ref, va_ref, zb_ref, zc_ref):
    h = _rms(x_ref[...], g1_ref[...]).astype(BF16)
    zb_ref[...] = (_dot(h, wb_ref[...]) * bscale_ref[...]).astype(BF16)
    zc_ref[...] = (_dot(h, wc_ref[...]) * cscale_ref[...]).astype(BF16)
    za = _dot(h, wa_ref[...])
    hq = _rms(za[:, :A_Q_LORA], qg_ref[...]).astype(BF16)
    hkv = _rms(za[:, A_Q_LORA:A_Q_LORA + A_KV_LORA], kvg_ref[...]).astype(BF16)
    kpe = za[:, A_Q_LORA + A_KV_LORA:A_Q_LORA + A_KV_LORA + LANES]
    kpe_rot = za[:, A_Q_LORA + A_KV_LORA + LANES:]
    q = _dot(hq, wq_ref[...]) * (A_SCALE * LOG2E)
    q_rot = _dot(hq, wqr_ref[...]) * (A_SCALE * LOG2E)
    k_rope = (kpe * ck_ref[...] + kpe_rot * sk_ref[...]).astype(BF16)
    k = _dot(hkv, wk_ref[...]) + _dot(k_rope, place_ref[...])
    ka_ref[...] = k.astype(BF16)
    va_ref[...] = (_dot(hkv, wv_ref[...]) + vones_ref[...]).astype(BF16)
    cq = cq_ref[...]
    sq = sq_ref[...]
    for hh in range(A_HEADS):
        sl = slice(hh * LANES, (hh + 1) * LANES)
        qa_ref[:, sl] = (q[:, sl] * cq + q_rot[:, sl] * sq).astype(BF16)


def _in_proj(x2, g1, wa, wb, wc, qg, kvg, wq, wqr, wk, wv, place, bscale, cscale, vones, cq, sq, ck, sk):
    t = x2.shape[0]
    tm = IN_TM
    n_pos = SEQ // tm
    full = lambda a: pl.BlockSpec(a.shape, lambda i: (0,) * a.ndim)
    row = lambda w: pl.BlockSpec((tm, w), lambda i: (i, 0))
    pos = pl.BlockSpec((tm, LANES), lambda i: (i % n_pos, 0))
    return pl.pallas_call(
        _in_proj_kernel,
        out_shape=(jax.ShapeDtypeStruct((t, A_HEADS * LANES), BF16),
                   jax.ShapeDtypeStruct((t, A_HEADS * LANES), BF16),
                   jax.ShapeDtypeStruct((t, A_HEADS * LANES), BF16),
                   jax.ShapeDtypeStruct((t, wb.shape[1]), BF16),
                   jax.ShapeDtypeStruct((t, wc.shape[1]), BF16)),
        grid=(t // tm,),
        in_specs=[row(D_MODEL), full(g1), full(wa), full(wb), full(wc), full(qg), full(kvg), full(wq), full(wqr),
                  full(wk), full(wv), full(place), full(bscale), full(cscale), full(vones), pos, pos, pos, pos],
        out_specs=(row(A_HEADS * LANES), row(A_HEADS * LANES), row(A_HEADS * LANES), row(wb.shape[1]),
                   row(wc.shape[1])),
        compiler_params=_cparams(("parallel",)),
        name="in_proj",
    )(x2, g1, wa, wb, wc, qg, kvg, wq, wqr, wk, wv, place, bscale, cscale, vones, cq, sq, ck, sk)


A_TQ = 2048
A_SUB = 256


def _mla_kernel(q_ref, k_ref, v_ref, o_ref):
    first = lax.broadcasted_iota(jnp.int32, (1, LANES), 1) < HALF
    tasks = [(r, hh) for r in range(A_TQ // A_SUB) for hh in range(2)]
    done = {}

    def rows(r):
        return slice(r * A_SUB, (r + 1) * A_SUB)

    def lanes(hh):
        return slice(hh * LANES, (hh + 1) * LANES)

    def scores(task, _):
        r, hh = task
        return _dot_nt(q_ref[rows(r), lanes(hh)], k_ref[:, lanes(hh)])

    def values(task, s):
        p = jnp.exp2(s - jnp.max(s, axis=-1, keepdims=True))
        return _dot(p.astype(BF16), v_ref[:, lanes(task[1])])

    def store(task, a):
        r, hh = task
        done[task] = a
        if hh == 1:
            a0 = done.pop((r, 0))
            a1 = done.pop((r, 1))
            l = pltpu.roll(jnp.where(first, a1, a0), HALF, axis=1)
            o_ref[rows(r), :] = (jnp.where(first, a0, a1) / l).astype(BF16)

    _software_pipeline(tasks, (scores, values, store))


def _mla_attention(qa, ka, va, batch):
    t = qa.shape[0]
    nq = SEQ // A_TQ
    return pl.pallas_call(
        _mla_kernel,
        out_shape=jax.ShapeDtypeStruct((t, A_HEADS * A_V), BF16),
        grid=(batch, A_HEADS // 2, nq),
        in_specs=[pl.BlockSpec((A_TQ, 2 * LANES), lambda b, p, i: (b * nq + i, p)),
                  pl.BlockSpec((SEQ, 2 * LANES), lambda b, p, i: (b, p)),
                  pl.BlockSpec((SEQ, 2 * LANES), lambda b, p, i: (b, p))],
        out_specs=pl.BlockSpec((A_TQ, LANES), lambda b, p, i: (b * nq + i, p)),
        compiler_params=_cparams(("parallel", "parallel", "arbitrary")),
        name="mla_attention",
    )(qa, ka, va)


def _software_pipeline(tasks, stages):
    state = [None] * len(tasks)
    for step in range(len(tasks) + len(stages) - 1):
        for s, stage in enumerate(stages):
            i = step - s
            if 0 <= i < len(tasks):
                state[i] = stage(tasks[i], state[i])


def _first_half():
    return lax.broadcasted_iota(jnp.int32, (1, LANES), 1) < HALF


def _pair_scores(q2, k2, biases):
    first = _first_half()
    zero = jnp.zeros_like(q2)
    tq = q2.shape[0]
    stacked = jnp.concatenate([jnp.where(first, q2, zero), jnp.where(first, zero, q2)], axis=0)
    s = _dot_nt(stacked, k2)
    return [s[:tq] + biases[0], s[tq:] + biases[1]]


def _pair_values(scores, v2, sinks=None):
    first = _first_half()
    one = jnp.ones_like(v2)
    res = []
    for hh, s in enumerate(scores):
        own = first if hh == 0 else jnp.logical_not(first)
        m = jnp.max(s, axis=-1, keepdims=True)
        if sinks is not None:
            m = jnp.maximum(m, sinks[hh])
        a = _dot(jnp.exp2(s - m).astype(BF16), jnp.where(own, v2, one))
        if sinks is not None:
            a = a + jnp.where(own, 0.0, jnp.exp2(sinks[hh] - m))
        res.append((m, a))
    return res


def _pair_merge(res):
    first = _first_half()
    (m0, a0), (m1, a1) = res
    return jnp.where(first, m0, m1), jnp.where(first, a1, a0), jnp.where(first, a0, a1)


def _dilated_kernel(q_ref, k_ref, v_ref, bw_ref, bn_ref, o_ref, qf, kf, vf, qs, ks, vs, stream, merged):
    qf[...] = q_ref[...].astype(F32)
    kf[...] = k_ref[...].astype(F32)
    vf[...] = v_ref[...].astype(F32)
    n_blocks = SEQ // QBLK
    for gi, (_, dil) in enumerate(B_GROUPS):
        length = SEQ // dil
        per_stream = length // QBLK
        for g in range(dil):
            rows = pl.ds(g, length, stride=dil) if dil > 1 else slice(None)
            dst = slice(g * length, (g + 1) * length)
            qs[dst, :] = qf[rows, :].astype(BF16)
            ks[dst, :] = kf[rows, :].astype(BF16)
            vs[dst, :] = vf[rows, :].astype(BF16)

        def rows(j):
            return slice(j * QBLK, (j + 1) * QBLK)

        def keys(j, length=length, per_stream=per_stream):
            if per_stream == 1:
                return rows(j)
            i = j % per_stream
            start = (j - i) * QBLK + min(max(i * QBLK - B_RADIUS, 0), length - 2 * QBLK)
            return slice(start, start + 2 * QBLK)

        def scores(j, _, gi=gi, per_stream=per_stream, keys=keys):
            if per_stream == 1:
                biases = (bn_ref[0, 0], bn_ref[1, 0])
            else:
                i = j % per_stream
                var = gi * 3 + (0 if i == 0 else 2 if i == per_stream - 1 else 1)
                biases = (bw_ref[0, var], bw_ref[1, var])
            return _pair_scores(qs[rows(j), :], ks[keys(j), :], biases)

        def values(j, s, keys=keys):
            return _pair_values(s, vs[keys(j), :])

        def store(j, res, gi=gi, dil=dil):
            for slot, value in enumerate(_pair_merge(res)):
                if dil == 1:
                    merged[gi, slot, rows(j), :] = value
                else:
                    stream[slot, rows(j), :] = value

        _software_pipeline(list(range(n_blocks)), (scores, values, store))
        for g in range(dil if dil > 1 else 0):
            seq_rows = pl.ds(g, length, stride=dil)
            src = slice(g * length, (g + 1) * length)
            for slot in range(3):
                merged[gi, slot, seq_rows, :] = stream[slot, src, :]

    def merge(c, carry):
        rows = pl.ds(pl.multiple_of(c * QBLK, QBLK), QBLK)
        m = [merged[gi, 0, rows, :] for gi in range(3)]
        top = jnp.maximum(jnp.maximum(m[0], m[1]), m[2])
        w = [jnp.exp2(mg - top) for mg in m]
        l = [pltpu.roll(merged[gi, 1, rows, :], HALF, axis=1) for gi in range(3)]
        num = w[0] * merged[0, 2, rows, :] + w[1] * merged[1, 2, rows, :] + w[2] * merged[2, 2, rows, :]
        den = w[0] * l[0] + w[1] * l[1] + w[2] * l[2]
        o_ref[rows, :] = (num / den).astype(BF16)
        return carry

    lax.fori_loop(0, n_blocks, merge, 0)


def _dilated_attention(zb, bias_wide, bias_narrow, batch):
    t = zb.shape[0]
    n_pairs = B_HEADS // 2
    seq_block = lambda off: pl.BlockSpec((SEQ, LANES), lambda p, b: (b, off + p))
    nw = bias_wide.shape[1]
    return pl.pallas_call(
        _dilated_kernel,
        out_shape=jax.ShapeDtypeStruct((t, B_HEADS * B_HEAD_DIM), BF16),
        grid=(n_pairs, batch),
        in_specs=[seq_block(0), seq_block(n_pairs), seq_block(2 * n_pairs),
                  pl.BlockSpec((2, nw, QBLK, 2 * QBLK), lambda p, b: (p, 0, 0, 0)),
                  pl.BlockSpec((2, 1, QBLK, QBLK), lambda p, b: (p, 0, 0, 0))],
        out_specs=pl.BlockSpec((SEQ, LANES), lambda p, b: (b, p)),
        scratch_shapes=[pltpu.VMEM((SEQ, LANES), F32)] * 3 + [pltpu.VMEM((SEQ, LANES), BF16)] * 3
        + [pltpu.VMEM((3, SEQ, LANES), F32), pltpu.VMEM((3, 3, SEQ, LANES), F32)],
        compiler_params=_cparams(("arbitrary", "arbitrary")),
        name="dilated_attention",
    )(zb, zb, zb, bias_wide, bias_narrow)


def _window_kernel(sink_ref, q_ref, k_ref, v_ref, bias_ref, o_ref):
    pair = pl.program_id(0)
    sinks = (sink_ref[pair] * LOG2E, sink_ref[pair + C_HEADS // 2] * LOG2E)
    n_blocks = SEQ // QBLK

    def keys(i):
        start = min(max((i - 1) * QBLK, 0), SEQ - 3 * QBLK)
        return slice(start, start + 3 * QBLK)

    def scores(i, _):
        var = 0 if i == 0 else 2 if i == n_blocks - 1 else 1
        return _pair_scores(q_ref[i * QBLK:(i + 1) * QBLK, :], k_ref[keys(i), :],
                            (bias_ref[0, var], bias_ref[1, var]))

    def values(i, s):
        return _pair_values(s, v_ref[keys(i), :], sinks)

    def store(i, res):
        _, l_swapped, o = _pair_merge(res)
        o_ref[i * QBLK:(i + 1) * QBLK, :] = (o / pltpu.roll(l_swapped, HALF, axis=1)).astype(BF16)

    _software_pipeline(list(range(n_blocks)), (scores, values, store))


def _window_attention(zc, sink, bias, batch):
    t = zc.shape[0]
    n_pairs = C_HEADS // 2
    return pl.pallas_call(
        _window_kernel,
        out_shape=jax.ShapeDtypeStruct((t, C_HEADS * C_HEAD_DIM), BF16),
        grid=(n_pairs, batch),
        in_specs=[pl.BlockSpec(memory_space=pltpu.SMEM),
                  pl.BlockSpec((SEQ, LANES), lambda p, b: (b, p)),
                  pl.BlockSpec((SEQ, LANES), lambda p, b: (b, n_pairs)),
                  pl.BlockSpec((SEQ, LANES), lambda p, b: (b, n_pairs + 1)),
                  pl.BlockSpec((None, 2, 3, QBLK, 3 * QBLK), lambda p, b: (p, 0, 0, 0, 0))],
        out_specs=pl.BlockSpec((SEQ, LANES), lambda p, b: (b, p)),
        compiler_params=_cparams(("arbitrary", "arbitrary")),
        name="window_attention",
    )(sink, zc, zc, zc, bias)


MERGE_TM = 512


def _merge_kernel(x_ref, g1_ref, ya_ref, yb_ref, yc_ref, wg_ref, wbr_ref, wo_ref, o_ref):
    x = x_ref[...]
    h = _rms(x, g1_ref[...]).astype(BF16)
    merged = None
    for b, y_ref in enumerate((ya_ref, yb_ref, yc_ref)):
        gate = jax.nn.sigmoid(_dot(h, wg_ref[:, b * D_MODEL:(b + 1) * D_MODEL]))
        term = gate * _dot(y_ref[...], wbr_ref[b])
        merged = term if merged is None else merged + term
    o_ref[...] = x + _dot(merged.astype(BF16), wo_ref[...])


def _merge(x2, g1, ya, yb, yc, wg, wbr, wo):
    t = x2.shape[0]
    tm = MERGE_TM
    full = lambda a: pl.BlockSpec(a.shape, lambda i: (0,) * a.ndim)
    row = lambda w: pl.BlockSpec((tm, w), lambda i: (i, 0))
    return pl.pallas_call(
        _merge_kernel,
        out_shape=jax.ShapeDtypeStruct((t, D_MODEL), F32),
        grid=(t // tm,),
        in_specs=[row(D_MODEL), full(g1), row(ya.shape[1]), row(yb.shape[1]), row(yc.shape[1]),
                  full(wg), full(wbr), full(wo)],
        out_specs=row(D_MODEL),
        compiler_params=_cparams(("parallel",)),
        name="gate_merge",
    )(x2, g1, ya, yb, yc, wg, wbr, wo)


TOP_K = 2
ROW_SUBLANES = D_MODEL // LANES
MOE_TM = 512
EXPERT_TF = D_FF_EXPERT // 2


def _router_kernel(x_ref, g_ref, wr_ref, eid_ref, rank_ref, gate_ref, cnt_ref, carry):
    @pl.when(pl.program_id(0) == 0)
    def _():
        carry[...] = jnp.zeros_like(carry)

    h = _rms(x_ref[...], g_ref[...])
    w = wr_ref[...]
    h_hi = h.astype(BF16)
    w_hi = w.astype(BF16)
    h_lo = (h - h_hi.astype(F32)).astype(BF16)
    w_lo = (w - w_hi.astype(F32)).astype(BF16)
    logits = _dot(h_hi, w_hi) + (_dot(h_lo, w_hi) + _dot(h_hi, w_lo))
    idx = lax.broadcasted_iota(jnp.int32, logits.shape, 1)
    m1 = jnp.max(logits, axis=-1, keepdims=True)
    i1 = jnp.min(jnp.where(logits == m1, idx, N_EXPERTS), axis=-1, keepdims=True)
    first = idx == i1
    rest = jnp.where(first, -jnp.inf, logits)
    m2 = jnp.max(rest, axis=-1, keepdims=True)
    i2 = jnp.min(jnp.where(rest == m2, idx, N_EXPERTS), axis=-1, keepdims=True)
    second = idx == i2
    e = jnp.exp(m2 - m1)
    den = 1.0 + e
    chosen = jnp.where(jnp.logical_or(first, second), 1.0, 0.0)
    tm = chosen.shape[0]
    earlier = (lax.broadcasted_iota(jnp.int32, (tm, tm), 0) > lax.broadcasted_iota(jnp.int32, (tm, tm), 1))
    before = _dot(jnp.where(earlier, 1.0, 0.0).astype(BF16), chosen.astype(BF16)) + carry[...]
    r1 = jnp.sum(jnp.where(first, before, 0.0), axis=-1, keepdims=True)
    r2 = jnp.sum(jnp.where(second, before, 0.0), axis=-1, keepdims=True)
    carry[...] += jnp.sum(chosen, axis=0, keepdims=True)
    slot0 = lax.broadcasted_iota(jnp.int32, (tm, TOP_K), 1) == 0
    eid_ref[...] = jnp.where(slot0, i1, i2)
    rank_ref[...] = jnp.where(slot0, r1, r2).astype(jnp.int32)
    gate_ref[...] = jnp.where(slot0, 1.0 / den, e / den)
    cnt_ref[...] = carry[...].astype(jnp.int32)


def _router(x2, g, wr):
    t = x2.shape[0]
    tm = MOE_TM
    pair = pl.BlockSpec((tm, TOP_K), lambda i: (i, 0))
    return pl.pallas_call(
        _router_kernel,
        out_shape=(jax.ShapeDtypeStruct((t, TOP_K), jnp.int32), jax.ShapeDtypeStruct((t, TOP_K), jnp.int32),
                   jax.ShapeDtypeStruct((t, TOP_K), F32), jax.ShapeDtypeStruct((1, N_EXPERTS), jnp.int32)),
        grid=(t // tm,),
        in_specs=[pl.BlockSpec((tm, D_MODEL), lambda i: (i, 0)),
                  pl.BlockSpec(g.shape, lambda i: (0, 0)),
                  pl.BlockSpec(wr.shape, lambda i: (0, 0))],
        out_specs=(pair, pair, pair, pl.BlockSpec((1, N_EXPERTS), lambda i: (0, 0))),
        scratch_shapes=[pltpu.VMEM((1, N_EXPERTS), F32)],
        compiler_params=_cparams(("arbitrary",)),
        name="router",
    )(x2, g, wr)


def _row_tile(ref, row):
    return ref.at[pl.ds(pl.multiple_of(row * ROW_SUBLANES, ROW_SUBLANES), ROW_SUBLANES), :]


def _dispatch_kernel(tail_start_ref, tail_len_ref, na_ref, dest_ref, x_ref, g_ref, hs_ref, stage, sems):
    i = pl.program_id(0)
    n_steps = pl.num_programs(0)
    tm = x_ref.shape[0]
    slot = i % 2
    rows_per_slot = stage.shape[1]

    def drain(s):
        for _ in range(TOP_K):
            pltpu.make_async_copy(stage.at[s], hs_ref.at[pl.ds(0, rows_per_slot), :], sems.at[s]).wait()

    @pl.when(i >= 2)
    def _():
        drain(slot)

    h = _rms(x_ref[...], g_ref[...])
    for s in range(ROW_SUBLANES):
        stage[slot, pl.ds(s, tm, stride=ROW_SUBLANES), :] = h[:, s * LANES:(s + 1) * LANES]

    def start(t, carry):
        for k in range(TOP_K):
            pltpu.make_async_copy(_row_tile(stage.at[slot], t), _row_tile(hs_ref, dest_ref[TOP_K * t + k]),
                                  sems.at[slot]).start(priority=k)
        return carry

    lax.fori_loop(0, tm, start, 0, unroll=8)

    @pl.when(i == n_steps - 1)
    def _():
        @pl.when(n_steps >= 2)
        def _():
            drain(1 - slot)

        drain(slot)
        zeros = stage.at[0]
        zeros[...] = jnp.zeros_like(zeros)
        for e in range(N_EXPERTS):
            def tail(r, e=e):
                return pltpu.make_async_copy(_row_tile(zeros, 0), _row_tile(hs_ref, tail_start_ref[e] + r), sems.at[0])

            lax.fori_loop(0, tail_len_ref[e], lambda r, c: (tail(r).start(), c)[1], 0)
            lax.fori_loop(0, tail_len_ref[e], lambda r, c: (tail(r).wait(), c)[1], 0)
        n_tiles = hs_ref.shape[0] // rows_per_slot

        def spare(j):
            rows = pl.ds(pl.multiple_of(j * rows_per_slot, rows_per_slot), rows_per_slot)
            return pltpu.make_async_copy(zeros, hs_ref.at[rows, :], sems.at[0])

        lax.fori_loop(na_ref[0], n_tiles, lambda j, c: (spare(j).start(), c)[1], 0)
        lax.fori_loop(na_ref[0], n_tiles, lambda j, c: (spare(j).wait(), c)[1], 0)


def _dispatch(x2, g, dest, tail_start, tail_len, n_active, n_rows):
    t = x2.shape[0]
    tm = MOE_TM
    return pl.pallas_call(
        _dispatch_kernel,
        out_shape=jax.ShapeDtypeStruct((n_rows * ROW_SUBLANES, LANES), F32),
        grid_spec=pltpu.PrefetchScalarGridSpec(
            num_scalar_prefetch=3,
            grid=(t // tm,),
            in_specs=[pl.BlockSpec((TOP_K * tm,), lambda i, ts, tl, na: (i,), memory_space=pltpu.SMEM),
                      pl.BlockSpec((tm, D_MODEL), lambda i, ts, tl, na: (i, 0)),
                      pl.BlockSpec(g.shape, lambda i, ts, tl, na: (0, 0))],
            out_specs=pl.BlockSpec(memory_space=pl.ANY),
            scratch_shapes=[pltpu.VMEM((2, tm * ROW_SUBLANES, LANES), F32), pltpu.SemaphoreType.DMA((2,))]),
        compiler_params=_cparams(("arbitrary",)),
        name="dispatch",
    )(tail_start, tail_len, n_active, dest, x2, g)


def _experts_kernel(te_ref, na_ref, hs_ref, wg_ref, wu_ref, wd_ref, ys_ref, h_scr, acc_scr):
    i = pl.program_id(0)
    f = pl.program_id(1)
    active = i < na_ref[0]
    tm = h_scr.shape[0]

    @pl.when(jnp.logical_and(active, f == 0))
    def _():
        for s in range(ROW_SUBLANES):
            h_scr[:, s * LANES:(s + 1) * LANES] = hs_ref[pl.ds(s, tm, stride=ROW_SUBLANES), :].astype(BF16)

    @pl.when(active)
    def _():
        h = h_scr[...]
        act = jax.nn.silu(_dot(h, wg_ref[...])) * _dot(h, wu_ref[...])
        part = _dot(act.astype(BF16), wd_ref[...])

        @pl.when(f == 0)
        def _():
            acc_scr[...] = part

        @pl.when(f != 0)
        def _():
            acc_scr[...] += part

    @pl.when(jnp.logical_and(active, f == pl.num_programs(1) - 1))
    def _():
        for s in range(ROW_SUBLANES):
            ys_ref[pl.ds(s, tm, stride=ROW_SUBLANES), :] = acc_scr[:, s * LANES:(s + 1) * LANES]

    @pl.when(jnp.logical_and(jnp.logical_not(active), f == pl.num_programs(1) - 1))
    def _():
        ys_ref[...] = jnp.zeros_like(ys_ref)


def _experts(hs, tile_expert, n_active, wg, wu, wd):
    tm = MOE_TM
    tf = EXPERT_TF
    n_tiles = hs.shape[0] // (tm * ROW_SUBLANES)
    n_f = wg.shape[2] // tf
    rows = lambda i, f, te, na: (jnp.minimum(i, na[0] - 1), 0)
    col = lambda i, f, na: jnp.where(i < na[0], f, n_f - 1)
    return pl.pallas_call(
        _experts_kernel,
        out_shape=jax.ShapeDtypeStruct(hs.shape, F32),
        grid_spec=pltpu.PrefetchScalarGridSpec(
            num_scalar_prefetch=2,
            grid=(n_tiles, n_f),
            in_specs=[pl.BlockSpec((tm * ROW_SUBLANES, LANES), rows),
                      pl.BlockSpec((None, D_MODEL, tf), lambda i, f, te, na: (te[i], 0, col(i, f, na))),
                      pl.BlockSpec((None, D_MODEL, tf), lambda i, f, te, na: (te[i], 0, col(i, f, na))),
                      pl.BlockSpec((None, tf, D_MODEL), lambda i, f, te, na: (te[i], col(i, f, na), 0))],
            out_specs=pl.BlockSpec((tm * ROW_SUBLANES, LANES), lambda i, f, te, na: (i, 0)),
            scratch_shapes=[pltpu.VMEM((tm, D_MODEL), BF16), pltpu.VMEM((tm, D_MODEL), F32)]),
        compiler_params=_cparams(("arbitrary", "arbitrary")),
        name="experts",
    )(tile_expert, n_active, hs, wg, wu, wd)


def _combine_kernel(dest_ref, next_dest_ref, x_ref, gate_ref, gf_ref, ys_ref, o_ref, bufs, sems, *, final):
    i = pl.program_id(0)
    tm = x_ref.shape[0]
    slot = i % 2

    def gather(idx_ref, s):
        def start(t, carry):
            for k in range(TOP_K):
                pltpu.make_async_copy(_row_tile(ys_ref, idx_ref[TOP_K * t + k]), _row_tile(bufs.at[s, k], t),
                                      sems.at[s]).start(priority=k)
            return carry

        lax.fori_loop(0, tm, start, 0, unroll=8)

    @pl.when(i == 0)
    def _():
        gather(dest_ref, slot)

    @pl.when(i + 1 < pl.num_programs(0))
    def _():
        gather(next_dest_ref, 1 - slot)

    for k in range(TOP_K):
        pltpu.make_async_copy(ys_ref.at[pl.ds(0, bufs.shape[2]), :], bufs.at[slot, k], sems.at[slot]).wait()
    gate = gate_ref[...]
    g0 = gate[:, 0:1]
    g1 = gate[:, 1:2]
    ss = jnp.zeros((tm, 1), F32)
    for s in range(ROW_SUBLANES):
        sl = slice(s * LANES, (s + 1) * LANES)
        rows = pl.ds(s, tm, stride=ROW_SUBLANES)
        y = x_ref[:, sl] + (g0 * bufs[slot, 0, rows, :] + g1 * bufs[slot, 1, rows, :])
        o_ref[:, sl] = y
        ss = ss + jnp.sum(y * y, axis=-1, keepdims=True)
    if final:
        o_ref[...] = o_ref[...] * lax.rsqrt(ss / D_MODEL + EPS) * gf_ref[...]


def _combine(x2, dest, gates, gf, ys, *, final):
    t = x2.shape[0]
    tm = MOE_TM
    return pl.pallas_call(
        functools.partial(_combine_kernel, final=final),
        out_shape=jax.ShapeDtypeStruct((t, D_MODEL), F32),
        grid=(t // tm,),
        in_specs=[pl.BlockSpec((TOP_K * tm,), lambda i: (i,), memory_space=pltpu.SMEM),
                  pl.BlockSpec((TOP_K * tm,), lambda i: (jnp.minimum(i + 1, t // tm - 1),),
                               memory_space=pltpu.SMEM),
                  pl.BlockSpec((tm, D_MODEL), lambda i: (i, 0)),
                  pl.BlockSpec((tm, TOP_K), lambda i: (i, 0)),
                  pl.BlockSpec(gf.shape, lambda i: (0, 0)),
                  pl.BlockSpec(memory_space=pl.ANY)],
        out_specs=pl.BlockSpec((tm, D_MODEL), lambda i: (i, 0)),
        scratch_shapes=[pltpu.VMEM((2, TOP_K, tm * ROW_SUBLANES, LANES), F32), pltpu.SemaphoreType.DMA((2,))],
        compiler_params=_cparams(("arbitrary",)),
        name="combine",
    )(dest, dest, x2, gates, gf, ys)


def _moe(x2, g, wr, wg, wu, wd, gf, *, final):
    t = x2.shape[0]
    tm = MOE_TM
    eid, rank, gates, counts = _router(x2, g, wr)
    counts = counts[0]
    padded = (counts + tm - 1) // tm * tm
    group_end = jnp.cumsum(padded)
    group_start = group_end - padded
    dest = (group_start[eid] + rank).reshape(-1)
    n_tiles = TOP_K * t // tm + N_EXPERTS
    n_active = (group_end[-1:] // tm).astype(jnp.int32)
    tile_row = jnp.arange(n_tiles, dtype=jnp.int32) * tm
    tile_row = jnp.minimum(tile_row, group_end[-1] - tm)
    tile_expert = jnp.sum(tile_row[:, None] >= group_end[None, :], axis=1).astype(jnp.int32)
    hs = _dispatch(x2, g, dest, group_start + counts, padded - counts, n_active, n_tiles * tm)
    ys = _experts(hs, tile_expert, n_active, wg, wu, wd)
    return _combine(x2, dest, gates, gf, ys, final=final)


FFN_TM = 512
FFN_TF = D_FF_DENSE


def _ffn_kernel(x_ref, g_ref, wg_ref, wu_ref, wd_ref, gf_ref, o_ref, h_scr, acc_scr, *, final):
    f = pl.program_id(1)

    @pl.when(f == 0)
    def _():
        h_scr[...] = _rms(x_ref[...], g_ref[...]).astype(BF16)

    h = h_scr[...]
    act = jax.nn.silu(_dot(h, wg_ref[...])) * _dot(h, wu_ref[...])
    part = _dot(act.astype(BF16), wd_ref[...])

    @pl.when(f == 0)
    def _():
        acc_scr[...] = part

    @pl.when(f != 0)
    def _():
        acc_scr[...] += part

    @pl.when(f == pl.num_programs(1) - 1)
    def _():
        y = x_ref[...] + acc_scr[...]
        if final:
            y = _rms(y, gf_ref[...])
        o_ref[...] = y


def _ffn(x2, g, wg, wu, wd, gf, *, final):
    t = x2.shape[0]
    tm = FFN_TM
    tf = FFN_TF
    d_ff = wg.shape[1]
    return pl.pallas_call(
        functools.partial(_ffn_kernel, final=final),
        out_shape=jax.ShapeDtypeStruct((t, D_MODEL), F32),
        grid=(t // tm, d_ff // tf),
        in_specs=[pl.BlockSpec((tm, D_MODEL), lambda i, f: (i, 0)),
                  pl.BlockSpec(g.shape, lambda i, f: (0, 0)),
                  pl.BlockSpec((D_MODEL, tf), lambda i, f: (0, f)),
                  pl.BlockSpec((D_MODEL, tf), lambda i, f: (0, f)),
                  pl.BlockSpec((tf, D_MODEL), lambda i, f: (f, 0)),
                  pl.BlockSpec(gf.shape, lambda i, f: (0, 0))],
        out_specs=pl.BlockSpec((tm, D_MODEL), lambda i, f: (i, 0)),
        scratch_shapes=[pltpu.VMEM((tm, D_MODEL), BF16), pltpu.VMEM((tm, D_MODEL), F32)],
        compiler_params=_cparams(("parallel", "arbitrary")),
        name="swiglu",
    )(x2, g, wg, wu, wd, gf)


def _rot_cols(w):
    half = A_ROPE // 2
    return jnp.concatenate([-w[..., half:], w[..., :half]], axis=-1)


def _prep_layer(w_in, w_uq, w_ukv, w_br_a, w_br_b, w_br_c):
    o = 0
    w_cq = w_in[:, o:o + A_Q_LORA]; o += A_Q_LORA
    w_ckv = w_in[:, o:o + A_KV_LORA]; o += A_KV_LORA
    w_kpe = w_in[:, o:o + A_ROPE]; o += A_ROPE
    nb = B_HEADS * B_HEAD_DIM
    w_b = w_in[:, o:o + 3 * nb]; o += 3 * nb
    nq = C_HEADS * C_HEAD_DIM
    nkv = C_KV_HEADS * C_HEAD_DIM
    w_c = w_in[:, o:o + nq + 2 * nkv]; o += nq + 2 * nkv
    w_g = w_in[:, o:]
    pad = jnp.zeros((D_MODEL, LANES - A_ROPE), F32)
    wa = jnp.concatenate([w_cq, w_ckv, w_kpe, pad, _rot_cols(w_kpe), pad], axis=1)
    wb = jnp.concatenate([w_b[:, :nb] * (B_HEAD_DIM ** -0.5), w_b[:, nb:]], axis=1)
    q_c = w_c[:, :nq].reshape(D_MODEL, 2, C_HEADS // 2, C_HEAD_DIM).transpose(0, 2, 1, 3).reshape(D_MODEL, nq)
    wc = jnp.concatenate([q_c * (C_HEAD_DIM ** -0.5), w_c[:, nq:]], axis=1)
    uq = w_uq.reshape(A_Q_LORA, A_HEADS, A_NOPE + A_ROPE)
    z32 = jnp.zeros((A_Q_LORA, A_HEADS, LANES - A_NOPE - A_ROPE), F32)
    wq = jnp.concatenate([uq, z32], axis=-1).reshape(A_Q_LORA, A_HEADS * LANES)
    wqr = jnp.concatenate([jnp.zeros((A_Q_LORA, A_HEADS, A_NOPE), F32), _rot_cols(uq[..., A_NOPE:]), z32],
                          axis=-1).reshape(A_Q_LORA, A_HEADS * LANES)
    ukv = w_ukv.reshape(A_KV_LORA, A_HEADS, A_NOPE + A_V)
    wk = jnp.concatenate([ukv[..., :A_NOPE], jnp.zeros((A_KV_LORA, A_HEADS, LANES - A_NOPE), F32)],
                         axis=-1).reshape(A_KV_LORA, A_HEADS * LANES)
    uv = ukv[..., A_NOPE:].reshape(A_KV_LORA, A_HEADS // 2, 2, A_V)
    zv = jnp.zeros_like(uv[:, :, 0])
    wv = jnp.stack([jnp.concatenate([uv[:, :, 0], zv], axis=-1), jnp.concatenate([zv, uv[:, :, 1]], axis=-1)],
                   axis=2).reshape(A_KV_LORA, A_HEADS * LANES)
    w_br_c = w_br_c.reshape(2, C_HEADS // 2, C_HEAD_DIM, D_MODEL).transpose(1, 0, 2, 3).reshape(nq, D_MODEL)
    wbr = jnp.stack([w_br_a, w_br_b, w_br_c])
    cast = lambda a: a.astype(BF16)
    return dict(wa=cast(wa), wb=cast(wb), wc=cast(wc), wg=cast(w_g), wq=cast(wq), wqr=cast(wqr), wk=cast(wk),
                wv=cast(wv), wbr=cast(wbr))


def _rope_tables():
    half = A_ROPE // 2
    inv = ROPE_THETA ** (-jnp.arange(half, dtype=F32) / half)
    ang = jnp.arange(SEQ, dtype=F32)[:, None] * inv[None, :]
    cos2 = jnp.tile(jnp.cos(ang), (1, 2))
    sin2 = jnp.tile(jnp.sin(ang), (1, 2))
    ones = jnp.ones((SEQ, A_NOPE), F32)
    z = lambda w: jnp.zeros((SEQ, w), F32)
    cq = jnp.concatenate([ones, cos2, z(LANES - A_NOPE - A_ROPE)], axis=1)
    sq = jnp.concatenate([z(A_NOPE), sin2, z(LANES - A_NOPE - A_ROPE)], axis=1)
    ck = jnp.concatenate([cos2, z(LANES - A_ROPE)], axis=1)
    sk = jnp.concatenate([sin2, z(LANES - A_ROPE)], axis=1)
    return cq, sq, ck, sk


def _rope_placement():
    place = np.zeros((LANES, A_HEADS * LANES), np.float32)
    for h in range(A_HEADS):
        for j in range(A_ROPE):
            place[j, h * LANES + A_NOPE + j] = 1.0
    return jnp.asarray(place, dtype=BF16)


def kernel(x, norm1_g, w_in, q_norm_g, w_uq, kv_norm_g, w_ukv, sink_logit, rel_bias, w_branch_a, w_branch_b,
           w_branch_c, w_out, norm2_g, ffn_w_gate, ffn_w_up, ffn_w_down, router_w, exp_w_gate, exp_w_up,
           exp_w_down, final_g):
    batch, seq, d = x.shape
    assert seq == SEQ and d == D_MODEL
    t = batch * seq
    x2 = x.reshape(t, d)

    wide_idx, narrow_idx = _dilated_index_tables()
    bias_wide = _bias_tables(rel_bias, wide_idx, 0, B_HEADS)
    bias_narrow = _bias_tables(rel_bias, narrow_idx, 0, B_HEADS)
    bias_c = _bias_tables(rel_bias, _window_index_tables(), B_HEADS, C_HEADS)
    bias_c = bias_c.reshape(2, C_HEADS // 2, *bias_c.shape[1:]).transpose(1, 0, 2, 3, 4)

    cq, sq, ck, sk = _rope_tables()
    place = _rope_placement()
    row = lambda v: v.reshape(1, -1)
    q_then = lambda nq, rest: jnp.asarray(np.concatenate([np.full(nq, LOG2E), np.ones(rest)])[None], F32)
    bscale = q_then(B_HEADS * B_HEAD_DIM, 2 * B_HEADS * B_HEAD_DIM)
    cscale = q_then(C_HEADS * C_HEAD_DIM, 2 * C_KV_HEADS * C_HEAD_DIM)
    vones = jnp.asarray(np.tile(np.repeat([0.0, 1.0, 1.0, 0.0], A_V), A_HEADS // 2)[None], F32)

    for l in range(DEPTH):
        w = _prep_layer(w_in[l], w_uq[l], w_ukv[l], w_branch_a[l], w_branch_b[l], w_branch_c[l])
        g1 = row(norm1_g[l])
        qa, ka, va, zb, zc = _in_proj(x2, g1, w["wa"], w["wb"], w["wc"], row(q_norm_g[l]), row(kv_norm_g[l]),
                                      w["wq"], w["wqr"], w["wk"], w["wv"], place, bscale, cscale, vones,
                                      cq, sq, ck, sk)
        ya = _mla_attention(qa, ka, va, batch)
        yb = _dilated_attention(zb, bias_wide, bias_narrow, batch)
        yc = _window_attention(zc, sink_logit[l], bias_c, batch)
        x2 = _merge(x2, g1, ya, yb, yc, w["wg"], w["wbr"], w_out[l].astype(BF16))
        g2 = row(norm2_g[l])
        final = l == DEPTH - 1
        i = l // 2
        if l % 2 == 0:
            x2 = _ffn(x2, g2, ffn_w_gate[i].astype(BF16), ffn_w_up[i].astype(BF16), ffn_w_down[i].astype(BF16),
                      row(final_g), final=final)
        else:
            x2 = _moe(x2, g2, router_w[i], exp_w_gate[i].astype(BF16), exp_w_up[i].astype(BF16),
                      exp_w_down[i].astype(BF16), row(final_g), final=final)
    return x2.reshape(batch, seq, d)
```

```python
import functools
import math

import numpy as np
import jax
import jax.numpy as jnp
from jax import lax
from jax.experimental import pallas as pl
from jax.experimental.pallas import tpu as pltpu

F32 = jnp.float32
BF16 = jnp.bfloat16

D_MODEL = 1024
SEQ = 2048
DEPTH = 2
A_HEADS = 8
A_NOPE = 64
A_ROPE = 32
A_V = 64
A_Q_LORA = 256
A_KV_LORA = 128
ROPE_THETA = 10000.0
B_HEADS = 8
B_HEAD_DIM = 64
B_GROUPS = ((128, 1), (512, 4), (2048, 16))
C_HEADS = 8
C_KV_HEADS = 2
C_HEAD_DIM = 64
C_RADIUS = 128
N_BUCKETS = 32
MAX_DISTANCE = 1024
D_FF_DENSE = 2816
N_EXPERTS = 8
D_FF_EXPERT = 3584
EPS = 1e-6
NEG_INF = -1e30

LANES = 128
HALF = 64
QBLK = 128
B_RADIUS = 64
VMEM_LIMIT = 56 * 1024 * 1024

A_SCALE = (A_NOPE + A_ROPE) ** -0.5
LOG2E = math.log2(math.e)
ZA_COLS = A_Q_LORA + A_KV_LORA + 2 * LANES


def _cparams(sem):
    return pltpu.CompilerParams(dimension_semantics=sem, vmem_limit_bytes=VMEM_LIMIT)


def _rms(x, g):
    return x * lax.rsqrt(jnp.mean(x * x, axis=-1, keepdims=True) + EPS) * g


def _dot(a, b):
    return jnp.dot(a, b, preferred_element_type=F32)


def _dot_nt(a, b):
    return lax.dot_general(a, b, (((1,), (1,)), ((), ())), preferred_element_type=F32)


def _t5_bucket_np(rel):
    nb = N_BUCKETS // 2
    max_exact = nb // 2
    ret = np.where(rel > 0, nb, 0)
    n = np.abs(rel)
    nf = np.maximum(n, 1).astype(np.float64)
    t = np.log(nf / max_exact) / math.log(MAX_DISTANCE / max_exact) * (nb - max_exact)
    frac = np.abs(t - np.round(t))
    assert np.all((frac > 1e-5) | (n <= max_exact) | (n >= MAX_DISTANCE))
    large = max_exact + np.floor(np.round(t, 9)).astype(np.int64)
    large = np.minimum(large, nb - 1)
    return (ret + np.where(n < max_exact, n, large)).astype(np.int32)


def _band_index(tq, tk, shift, radius, dil):
    rel = np.arange(tk)[None, :] - np.arange(tq)[:, None] + shift
    idx = _t5_bucket_np(rel * dil)
    return np.where(np.abs(rel) <= radius, idx, -1).astype(np.int32)


def _dilated_index_tables():
    wide = []
    for window, dil in B_GROUPS[:2]:
        for shift in (0, -B_RADIUS, -2 * B_RADIUS):
            wide.append(_band_index(QBLK, 2 * QBLK, shift, B_RADIUS, dil))
    narrow = [_band_index(QBLK, QBLK, 0, B_RADIUS, B_GROUPS[2][1])]
    return np.stack(wide), np.stack(narrow)


def _window_index_tables():
    return np.stack([_band_index(QBLK, 3 * QBLK, shift, C_RADIUS, 1) for shift in (0, -QBLK, -2 * QBLK)])


def _bias_kernel(tab_ref, idx_ref, o_ref, *, col0):
    col = pl.program_id(0) + col0
    idx = idx_ref[...]
    acc = jnp.full(idx.shape, NEG_INF, F32)
    for b in range(N_BUCKETS):
        acc = jnp.where(idx == b, tab_ref[b, col] * LOG2E, acc)
    o_ref[...] = acc


def _bias_tables(rel_bias, idx_np, col0, n_heads):
    nv, r, c = idx_np.shape
    return pl.pallas_call(
        functools.partial(_bias_kernel, col0=col0),
        out_shape=jax.ShapeDtypeStruct((n_heads, nv, r, c), F32),
        grid=(n_heads, nv),
        in_specs=[pl.BlockSpec(memory_space=pltpu.SMEM),
                  pl.BlockSpec((None, r, c), lambda h, v: (v, 0, 0))],
        out_specs=pl.BlockSpec((None, None, r, c), lambda h, v: (h, v, 0, 0)),
        compiler_params=_cparams(("arbitrary", "arbitrary")),
        name="bias_tables",
    )(rel_bias, jnp.asarray(idx_np))


IN_TM = 512


def _in_proj_kernel(x_ref, g1_ref, wa_ref, wb_ref, wc_ref, qg_ref, kvg_ref, wq_ref, wqr_ref, wk_ref, wv_ref,
                    place_ref, bscale_ref, cscale_ref, vones_ref, cq_ref, sq_ref, ck_ref, sk_ref,
                    qa_ref, ka_ref, va_ref, zb_ref, zc_ref):
    h = _rms(x_ref[...], g1_ref[...]).astype(BF16)
    zb_ref[...] = (_dot(h, wb_ref[...]) * bscale_ref[...]).astype(BF16)
    zc_ref[...] = (_dot(h, wc_ref[...]) * cscale_ref[...]).astype(BF16)
    za = _dot(h, wa_ref[...])
    hq = _rms(za[:, :A_Q_LORA], qg_ref[...]).astype(BF16)
    hkv = _rms(za[:, A_Q_LORA:A_Q_LORA + A_KV_LORA], kvg_ref[...]).astype(BF16)
    kpe = za[:, A_Q_LORA + A_KV_LORA:A_Q_LORA + A_KV_LORA + LANES]
    kpe_rot = za[:, A_Q_LORA + A_KV_LORA + LANES:]
    q = _dot(hq, wq_ref[...]) * (A_SCALE * LOG2E)
    q_rot = _dot(hq, wqr_ref[...]) * (A_SCALE * LOG2E)
    k_rope = (kpe * ck_ref[...] + kpe_rot * sk_ref[...]).astype(BF16)
    k = _dot(hkv, wk_ref[...]) + _dot(k_rope, place_ref[...])
    ka_ref[...] = k.astype(BF16)
    va_ref[...] = (_dot(hkv, wv_ref[...]) + vones_ref[...]).astype(BF16)
    cq = cq_ref[...]
    sq = sq_ref[...]
    for hh in range(A_HEADS):
        sl = slice(hh * LANES, (hh + 1) * LANES)
        qa_ref[:, sl] = (q[:, sl] * cq + q_rot[:, sl] * sq).astype(BF16)


def _in_proj(x2, g1, wa, wb, wc, qg, kvg, wq, wqr, wk, wv, place, bscale, cscale, vones, cq, sq, ck, sk):
    t = x2.shape[0]
    tm = IN_TM
    n_pos = SEQ // tm
    full = lambda a: pl.BlockSpec(a.shape, lambda i: (0,) * a.ndim)
    row = lambda w: pl.BlockSpec((tm, w), lambda i: (i, 0))
    pos = pl.BlockSpec((tm, LANES), lambda i: (i % n_pos, 0))
    return pl.pallas_call(
        _in_proj_kernel,
        out_shape=(jax.ShapeDtypeStruct((t, A_HEADS * LANES), BF16),
                   jax.ShapeDtypeStruct((t, A_HEADS * LANES), BF16),
                   jax.ShapeDtypeStruct((t, A_HEADS * LANES), BF16),
                   jax.ShapeDtypeStruct((t, wb.shape[1]), BF16),
                   jax.ShapeDtypeStruct((t, wc.shape[1]), BF16)),
        grid=(t // tm,),
        in_specs=[row(D_MODEL), full(g1), full(wa), full(wb), full(wc), full(qg), full(kvg), full(wq), full(wqr),
                  full(wk), full(wv), full(place), full(bscale), full(cscale), full(vones), pos, pos, pos, pos],
        out_specs=(row(A_HEADS * LANES), row(A_HEADS * LANES), row(A_HEADS * LANES), row(wb.shape[1]),
                   row(wc.shape[1])),
        compiler_params=_cparams(("parallel",)),
        name="in_proj",
    )(x2, g1, wa, wb, wc, qg, kvg, wq, wqr, wk, wv, place, bscale, cscale, vones, cq, sq, ck, sk)


A_TQ = 2048
A_SUB = 256


def _mla_kernel(q_ref, k_ref, v_ref, o_ref):
    first = lax.broadcasted_iota(jnp.int32, (1, LANES), 1) < HALF
    tasks = [(r, hh) for r in range(A_TQ // A_SUB) for hh in range(2)]
    done = {}

    def rows(r):
        return slice(r * A_SUB, (r + 1) * A_SUB)

    def lanes(hh):
        return slice(hh * LANES, (hh + 1) * LANES)

    def scores(task, _):
        r, hh = task
        return _dot_nt(q_ref[rows(r), lanes(hh)], k_ref[:, lanes(hh)])

    def values(task, s):
        p = jnp.exp2(s - jnp.max(s, axis=-1, keepdims=True))
        return _dot(p.astype(BF16), v_ref[:, lanes(task[1])])

    def store(task, a):
        r, hh = task
        done[task] = a
        if hh == 1:
            a0 = done.pop((r, 0))
            a1 = done.pop((r, 1))
            l = pltpu.roll(jnp.where(first, a1, a0), HALF, axis=1)
            o_ref[rows(r), :] = (jnp.where(first, a0, a1) / l).astype(BF16)

    _software_pipeline(tasks, (scores, values, store))


def _mla_attention(qa, ka, va, batch):
    t = qa.shape[0]
    nq = SEQ // A_TQ
    return pl.pallas_call(
        _mla_kernel,
        out_shape=jax.ShapeDtypeStruct((t, A_HEADS * A_V), BF16),
        grid=(batch, A_HEADS // 2, nq),
        in_specs=[pl.BlockSpec((A_TQ, 2 * LANES), lambda b, p, i: (b * nq + i, p)),
                  pl.BlockSpec((SEQ, 2 * LANES), lambda b, p, i: (b, p)),
                  pl.BlockSpec((SEQ, 2 * LANES), lambda b, p, i: (b, p))],
        out_specs=pl.BlockSpec((A_TQ, LANES), lambda b, p, i: (b * nq + i, p)),
        compiler_params=_cparams(("parallel", "parallel", "arbitrary")),
        name="mla_attention",
    )(qa, ka, va)


def _software_pipeline(tasks, stages):
    state = [None] * len(tasks)
    for step in range(len(tasks) + len(stages) - 1):
        for s, stage in enumerate(stages):
            i = step - s
            if 0 <= i < len(tasks):
                state[i] = stage(tasks[i], state[i])


def _first_half():
    return lax.broadcasted_iota(jnp.int32, (1, LANES), 1) < HALF


def _pair_scores(q2, k2, biases):
    first = _first_half()
    zero = jnp.zeros_like(q2)
    tq = q2.shape[0]
    stacked = jnp.concatenate([jnp.where(first, q2, zero), jnp.where(first, zero, q2)], axis=0)
    s = _dot_nt(stacked, k2)
    return [s[:tq] + biases[0], s[tq:] + biases[1]]


def _pair_values(scores, v2, sinks=None):
    first = _first_half()
    one = jnp.ones_like(v2)
    res = []
    for hh, s in enumerate(scores):
        own = first if hh == 0 else jnp.logical_not(first)
        m = jnp.max(s, axis=-1, keepdims=True)
        if sinks is not None:
            m = jnp.maximum(m, sinks[hh])
        a = _dot(jnp.exp2(s - m).astype(BF16), jnp.where(own, v2, one))
        if sinks is not None:
            a = a + jnp.where(own, 0.0, jnp.exp2(sinks[hh] - m))
        res.append((m, a))
    return res


def _pair_merge(res):
    first = _first_half()
    (m0, a0), (m1, a1) = res
    return jnp.where(first, m0, m1), jnp.where(first, a1, a0), jnp.where(first, a0, a1)


def _dilated_kernel(q1, k1, v1, bw_ref, bn_ref, o_ref, nat32, by4_32, by4, by16, stream, by4_out, merged):
    n_blocks = SEQ // QBLK
    d4, d16 = B_GROUPS[1][1], B_GROUPS[2][1]
    len4, len16 = SEQ // d4, SEQ // d16
    for t, ref in enumerate((q1, k1, v1)):
        nat32[t] = ref[...].astype(F32)
        for g in range(d4):
            run = nat32[t, pl.ds(g, len4, stride=d4), :]
            by4_32[t, g * len4:(g + 1) * len4, :] = run
            by4[t, g] = run.astype(BF16)
        for g in range(d16):
            start = (g % d4) * len4 + g // d4
            by16[t, g] = by4_32[t, pl.ds(start, len16, stride=d4), :].astype(BF16)
    sources = ((q1, k1, v1), tuple(by4.at[t] for t in range(3)), tuple(by16.at[t] for t in range(3)))

    def rows(j):
        return slice(j * QBLK, (j + 1) * QBLK)

    def view(ref, gi, sl):
        dil = B_GROUPS[gi][1]
        if dil == 1:
            return ref[sl, :]
        length = SEQ // dil
        g = sl.start // length
        return ref[g, sl.start - g * length:sl.stop - g * length, :]

    def keys(gi, j):
        length = SEQ // B_GROUPS[gi][1]
        per_stream = length // QBLK
        if per_stream == 1:
            return rows(j)
        i = j % per_stream
        start = (j - i) * QBLK + min(max(i * QBLK - B_RADIUS, 0), length - 2 * QBLK)
        return slice(start, start + 2 * QBLK)

    def scores(task, _):
        gi, j = task
        per_stream = SEQ // B_GROUPS[gi][1] // QBLK
        if per_stream == 1:
            biases = (bn_ref[0, 0], bn_ref[1, 0])
        else:
            i = j % per_stream
            var = gi * 3 + (0 if i == 0 else 2 if i == per_stream - 1 else 1)
            biases = (bw_ref[0, var], bw_ref[1, var])
        q_ref, k_ref, _ = sources[gi]
        return _pair_scores(view(q_ref, gi, rows(j)), view(k_ref, gi, keys(gi, j)), biases)

    def values(task, s):
        gi, j = task
        return _pair_values(s, view(sources[gi][2], gi, keys(gi, j)))

    def store(task, res):
        gi, j = task
        for slot, value in enumerate(_pair_merge(res)):
            if gi == 0:
                merged[0, slot, rows(j), :] = value
            else:
                stream[gi - 1, slot, rows(j), :] = value

    _software_pipeline([(gi, j) for gi in range(len(B_GROUPS)) for j in range(n_blocks)], (scores, values, store))
    for slot in range(3):
        for g in range(d16):
            start = (g % d4) * len4 + g // d4
            by4_out[slot, pl.ds(start, len16, stride=d4), :] = stream[1, slot, g * len16:(g + 1) * len16, :]
        for g in range(d4):
            run = slice(g * len4, (g + 1) * len4)
            merged[1, slot, pl.ds(g, len4, stride=d4), :] = stream[0, slot, run, :]
            merged[2, slot, pl.ds(g, len4, stride=d4), :] = by4_out[slot, run, :]

    def merge(c, carry):
        rows = pl.ds(pl.multiple_of(c * QBLK, QBLK), QBLK)
        m = [merged[gi, 0, rows, :] for gi in range(3)]
        top = jnp.maximum(jnp.maximum(m[0], m[1]), m[2])
        w = [jnp.exp2(mg - top) for mg in m]
        l = [pltpu.roll(merged[gi, 1, rows, :], HALF, axis=1) for gi in range(3)]
        num = w[0] * merged[0, 2, rows, :] + w[1] * merged[1, 2, rows, :] + w[2] * merged[2, 2, rows, :]
        den = w[0] * l[0] + w[1] * l[1] + w[2] * l[2]
        o_ref[rows, :] = (num / den).astype(BF16)
        return carry

    lax.fori_loop(0, n_blocks, merge, 0)


def _dilated_attention(zb, bias_wide, bias_narrow, batch):
    t = zb.shape[0]
    n_pairs = B_HEADS // 2
    nw = bias_wide.shape[1]
    d4, d16 = B_GROUPS[1][1], B_GROUPS[2][1]
    seq_block = lambda off: pl.BlockSpec((SEQ, LANES), lambda p, b: (b, off + p))
    return pl.pallas_call(
        _dilated_kernel,
        out_shape=jax.ShapeDtypeStruct((t, B_HEADS * B_HEAD_DIM), BF16),
        grid=(n_pairs, batch),
        in_specs=[seq_block(0), seq_block(n_pairs), seq_block(2 * n_pairs),
                  pl.BlockSpec((2, nw, QBLK, 2 * QBLK), lambda p, b: (p, 0, 0, 0)),
                  pl.BlockSpec((2, 1, QBLK, QBLK), lambda p, b: (p, 0, 0, 0))],
        out_specs=pl.BlockSpec((SEQ, LANES), lambda p, b: (b, p)),
        scratch_shapes=[pltpu.VMEM((3, SEQ, LANES), F32), pltpu.VMEM((3, SEQ, LANES), F32),
                        pltpu.VMEM((3, d4, SEQ // d4, LANES), BF16), pltpu.VMEM((3, d16, SEQ // d16, LANES), BF16),
                        pltpu.VMEM((2, 3, SEQ, LANES), F32), pltpu.VMEM((3, SEQ, LANES), F32),
                        pltpu.VMEM((3, 3, SEQ, LANES), F32)],
        compiler_params=_cparams(("arbitrary", "arbitrary")),
        name="dilated_attention",
    )(zb, zb, zb, bias_wide, bias_narrow)


def _window_kernel(sink_ref, q_ref, k_ref, v_ref, bias_ref, o_ref):
    pair = pl.program_id(0)
    sinks = (sink_ref[pair] * LOG2E, sink_ref[pair + C_HEADS // 2] * LOG2E)
    n_blocks = SEQ // QBLK

    def keys(i):
        start = min(max((i - 1) * QBLK, 0), SEQ - 3 * QBLK)
        return slice(start, start + 3 * QBLK)

    def scores(i, _):
        var = 0 if i == 0 else 2 if i == n_blocks - 1 else 1
        return _pair_scores(q_ref[i * QBLK:(i + 1) * QBLK, :], k_ref[keys(i), :],
                            (bias_ref[0, var], bias_ref[1, var]))

    def values(i, s):
        return _pair_values(s, v_ref[keys(i), :], sinks)

    def store(i, res):
        _, l_swapped, o = _pair_merge(res)
        o_ref[i * QBLK:(i + 1) * QBLK, :] = (o / pltpu.roll(l_swapped, HALF, axis=1)).astype(BF16)

    _software_pipeline(list(range(n_blocks)), (scores, values, store))


def _window_attention(zc, sink, bias, batch):
    t = zc.shape[0]
    n_pairs = C_HEADS // 2
    return pl.pallas_call(
        _window_kernel,
        out_shape=jax.ShapeDtypeStruct((t, C_HEADS * C_HEAD_DIM), BF16),
        grid=(n_pairs, batch),
        in_specs=[pl.BlockSpec(memory_space=pltpu.SMEM),
                  pl.BlockSpec((SEQ, LANES), lambda p, b: (b, p)),
                  pl.BlockSpec((SEQ, LANES), lambda p, b: (b, n_pairs)),
                  pl.BlockSpec((SEQ, LANES), lambda p, b: (b, n_pairs + 1)),
                  pl.BlockSpec((None, 2, 3, QBLK, 3 * QBLK), lambda p, b: (p, 0, 0, 0, 0))],
        out_specs=pl.BlockSpec((SEQ, LANES), lambda p, b: (b, p)),
        compiler_params=_cparams(("arbitrary", "arbitrary")),
        name="window_attention",
    )(sink, zc, zc, zc, bias)


MERGE_TM = 512


def _merge_kernel(x_ref, g1_ref, ya_ref, yb_ref, yc_ref, wg_ref, wbr_ref, wo_ref, o_ref):
    x = x_ref[...]
    h = _rms(x, g1_ref[...]).astype(BF16)
    merged = None
    for b, y_ref in enumerate((ya_ref, yb_ref, yc_ref)):
        gate = jax.nn.sigmoid(_dot(h, wg_ref[:, b * D_MODEL:(b + 1) * D_MODEL]))
        term = gate * _dot(y_ref[...], wbr_ref[b])
        merged = term if merged is None else merged + term
    o_ref[...] = x + _dot(merged.astype(BF16), wo_ref[...])


def _merge(x2, g1, ya, yb, yc, wg, wbr, wo):
    t = x2.shape[0]
    tm = MERGE_TM
    full = lambda a: pl.BlockSpec(a.shape, lambda i: (0,) * a.ndim)
    row = lambda w: pl.BlockSpec((tm, w), lambda i: (i, 0))
    return pl.pallas_call(
        _merge_kernel,
        out_shape=jax.ShapeDtypeStruct((t, D_MODEL), F32),
        grid=(t // tm,),
        in_specs=[row(D_MODEL), full(g1), row(ya.shape[1]), row(yb.shape[1]), row(yc.shape[1]),
                  full(wg), full(wbr), full(wo)],
        out_specs=row(D_MODEL),
        compiler_params=_cparams(("parallel",)),
        name="gate_merge",
    )(x2, g1, ya, yb, yc, wg, wbr, wo)


TOP_K = 2
ROW_SUBLANES = D_MODEL // LANES
MOE_TM = 512
EXPERT_TF = D_FF_EXPERT // 2


def _router_kernel(x_ref, g_ref, wr_ref, eid_ref, rank_ref, gate_ref, cnt_ref, carry):
    @pl.when(pl.program_id(0) == 0)
    def _():
        carry[...] = jnp.zeros_like(carry)

    h = _rms(x_ref[...], g_ref[...])
    w = wr_ref[...]
    h_hi = h.astype(BF16)
    w_hi = w.astype(BF16)
    h_lo = (h - h_hi.astype(F32)).astype(BF16)
    w_lo = (w - w_hi.astype(F32)).astype(BF16)
    logits = _dot(h_hi, w_hi) + (_dot(h_lo, w_hi) + _dot(h_hi, w_lo))
    idx = lax.broadcasted_iota(jnp.int32, logits.shape, 1)
    m1 = jnp.max(logits, axis=-1, keepdims=True)
    i1 = jnp.min(jnp.where(logits == m1, idx, N_EXPERTS), axis=-1, keepdims=True)
    first = idx == i1
    rest = jnp.where(first, -jnp.inf, logits)
    m2 = jnp.max(rest, axis=-1, keepdims=True)
    i2 = jnp.min(jnp.where(rest == m2, idx, N_EXPERTS), axis=-1, keepdims=True)
    second = idx == i2
    e = jnp.exp(m2 - m1)
    den = 1.0 + e
    chosen = jnp.where(jnp.logical_or(first, second), 1.0, 0.0)
    tm = chosen.shape[0]
    earlier = (lax.broadcasted_iota(jnp.int32, (tm, tm), 0) > lax.broadcasted_iota(jnp.int32, (tm, tm), 1))
    before = _dot(jnp.where(earlier, 1.0, 0.0).astype(BF16), chosen.astype(BF16)) + carry[...]
    r1 = jnp.sum(jnp.where(first, before, 0.0), axis=-1, keepdims=True)
    r2 = jnp.sum(jnp.where(second, before, 0.0), axis=-1, keepdims=True)
    carry[...] += jnp.sum(chosen, axis=0, keepdims=True)
    slot0 = lax.broadcasted_iota(jnp.int32, (tm, TOP_K), 1) == 0
    eid_ref[...] = jnp.where(slot0, i1, i2)
    rank_ref[...] = jnp.where(slot0, r1, r2).astype(jnp.int32)
    gate_ref[...] = jnp.where(slot0, 1.0 / den, e / den)
    cnt_ref[...] = carry[...].astype(jnp.int32)


def _router(x2, g, wr):
    t = x2.shape[0]
    tm = MOE_TM
    pair = pl.BlockSpec((tm, TOP_K), lambda i: (i, 0))
    return pl.pallas_call(
        _router_kernel,
        out_shape=(jax.ShapeDtypeStruct((t, TOP_K), jnp.int32), jax.ShapeDtypeStruct((t, TOP_K), jnp.int32),
                   jax.ShapeDtypeStruct((t, TOP_K), F32), jax.ShapeDtypeStruct((1, N_EXPERTS), jnp.int32)),
        grid=(t // tm,),
        in_specs=[pl.BlockSpec((tm, D_MODEL), lambda i: (i, 0)),
                  pl.BlockSpec(g.shape, lambda i: (0, 0)),
                  pl.BlockSpec(wr.shape, lambda i: (0, 0))],
        out_specs=(pair, pair, pair, pl.BlockSpec((1, N_EXPERTS), lambda i: (0, 0))),
        scratch_shapes=[pltpu.VMEM((1, N_EXPERTS), F32)],
        compiler_params=_cparams(("arbitrary",)),
        name="router",
    )(x2, g, wr)


def _row_tile(ref, row):
    return ref.at[pl.ds(pl.multiple_of(row * ROW_SUBLANES, ROW_SUBLANES), ROW_SUBLANES), :]


def _dispatch_kernel(tail_start_ref, tail_len_ref, na_ref, dest_ref, x_ref, g_ref, hs_ref, stage, sems):
    i = pl.program_id(0)
    n_steps = pl.num_programs(0)
    tm = x_ref.shape[0]
    slot = i % 2
    rows_per_slot = stage.shape[1]

    def drain(s):
        for _ in range(TOP_K):
            pltpu.make_async_copy(stage.at[s], hs_ref.at[pl.ds(0, rows_per_slot), :], sems.at[s]).wait()

    @pl.when(i >= 2)
    def _():
        drain(slot)

    h = _rms(x_ref[...], g_ref[...])
    for s in range(ROW_SUBLANES):
        stage[slot, pl.ds(s, tm, stride=ROW_SUBLANES), :] = h[:, s * LANES:(s + 1) * LANES]

    def start(t, carry):
        for k in range(TOP_K):
            pltpu.make_async_copy(_row_tile(stage.at[slot], t), _row_tile(hs_ref, dest_ref[TOP_K * t + k]),
                                  sems.at[slot]).start(priority=k)
        return carry

    lax.fori_loop(0, tm, start, 0, unroll=8)

    @pl.when(i == n_steps - 1)
    def _():
        @pl.when(n_steps >= 2)
        def _():
            drain(1 - slot)

        drain(slot)
        zeros = stage.at[0]
        zeros[...] = jnp.zeros_like(zeros)
        for e in range(N_EXPERTS):
            def tail(r, e=e):
                return pltpu.make_async_copy(_row_tile(zeros, 0), _row_tile(hs_ref, tail_start_ref[e] + r), sems.at[0])

            lax.fori_loop(0, tail_len_ref[e], lambda r, c: (tail(r).start(), c)[1], 0)
            lax.fori_loop(0, tail_len_ref[e], lambda r, c: (tail(r).wait(), c)[1], 0)
        n_tiles = hs_ref.shape[0] // rows_per_slot

        def spare(j):
            rows = pl.ds(pl.multiple_of(j * rows_per_slot, rows_per_slot), rows_per_slot)
            return pltpu.make_async_copy(zeros, hs_ref.at[rows, :], sems.at[0])

        lax.fori_loop(na_ref[0], n_tiles, lambda j, c: (spare(j).start(), c)[1], 0)
        lax.fori_loop(na_ref[0], n_tiles, lambda j, c: (spare(j).wait(), c)[1], 0)


def _dispatch(x2, g, dest, tail_start, tail_len, n_active, n_rows):
    t = x2.shape[0]
    tm = MOE_TM
    return pl.pallas_call(
        _dispatch_kernel,
        out_shape=jax.ShapeDtypeStruct((n_rows * ROW_SUBLANES, LANES), F32),
        grid_spec=pltpu.PrefetchScalarGridSpec(
            num_scalar_prefetch=3,
            grid=(t // tm,),
            in_specs=[pl.BlockSpec((TOP_K * tm,), lambda i, ts, tl, na: (i,), memory_space=pltpu.SMEM),
                      pl.BlockSpec((tm, D_MODEL), lambda i, ts, tl, na: (i, 0)),
                      pl.BlockSpec(g.shape, lambda i, ts, tl, na: (0, 0))],
            out_specs=pl.BlockSpec(memory_space=pl.ANY),
            scratch_shapes=[pltpu.VMEM((2, tm * ROW_SUBLANES, LANES), F32), pltpu.SemaphoreType.DMA((2,))]),
        compiler_params=_cparams(("arbitrary",)),
        name="dispatch",
    )(tail_start, tail_len, n_active, dest, x2, g)


def _experts_kernel(te_ref, na_ref, hs_ref, wg_ref, wu_ref, wd_ref, ys_ref, h_scr, acc_scr):
    i = pl.program_id(0)
    f = pl.program_id(1)
    active = i < na_ref[0]
    tm = h_scr.shape[0]

    @pl.when(jnp.logical_and(active, f == 0))
    def _():
        for s in range(ROW_SUBLANES):
            h_scr[:, s * LANES:(s + 1) * LANES] = hs_ref[pl.ds(s, tm, stride=ROW_SUBLANES), :].astype(BF16)

    @pl.when(active)
    def _():
        h = h_scr[...]
        act = jax.nn.silu(_dot(h, wg_ref[...])) * _dot(h, wu_ref[...])
        part = _dot(act.astype(BF16), wd_ref[...])

        @pl.when(f == 0)
        def _():
            acc_scr[...] = part

        @pl.when(f != 0)
        def _():
            acc_scr[...] += part

    @pl.when(jnp.logical_and(active, f == pl.num_programs(1) - 1))
    def _():
        for s in range(ROW_SUBLANES):
            ys_ref[pl.ds(s, tm, stride=ROW_SUBLANES), :] = acc_scr[:, s * LANES:(s + 1) * LANES]

    @pl.when(jnp.logical_and(jnp.logical_not(active), f == pl.num_programs(1) - 1))
    def _():
        ys_ref[...] = jnp.zeros_like(ys_ref)


def _experts(hs, tile_expert, n_active, wg, wu, wd):
    tm = MOE_TM
    tf = EXPERT_TF
    n_tiles = hs.shape[0] // (tm * ROW_SUBLANES)
    n_f = wg.shape[2] // tf
    rows = lambda i, f, te, na: (jnp.minimum(i, na[0] - 1), 0)
    col = lambda i, f, na: jnp.where(i < na[0], f, n_f - 1)
    return pl.pallas_call(
        _experts_kernel,
        out_shape=jax.ShapeDtypeStruct(hs.shape, F32),
        grid_spec=pltpu.PrefetchScalarGridSpec(
            num_scalar_prefetch=2,
            grid=(n_tiles, n_f),
            in_specs=[pl.BlockSpec((tm * ROW_SUBLANES, LANES), rows),
                      pl.BlockSpec((None, D_MODEL, tf), lambda i, f, te, na: (te[i], 0, col(i, f, na))),
                      pl.BlockSpec((None, D_MODEL, tf), lambda i, f, te, na: (te[i], 0, col(i, f, na))),
                      pl.BlockSpec((None, tf, D_MODEL), lambda i, f, te, na: (te[i], col(i, f, na), 0))],
            out_specs=pl.BlockSpec((tm * ROW_SUBLANES, LANES), lambda i, f, te, na: (i, 0)),
            scratch_shapes=[pltpu.VMEM((tm, D_MODEL), BF16), pltpu.VMEM((tm, D_MODEL), F32)]),
        compiler_params=_cparams(("arbitrary", "arbitrary")),
        name="experts",
    )(tile_expert, n_active, hs, wg, wu, wd)


def _combine_kernel(dest_ref, next_dest_ref, x_ref, gate_ref, gf_ref, ys_ref, o_ref, bufs, sems, *, final):
    i = pl.program_id(0)
    tm = x_ref.shape[0]
    slot = i % 2

    def gather(idx_ref, s):
        def start(t, carry):
            for k in range(TOP_K):
                pltpu.make_async_copy(_row_tile(ys_ref, idx_ref[TOP_K * t + k]), _row_tile(bufs.at[s, k], t),
                                      sems.at[s]).start(priority=k)
            return carry

        lax.fori_loop(0, tm, start, 0, unroll=8)

    @pl.when(i == 0)
    def _():
        gather(dest_ref, slot)

    @pl.when(i + 1 < pl.num_programs(0))
    def _():
        gather(next_dest_ref, 1 - slot)

    for k in range(TOP_K):
        pltpu.make_async_copy(ys_ref.at[pl.ds(0, bufs.shape[2]), :], bufs.at[slot, k], sems.at[slot]).wait()
    gate = gate_ref[...]
    g0 = gate[:, 0:1]
    g1 = gate[:, 1:2]
    ss = jnp.zeros((tm, 1), F32)
    for s in range(ROW_SUBLANES):
        sl = slice(s * LANES, (s + 1) * LANES)
        rows = pl.ds(s, tm, stride=ROW_SUBLANES)
        y = x_ref[:, sl] + (g0 * bufs[slot, 0, rows, :] + g1 * bufs[slot, 1, rows, :])
        o_ref[:, sl] = y
        ss = ss + jnp.sum(y * y, axis=-1, keepdims=True)
    if final:
        o_ref[...] = o_ref[...] * lax.rsqrt(ss / D_MODEL + EPS) * gf_ref[...]


def _combine(x2, dest, gates, gf, ys, *, final):
    t = x2.shape[0]
    tm = MOE_TM
    return pl.pallas_call(
        functools.partial(_combine_kernel, final=final),
        out_shape=jax.ShapeDtypeStruct((t, D_MODEL), F32),
        grid=(t // tm,),
        in_specs=[pl.BlockSpec((TOP_K * tm,), lambda i: (i,), memory_space=pltpu.SMEM),
                  pl.BlockSpec((TOP_K * tm,), lambda i: (jnp.minimum(i + 1, t // tm - 1),),
                               memory_space=pltpu.SMEM),
                  pl.BlockSpec((tm, D_MODEL), lambda i: (i, 0)),
                  pl.BlockSpec((tm, TOP_K), lambda i: (i, 0)),
                  pl.BlockSpec(gf.shape, lambda i: (0, 0)),
                  pl.BlockSpec(memory_space=pl.ANY)],
        out_specs=pl.BlockSpec((tm, D_MODEL), lambda i: (i, 0)),
        scratch_shapes=[pltpu.VMEM((2, TOP_K, tm * ROW_SUBLANES, LANES), F32), pltpu.SemaphoreType.DMA((2,))],
        compiler_params=_cparams(("arbitrary",)),
        name="combine",
    )(dest, dest, x2, gates, gf, ys)


def _moe(x2, g, wr, wg, wu, wd, gf, *, final):
    t = x2.shape[0]
    tm = MOE_TM
    eid, rank, gates, counts = _router(x2, g, wr)
    counts = counts[0]
    padded = (counts + tm - 1) // tm * tm
    group_end = jnp.cumsum(padded)
    group_start = group_end - padded
    dest = (group_start[eid] + rank).reshape(-1)
    n_tiles = TOP_K * t // tm + N_EXPERTS
    n_active = (group_end[-1:] // tm).astype(jnp.int32)
    tile_row = jnp.arange(n_tiles, dtype=jnp.int32) * tm
    tile_row = jnp.minimum(tile_row, group_end[-1] - tm)
    tile_expert = jnp.sum(tile_row[:, None] >= group_end[None, :], axis=1).astype(jnp.int32)
    hs = _dispatch(x2, g, dest, group_start + counts, padded - counts, n_active, n_tiles * tm)
    ys = _experts(hs, tile_expert, n_active, wg, wu, wd)
    return _combine(x2, dest, gates, gf, ys, final=final)


FFN_TM = 512
FFN_TF = D_FF_DENSE


def _ffn_kernel(x_ref, g_ref, wg_ref, wu_ref, wd_ref, gf_ref, o_ref, h_scr, acc_scr, *, final):
    f = pl.program_id(1)

    @pl.when(f == 0)
    def _():
        h_scr[...] = _rms(x_ref[...], g_ref[...]).astype(BF16)

    h = h_scr[...]
    act = jax.nn.silu(_dot(h, wg_ref[...])) * _dot(h, wu_ref[...])
    part = _dot(act.astype(BF16), wd_ref[...])

    @pl.when(f == 0)
    def _():
        acc_scr[...] = part

    @pl.when(f != 0)
    def _():
        acc_scr[...] += part

    @pl.when(f == pl.num_programs(1) - 1)
    def _():
        y = x_ref[...] + acc_scr[...]
        if final:
            y = _rms(y, gf_ref[...])
        o_ref[...] = y


def _ffn(x2, g, wg, wu, wd, gf, *, final):
    t = x2.shape[0]
    tm = FFN_TM
    tf = FFN_TF
    d_ff = wg.shape[1]
    return pl.pallas_call(
        functools.partial(_ffn_kernel, final=final),
        out_shape=jax.ShapeDtypeStruct((t, D_MODEL), F32),
        grid=(t // tm, d_ff // tf),
        in_specs=[pl.BlockSpec((tm, D_MODEL), lambda i, f: (i, 0)),
                  pl.BlockSpec(g.shape, lambda i, f: (0, 0)),
                  pl.BlockSpec((D_MODEL, tf), lambda i, f: (0, f)),
                  pl.BlockSpec((D_MODEL, tf), lambda i, f: (0, f)),
                  pl.BlockSpec((tf, D_MODEL), lambda i, f: (f, 0)),
                  pl.BlockSpec(gf.shape, lambda i, f: (0, 0))],
        out_specs=pl.BlockSpec((tm, D_MODEL), lambda i, f: (i, 0)),
        scratch_shapes=[pltpu.VMEM((tm, D_MODEL), BF16), pltpu.VMEM((tm, D_MODEL), F32)],
        compiler_params=_cparams(("parallel", "arbitrary")),
        name="swiglu",
    )(x2, g, wg, wu, wd, gf)


def _rot_cols(w):
    half = A_ROPE // 2
    return jnp.concatenate([-w[..., half:], w[..., :half]], axis=-1)


def _prep_layer(w_in, w_uq, w_ukv, w_br_a, w_br_b, w_br_c):
    o = 0
    w_cq = w_in[:, o:o + A_Q_LORA]; o += A_Q_LORA
    w_ckv = w_in[:, o:o + A_KV_LORA]; o += A_KV_LORA
    w_kpe = w_in[:, o:o + A_ROPE]; o += A_ROPE
    nb = B_HEADS * B_HEAD_DIM
    w_b = w_in[:, o:o + 3 * nb]; o += 3 * nb
    nq = C_HEADS * C_HEAD_DIM
    nkv = C_KV_HEADS * C_HEAD_DIM
    w_c = w_in[:, o:o + nq + 2 * nkv]; o += nq + 2 * nkv
    w_g = w_in[:, o:]
    pad = jnp.zeros((D_MODEL, LANES - A_ROPE), F32)
    wa = jnp.concatenate([w_cq, w_ckv, w_kpe, pad, _rot_cols(w_kpe), pad], axis=1)
    wb = jnp.concatenate([w_b[:, :nb] * (B_HEAD_DIM ** -0.5), w_b[:, nb:]], axis=1)
    q_c = w_c[:, :nq].reshape(D_MODEL, 2, C_HEADS // 2, C_HEAD_DIM).transpose(0, 2, 1, 3).reshape(D_MODEL, nq)
    wc = jnp.concatenate([q_c * (C_HEAD_DIM ** -0.5), w_c[:, nq:]], axis=1)
    uq = w_uq.reshape(A_Q_LORA, A_HEADS, A_NOPE + A_ROPE)
    z32 = jnp.zeros((A_Q_LORA, A_HEADS, LANES - A_NOPE - A_ROPE), F32)
    wq = jnp.concatenate([uq, z32], axis=-1).reshape(A_Q_LORA, A_HEADS * LANES)
    wqr = jnp.concatenate([jnp.zeros((A_Q_LORA, A_HEADS, A_NOPE), F32), _rot_cols(uq[..., A_NOPE:]), z32],
                          axis=-1).reshape(A_Q_LORA, A_HEADS * LANES)
    ukv = w_ukv.reshape(A_KV_LORA, A_HEADS, A_NOPE + A_V)
    wk = jnp.concatenate([ukv[..., :A_NOPE], jnp.zeros((A_KV_LORA, A_HEADS, LANES - A_NOPE), F32)],
                         axis=-1).reshape(A_KV_LORA, A_HEADS * LANES)
    uv = ukv[..., A_NOPE:].reshape(A_KV_LORA, A_HEADS // 2, 2, A_V)
    zv = jnp.zeros_like(uv[:, :, 0])
    wv = jnp.stack([jnp.concatenate([uv[:, :, 0], zv], axis=-1), jnp.concatenate([zv, uv[:, :, 1]], axis=-1)],
                   axis=2).reshape(A_KV_LORA, A_HEADS * LANES)
    w_br_c = w_br_c.reshape(2, C_HEADS // 2, C_HEAD_DIM, D_MODEL).transpose(1, 0, 2, 3).reshape(nq, D_MODEL)
    wbr = jnp.stack([w_br_a, w_br_b, w_br_c])
    cast = lambda a: a.astype(BF16)
    return dict(wa=cast(wa), wb=cast(wb), wc=cast(wc), wg=cast(w_g), wq=cast(wq), wqr=cast(wqr), wk=cast(wk),
                wv=cast(wv), wbr=cast(wbr))


def _rope_tables():
    half = A_ROPE // 2
    inv = ROPE_THETA ** (-jnp.arange(half, dtype=F32) / half)
    ang = jnp.arange(SEQ, dtype=F32)[:, None] * inv[None, :]
    cos2 = jnp.tile(jnp.cos(ang), (1, 2))
    sin2 = jnp.tile(jnp.sin(ang), (1, 2))
    ones = jnp.ones((SEQ, A_NOPE), F32)
    z = lambda w: jnp.zeros((SEQ, w), F32)
    cq = jnp.concatenate([ones, cos2, z(LANES - A_NOPE - A_ROPE)], axis=1)
    sq = jnp.concatenate([z(A_NOPE), sin2, z(LANES - A_NOPE - A_ROPE)], axis=1)
    ck = jnp.concatenate([cos2, z(LANES - A_ROPE)], axis=1)
    sk = jnp.concatenate([sin2, z(LANES - A_ROPE)], axis=1)
    return cq, sq, ck, sk


def _rope_placement():
    place = np.zeros((LANES, A_HEADS * LANES), np.float32)
    for h in range(A_HEADS):
        for j in range(A_ROPE):
            place[j, h * LANES + A_NOPE + j] = 1.0
    return jnp.asarray(place, dtype=BF16)


def kernel(x, norm1_g, w_in, q_norm_g, w_uq, kv_norm_g, w_ukv, sink_logit, rel_bias, w_branch_a, w_branch_b,
           w_branch_c, w_out, norm2_g, ffn_w_gate, ffn_w_up, ffn_w_down, router_w, exp_w_gate, exp_w_up,
           exp_w_down, final_g):
    batch, seq, d = x.shape
    assert seq == SEQ and d == D_MODEL
    t = batch * seq
    x2 = x.reshape(t, d)

    wide_idx, narrow_idx = _dilated_index_tables()
    bias_wide = _bias_tables(rel_bias, wide_idx, 0, B_HEADS)
    bias_narrow = _bias_tables(rel_bias, narrow_idx, 0, B_HEADS)
    bias_c = _bias_tables(rel_bias, _window_index_tables(), B_HEADS, C_HEADS)
    bias_c = bias_c.reshape(2, C_HEADS // 2, *bias_c.shape[1:]).transpose(1, 0, 2, 3, 4)

    cq, sq, ck, sk = _rope_tables()
    place = _rope_placement()
    row = lambda v: v.reshape(1, -1)
    q_then = lambda nq, rest: jnp.asarray(np.concatenate([np.full(nq, LOG2E), np.ones(rest)])[None], F32)
    bscale = q_then(B_HEADS * B_HEAD_DIM, 2 * B_HEADS * B_HEAD_DIM)
    cscale = q_then(C_HEADS * C_HEAD_DIM, 2 * C_KV_HEADS * C_HEAD_DIM)
    vones = jnp.asarray(np.tile(np.repeat([0.0, 1.0, 1.0, 0.0], A_V), A_HEADS // 2)[None], F32)

    for l in range(DEPTH):
        w = _prep_layer(w_in[l], w_uq[l], w_ukv[l], w_branch_a[l], w_branch_b[l], w_branch_c[l])
        g1 = row(norm1_g[l])
        qa, ka, va, zb, zc = _in_proj(x2, g1, w["wa"], w["wb"], w["wc"], row(q_norm_g[l]), row(kv_norm_g[l]),
                                      w["wq"], w["wqr"], w["wk"], w["wv"], place, bscale, cscale, vones,
                                      cq, sq, ck, sk)
        ya = _mla_attention(qa, ka, va, batch)
        yb = _dilated_attention(zb, bias_wide, bias_narrow, batch)
        yc = _window_attention(zc, sink_logit[l], bias_c, batch)
        x2 = _merge(x2, g1, ya, yb, yc, w["wg"], w["wbr"], w_out[l].astype(BF16))
        g2 = row(norm2_g[l])
        final = l == DEPTH - 1
        i = l // 2
        if l % 2 == 0:
            x2 = _ffn(x2, g2, ffn_w_gate[i].astype(BF16), ffn_w_up[i].astype(BF16), ffn_w_down[i].astype(BF16),
                      row(final_g), final=final)
        else:
            x2 = _moe(x2, g2, router_w[i], exp_w_gate[i].astype(BF16), exp_w_up[i].astype(BF16),
                      exp_w_down[i].astype(BF16), row(final_g), final=final)
    return x2.reshape(batch, seq, d)
```

```python
import functools
import math

import numpy as np
import jax
import jax.numpy as jnp
from jax import lax
from jax.experimental import pallas as pl
from jax.experimental.pallas import tpu as pltpu

F32 = jnp.float32
BF16 = jnp.bfloat16

D_MODEL = 1024
SEQ = 2048
DEPTH = 2
A_HEADS = 8
A_NOPE = 64
A_ROPE = 32
A_V = 64
A_Q_LORA = 256
A_KV_LORA = 128
ROPE_THETA = 10000.0
B_HEADS = 8
B_HEAD_DIM = 64
B_GROUPS = ((128, 1), (512, 4), (2048, 16))
C_HEADS = 8
C_KV_HEADS = 2
C_HEAD_DIM = 64
C_RADIUS = 128
N_BUCKETS = 32
MAX_DISTANCE = 1024
D_FF_DENSE = 2816
N_EXPERTS = 8
D_FF_EXPERT = 3584
EPS = 1e-6
NEG_INF = -1e30

LANES = 128
HALF = 64
QBLK = 128
B_RADIUS = 64
VMEM_LIMIT = 56 * 1024 * 1024

A_SCALE = (A_NOPE + A_ROPE) ** -0.5
LOG2E = math.log2(math.e)
ZA_COLS = A_Q_LORA + A_KV_LORA + 2 * LANES


def _cparams(sem):
    return pltpu.CompilerParams(dimension_semantics=sem, vmem_limit_bytes=VMEM_LIMIT)


def _rms(x, g):
    return x * lax.rsqrt(jnp.mean(x * x, axis=-1, keepdims=True) + EPS) * g


def _dot(a, b):
    return jnp.dot(a, b, preferred_element_type=F32)


def _dot_nt(a, b):
    return lax.dot_general(a, b, (((1,), (1,)), ((), ())), preferred_element_type=F32)


def _t5_bucket_np(rel):
    nb = N_BUCKETS // 2
    max_exact = nb // 2
    ret = np.where(rel > 0, nb, 0)
    n = np.abs(rel)
    nf = np.maximum(n, 1).astype(np.float64)
    t = np.log(nf / max_exact) / math.log(MAX_DISTANCE / max_exact) * (nb - max_exact)
    frac = np.abs(t - np.round(t))
    assert np.all((frac > 1e-5) | (n <= max_exact) | (n >= MAX_DISTANCE))
    large = max_exact + np.floor(np.round(t, 9)).astype(np.int64)
    large = np.minimum(large, nb - 1)
    return (ret + np.where(n < max_exact, n, large)).astype(np.int32)


def _band_index(tq, tk, shift, radius, dil):
    rel = np.arange(tk)[None, :] - np.arange(tq)[:, None] + shift
    idx = _t5_bucket_np(rel * dil)
    return np.where(np.abs(rel) <= radius, idx, -1).astype(np.int32)


def _dilated_index_tables():
    wide = []
    for window, dil in B_GROUPS[:2]:
        for shift in (0, -B_RADIUS, -2 * B_RADIUS):
            wide.append(_band_index(QBLK, 2 * QBLK, shift, B_RADIUS, dil))
    narrow = [_band_index(QBLK, QBLK, 0, B_RADIUS, B_GROUPS[2][1])]
    return np.stack(wide), np.stack(narrow)


def _window_index_tables():
    return np.stack([_band_index(QBLK, 3 * QBLK, shift, C_RADIUS, 1) for shift in (0, -QBLK, -2 * QBLK)])


def _bias_kernel(tab_ref, idx_ref, o_ref, *, col0):
    col = pl.program_id(0) + col0
    idx = idx_ref[...]
    acc = jnp.full(idx.shape, NEG_INF, F32)
    for b in range(N_BUCKETS):
        acc = jnp.where(idx == b, tab_ref[b, col] * LOG2E, acc)
    o_ref[...] = acc


def _bias_tables(rel_bias, idx_np, col0, n_heads):
    nv, r, c = idx_np.shape
    return pl.pallas_call(
        functools.partial(_bias_kernel, col0=col0),
        out_shape=jax.ShapeDtypeStruct((n_heads, nv, r, c), F32),
        grid=(n_heads, nv),
        in_specs=[pl.BlockSpec(memory_space=pltpu.SMEM),
                  pl.BlockSpec((None, r, c), lambda h, v: (v, 0, 0))],
        out_specs=pl.BlockSpec((None, None, r, c), lambda h, v: (h, v, 0, 0)),
        compiler_params=_cparams(("arbitrary", "arbitrary")),
        name="bias_tables",
    )(rel_bias, jnp.asarray(idx_np))


IN_TM = 512


def _in_proj_kernel(x_ref, g1_ref, wa_ref, wb_ref, wc_ref, qg_ref, kvg_ref, wq_ref, wqr_ref, wk_ref, wv_ref,
                    place_ref, bscale_ref, cscale_ref, vones_ref, cq_ref, sq_ref, ck_ref, sk_ref,
                    qa_ref, ka_ref, va_ref, zb_ref, zc_ref):
    h = _rms(x_ref[...], g1_ref[...]).astype(BF16)
    zb_ref[...] = (_dot(h, wb_ref[...]) * bscale_ref[...]).astype(BF16)
    zc_ref[...] = (_dot(h, wc_ref[...]) * cscale_ref[...]).astype(BF16)
    za = _dot(h, wa_ref[...])
    hq = _rms(za[:, :A_Q_LORA], qg_ref[...]).astype(BF16)
    hkv = _rms(za[:, A_Q_LORA:A_Q_LORA + A_KV_LORA], kvg_ref[...]).astype(BF16)
    kpe = za[:, A_Q_LORA + A_KV_LORA:A_Q_LORA + A_KV_LORA + LANES]
    kpe_rot = za[:, A_Q_LORA + A_KV_LORA + LANES:]
    q = _dot(hq, wq_ref[...]) * (A_SCALE * LOG2E)
    q_rot = _dot(hq, wqr_ref[...]) * (A_SCALE * LOG2E)
    k_rope = (kpe * ck_ref[...] + kpe_rot * sk_ref[...]).astype(BF16)
    k = _dot(hkv, wk_ref[...]) + _dot(k_rope, place_ref[...])
    ka_ref[...] = k.astype(BF16)
    va_ref[...] = (_dot(hkv, wv_ref[...]) + vones_ref[...]).astype(BF16)
    cq = cq_ref[...]
    sq = sq_ref[...]
    for hh in range(A_HEADS):
        sl = slice(hh * LANES, (hh + 1) * LANES)
        qa_ref[:, sl] = (q[:, sl] * cq + q_rot[:, sl] * sq).astype(BF16)


def _in_proj(x2, g1, wa, wb, wc, qg, kvg, wq, wqr, wk, wv, place, bscale, cscale, vones, cq, sq, ck, sk):
    t = x2.shape[0]
    tm = IN_TM
    n_pos = SEQ // tm
    full = lambda a: pl.BlockSpec(a.shape, lambda i: (0,) * a.ndim)
    row = lambda w: pl.BlockSpec((tm, w), lambda i: (i, 0))
    pos = pl.BlockSpec((tm, LANES), lambda i: (i % n_pos, 0))
    return pl.pallas_call(
        _in_proj_kernel,
        out_shape=(jax.ShapeDtypeStruct((t, A_HEADS * LANES), BF16),
                   jax.ShapeDtypeStruct((t, A_HEADS * LANES), BF16),
                   jax.ShapeDtypeStruct((t, A_HEADS * LANES), BF16),
                   jax.ShapeDtypeStruct((t, wb.shape[1]), BF16),
                   jax.ShapeDtypeStruct((t, wc.shape[1]), BF16)),
        grid=(t // tm,),
        in_specs=[row(D_MODEL), full(g1), full(wa), full(wb), full(wc), full(qg), full(kvg), full(wq), full(wqr),
                  full(wk), full(wv), full(place), full(bscale), full(cscale), full(vones), pos, pos, pos, pos],
        out_specs=(row(A_HEADS * LANES), row(A_HEADS * LANES), row(A_HEADS * LANES), row(wb.shape[1]),
                   row(wc.shape[1])),
        compiler_params=_cparams(("parallel",)),
        name="in_proj",
    )(x2, g1, wa, wb, wc, qg, kvg, wq, wqr, wk, wv, place, bscale, cscale, vones, cq, sq, ck, sk)


A_TQ = 2048
A_SUB = 256


def _mla_kernel(q_ref, k_ref, v_ref, o_ref):
    first = lax.broadcasted_iota(jnp.int32, (1, LANES), 1) < HALF
    tasks = [(r, hh) for r in range(A_TQ // A_SUB) for hh in range(2)]
    done = {}

    def rows(r):
        return slice(r * A_SUB, (r + 1) * A_SUB)

    def lanes(hh):
        return slice(hh * LANES, (hh + 1) * LANES)

    def scores(task, _):
        r, hh = task
        return _dot_nt(q_ref[rows(r), lanes(hh)], k_ref[:, lanes(hh)])

    def values(task, s):
        p = jnp.exp2(s - jnp.max(s, axis=-1, keepdims=True))
        return _dot(p.astype(BF16), v_ref[:, lanes(task[1])])

    def store(task, a):
        r, hh = task
        done[task] = a
        if hh == 1:
            a0 = done.pop((r, 0))
            a1 = done.pop((r, 1))
            l = pltpu.roll(jnp.where(first, a1, a0), HALF, axis=1)
            o_ref[rows(r), :] = (jnp.where(first, a0, a1) / l).astype(BF16)

    _software_pipeline(tasks, (scores, values, store))


def _mla_attention(qa, ka, va, batch):
    t = qa.shape[0]
    nq = SEQ // A_TQ
    return pl.pallas_call(
        _mla_kernel,
        out_shape=jax.ShapeDtypeStruct((t, A_HEADS * A_V), BF16),
        grid=(batch, A_HEADS // 2, nq),
        in_specs=[pl.BlockSpec((A_TQ, 2 * LANES), lambda b, p, i: (b * nq + i, p)),
                  pl.BlockSpec((SEQ, 2 * LANES), lambda b, p, i: (b, p)),
                  pl.BlockSpec((SEQ, 2 * LANES), lambda b, p, i: (b, p))],
        out_specs=pl.BlockSpec((A_TQ, LANES), lambda b, p, i: (b * nq + i, p)),
        compiler_params=_cparams(("parallel", "parallel", "arbitrary")),
        name="mla_attention",
    )(qa, ka, va)


def _software_pipeline(tasks, stages):
    state = [None] * len(tasks)
    for step in range(len(tasks) + len(stages) - 1):
        for s, stage in enumerate(stages):
            i = step - s
            if 0 <= i < len(tasks):
                state[i] = stage(tasks[i], state[i])


def _first_half():
    return lax.broadcasted_iota(jnp.int32, (1, LANES), 1) < HALF


def _pair_scores(q2, k2, biases):
    first = _first_half()
    zero = jnp.zeros_like(q2)
    tq = q2.shape[0]
    stacked = jnp.concatenate([jnp.where(first, q2, zero), jnp.where(first, zero, q2)], axis=0)
    s = _dot_nt(stacked, k2)
    return [s[:tq] + biases[0], s[tq:] + biases[1]]


def _pair_values(scores, v2, sinks=None):
    first = _first_half()
    one = jnp.ones_like(v2)
    res = []
    for hh, s in enumerate(scores):
        own = first if hh == 0 else jnp.logical_not(first)
        m = jnp.max(s, axis=-1, keepdims=True)
        if sinks is not None:
            m = jnp.maximum(m, sinks[hh])
        a = _dot(jnp.exp2(s - m).astype(BF16), jnp.where(own, v2, one))
        if sinks is not None:
            a = a + jnp.where(own, 0.0, jnp.exp2(sinks[hh] - m))
        res.append((m, a))
    return res


def _pair_merge(res):
    first = _first_half()
    (m0, a0), (m1, a1) = res
    return jnp.where(first, m0, m1), jnp.where(first, a1, a0), jnp.where(first, a0, a1)


def _dilated_kernel(q1, k1, v1, bw_ref, bn_ref, o_ref, nat32, by4_32, by4, by16, stream, by4_out, merged):
    n_blocks = SEQ // QBLK
    d4, d16 = B_GROUPS[1][1], B_GROUPS[2][1]
    len4, len16 = SEQ // d4, SEQ // d16
    for t, ref in enumerate((q1, k1, v1)):
        nat32[t] = ref[...].astype(F32)
        for g in range(d4):
            run = nat32[t, pl.ds(g, len4, stride=d4), :]
            by4_32[t, g * len4:(g + 1) * len4, :] = run
            by4[t, g] = run.astype(BF16)
        for g in range(d16):
            start = (g % d4) * len4 + g // d4
            by16[t, g] = by4_32[t, pl.ds(start, len16, stride=d4), :].astype(BF16)
    sources = ((q1, k1, v1), tuple(by4.at[t] for t in range(3)), tuple(by16.at[t] for t in range(3)))

    def rows(j):
        return slice(j * QBLK, (j + 1) * QBLK)

    def view(ref, gi, sl):
        dil = B_GROUPS[gi][1]
        if dil == 1:
            return ref[sl, :]
        length = SEQ // dil
        g = sl.start // length
        return ref[g, sl.start - g * length:sl.stop - g * length, :]

    def keys(gi, j):
        length = SEQ // B_GROUPS[gi][1]
        per_stream = length // QBLK
        if per_stream == 1:
            return rows(j)
        i = j % per_stream
        start = (j - i) * QBLK + min(max(i * QBLK - B_RADIUS, 0), length - 2 * QBLK)
        return slice(start, start + 2 * QBLK)

    def scores(task, _):
        gi, j = task
        per_stream = SEQ // B_GROUPS[gi][1] // QBLK
        if per_stream == 1:
            biases = (bn_ref[0, 0], bn_ref[1, 0])
        else:
            i = j % per_stream
            var = gi * 3 + (0 if i == 0 else 2 if i == per_stream - 1 else 1)
            biases = (bw_ref[0, var], bw_ref[1, var])
        q_ref, k_ref, _ = sources[gi]
        return _pair_scores(view(q_ref, gi, rows(j)), view(k_ref, gi, keys(gi, j)), biases)

    def values(task, s):
        gi, j = task
        return _pair_values(s, view(sources[gi][2], gi, keys(gi, j)))

    def store(task, res):
        gi, j = task
        for slot, value in enumerate(_pair_merge(res)):
            if gi == 0:
                merged[0, slot, rows(j), :] = value
            else:
                stream[gi - 1, slot, rows(j), :] = value

    _software_pipeline([(gi, j) for gi in range(len(B_GROUPS)) for j in range(n_blocks)], (scores, values, store))
    for slot in range(3):
        for g in range(d16):
            start = (g % d4) * len4 + g // d4
            by4_out[slot, pl.ds(start, len16, stride=d4), :] = stream[1, slot, g * len16:(g + 1) * len16, :]
        for g in range(d4):
            run = slice(g * len4, (g + 1) * len4)
            merged[1, slot, pl.ds(g, len4, stride=d4), :] = stream[0, slot, run, :]
            merged[2, slot, pl.ds(g, len4, stride=d4), :] = by4_out[slot, run, :]

    def merge(c, carry):
        rows = pl.ds(pl.multiple_of(c * QBLK, QBLK), QBLK)
        m = [merged[gi, 0, rows, :] for gi in range(3)]
        top = jnp.maximum(jnp.maximum(m[0], m[1]), m[2])
        w = [jnp.exp2(mg - top) for mg in m]
        l = [pltpu.roll(merged[gi, 1, rows, :], HALF, axis=1) for gi in range(3)]
        num = w[0] * merged[0, 2, rows, :] + w[1] * merged[1, 2, rows, :] + w[2] * merged[2, 2, rows, :]
        den = w[0] * l[0] + w[1] * l[1] + w[2] * l[2]
        o_ref[rows, :] = (num / den).astype(BF16)
        return carry

    lax.fori_loop(0, n_blocks, merge, 0)


def _dilated_attention(zb, bias_wide, bias_narrow, batch):
    t = zb.shape[0]
    n_pairs = B_HEADS // 2
    nw = bias_wide.shape[1]
    d4, d16 = B_GROUPS[1][1], B_GROUPS[2][1]
    seq_block = lambda off: pl.BlockSpec((SEQ, LANES), lambda p, b: (b, off + p))
    return pl.pallas_call(
        _dilated_kernel,
        out_shape=jax.ShapeDtypeStruct((t, B_HEADS * B_HEAD_DIM), BF16),
        grid=(n_pairs, batch),
        in_specs=[seq_block(0), seq_block(n_pairs), seq_block(2 * n_pairs),
                  pl.BlockSpec((2, nw, QBLK, 2 * QBLK), lambda p, b: (p, 0, 0, 0)),
                  pl.BlockSpec((2, 1, QBLK, QBLK), lambda p, b: (p, 0, 0, 0))],
        out_specs=pl.BlockSpec((SEQ, LANES), lambda p, b: (b, p)),
        scratch_shapes=[pltpu.VMEM((3, SEQ, LANES), F32), pltpu.VMEM((3, SEQ, LANES), F32),
                        pltpu.VMEM((3, d4, SEQ // d4, LANES), BF16), pltpu.VMEM((3, d16, SEQ // d16, LANES), BF16),
                        pltpu.VMEM((2, 3, SEQ, LANES), F32), pltpu.VMEM((3, SEQ, LANES), F32),
                        pltpu.VMEM((3, 3, SEQ, LANES), F32)],
        compiler_params=_cparams(("arbitrary", "arbitrary")),
        name="dilated_attention",
    )(zb, zb, zb, bias_wide, bias_narrow)


def _window_kernel(sink_ref, q_ref, k_ref, v_ref, bias_ref, o_ref):
    pair = pl.program_id(0)
    sinks = (sink_ref[pair] * LOG2E, sink_ref[pair + C_HEADS // 2] * LOG2E)
    n_blocks = SEQ // QBLK

    def keys(i):
        start = min(max((i - 1) * QBLK, 0), SEQ - 3 * QBLK)
        return slice(start, start + 3 * QBLK)

    def scores(i, _):
        var = 0 if i == 0 else 2 if i == n_blocks - 1 else 1
        return _pair_scores(q_ref[i * QBLK:(i + 1) * QBLK, :], k_ref[keys(i), :],
                            (bias_ref[0, var], bias_ref[1, var]))

    def values(i, s):
        return _pair_values(s, v_ref[keys(i), :], sinks)

    def store(i, res):
        _, l_swapped, o = _pair_merge(res)
        o_ref[i * QBLK:(i + 1) * QBLK, :] = (o / pltpu.roll(l_swapped, HALF, axis=1)).astype(BF16)

    _software_pipeline(list(range(n_blocks)), (scores, values, store))


def _window_attention(zc, sink, bias, batch):
    t = zc.shape[0]
    n_pairs = C_HEADS // 2
    return pl.pallas_call(
        _window_kernel,
        out_shape=jax.ShapeDtypeStruct((t, C_HEADS * C_HEAD_DIM), BF16),
        grid=(n_pairs, batch),
        in_specs=[pl.BlockSpec(memory_space=pltpu.SMEM),
                  pl.BlockSpec((SEQ, LANES), lambda p, b: (b, p)),
                  pl.BlockSpec((SEQ, LANES), lambda p, b: (b, n_pairs)),
                  pl.BlockSpec((SEQ, LANES), lambda p, b: (b, n_pairs + 1)),
                  pl.BlockSpec((None, 2, 3, QBLK, 3 * QBLK), lambda p, b: (p, 0, 0, 0, 0))],
        out_specs=pl.BlockSpec((SEQ, LANES), lambda p, b: (b, p)),
        compiler_params=_cparams(("arbitrary", "arbitrary")),
        name="window_attention",
    )(sink, zc, zc, zc, bias)


MERGE_TM = 512


def _merge_kernel(x_ref, g1_ref, ya_ref, yb_ref, yc_ref, wg_ref, wbr_ref, wo_ref, o_ref):
    x = x_ref[...]
    h = _rms(x, g1_ref[...]).astype(BF16)
    merged = None
    for b, y_ref in enumerate((ya_ref, yb_ref, yc_ref)):
        gate = jax.nn.sigmoid(_dot(h, wg_ref[:, b * D_MODEL:(b + 1) * D_MODEL]))
        term = gate * _dot(y_ref[...], wbr_ref[b])
        merged = term if merged is None else merged + term
    o_ref[...] = x + _dot(merged.astype(BF16), wo_ref[...])


def _merge(x2, g1, ya, yb, yc, wg, wbr, wo):
    t = x2.shape[0]
    tm = MERGE_TM
    full = lambda a: pl.BlockSpec(a.shape, lambda i: (0,) * a.ndim)
    row = lambda w: pl.BlockSpec((tm, w), lambda i: (i, 0))
    return pl.pallas_call(
        _merge_kernel,
        out_shape=jax.ShapeDtypeStruct((t, D_MODEL), F32),
        grid=(t // tm,),
        in_specs=[row(D_MODEL), full(g1), row(ya.shape[1]), row(yb.shape[1]), row(yc.shape[1]),
                  full(wg), full(wbr), full(wo)],
        out_specs=row(D_MODEL),
        compiler_params=_cparams(("parallel",)),
        name="gate_merge",
    )(x2, g1, ya, yb, yc, wg, wbr, wo)


TOP_K = 2
ROW_SUBLANES = D_MODEL // LANES
MOE_TM = 512
EXPERT_TF = D_FF_EXPERT


def _router_kernel(x_ref, g_ref, wr_ref, eid_ref, rank_ref, gate_ref, cnt_ref, carry):
    @pl.when(pl.program_id(0) == 0)
    def _():
        carry[...] = jnp.zeros_like(carry)

    h = _rms(x_ref[...], g_ref[...])
    w = wr_ref[...]
    h_hi = h.astype(BF16)
    w_hi = w.astype(BF16)
    h_lo = (h - h_hi.astype(F32)).astype(BF16)
    w_lo = (w - w_hi.astype(F32)).astype(BF16)
    logits = _dot(h_hi, w_hi) + (_dot(h_lo, w_hi) + _dot(h_hi, w_lo))
    idx = lax.broadcasted_iota(jnp.int32, logits.shape, 1)
    m1 = jnp.max(logits, axis=-1, keepdims=True)
    i1 = jnp.min(jnp.where(logits == m1, idx, N_EXPERTS), axis=-1, keepdims=True)
    first = idx == i1
    rest = jnp.where(first, -jnp.inf, logits)
    m2 = jnp.max(rest, axis=-1, keepdims=True)
    i2 = jnp.min(jnp.where(rest == m2, idx, N_EXPERTS), axis=-1, keepdims=True)
    second = idx == i2
    e = jnp.exp(m2 - m1)
    den = 1.0 + e
    chosen = jnp.where(jnp.logical_or(first, second), 1.0, 0.0)
    tm = chosen.shape[0]
    earlier = (lax.broadcasted_iota(jnp.int32, (tm, tm), 0) > lax.broadcasted_iota(jnp.int32, (tm, tm), 1))
    before = _dot(jnp.where(earlier, 1.0, 0.0).astype(BF16), chosen.astype(BF16)) + carry[...]
    r1 = jnp.sum(jnp.where(first, before, 0.0), axis=-1, keepdims=True)
    r2 = jnp.sum(jnp.where(second, before, 0.0), axis=-1, keepdims=True)
    carry[...] += jnp.sum(chosen, axis=0, keepdims=True)
    slot0 = lax.broadcasted_iota(jnp.int32, (tm, TOP_K), 1) == 0
    eid_ref[...] = jnp.where(slot0, i1, i2)
    rank_ref[...] = jnp.where(slot0, r1, r2).astype(jnp.int32)
    gate_ref[...] = jnp.where(slot0, 1.0 / den, e / den)
    cnt_ref[...] = carry[...].astype(jnp.int32)


def _router(x2, g, wr):
    t = x2.shape[0]
    tm = MOE_TM
    pair = pl.BlockSpec((tm, TOP_K), lambda i: (i, 0))
    return pl.pallas_call(
        _router_kernel,
        out_shape=(jax.ShapeDtypeStruct((t, TOP_K), jnp.int32), jax.ShapeDtypeStruct((t, TOP_K), jnp.int32),
                   jax.ShapeDtypeStruct((t, TOP_K), F32), jax.ShapeDtypeStruct((1, N_EXPERTS), jnp.int32)),
        grid=(t // tm,),
        in_specs=[pl.BlockSpec((tm, D_MODEL), lambda i: (i, 0)),
                  pl.BlockSpec(g.shape, lambda i: (0, 0)),
                  pl.BlockSpec(wr.shape, lambda i: (0, 0))],
        out_specs=(pair, pair, pair, pl.BlockSpec((1, N_EXPERTS), lambda i: (0, 0))),
        scratch_shapes=[pltpu.VMEM((1, N_EXPERTS), F32)],
        compiler_params=_cparams(("arbitrary",)),
        name="router",
    )(x2, g, wr)


def _row_tile(ref, row):
    return ref.at[pl.ds(pl.multiple_of(row * ROW_SUBLANES, ROW_SUBLANES), ROW_SUBLANES), :]


def _dispatch_kernel(tail_start_ref, tail_len_ref, na_ref, dest_ref, x_ref, g_ref, hs_ref, stage, sems):
    i = pl.program_id(0)
    n_steps = pl.num_programs(0)
    tm = x_ref.shape[0]
    slot = i % 2
    rows_per_slot = stage.shape[1]

    def drain(s):
        for _ in range(TOP_K):
            pltpu.make_async_copy(stage.at[s], hs_ref.at[pl.ds(0, rows_per_slot), :], sems.at[s]).wait()

    @pl.when(i >= 2)
    def _():
        drain(slot)

    h = _rms(x_ref[...], g_ref[...])
    for s in range(ROW_SUBLANES):
        stage[slot, pl.ds(s, tm, stride=ROW_SUBLANES), :] = h[:, s * LANES:(s + 1) * LANES]

    def start(t, carry):
        for k in range(TOP_K):
            pltpu.make_async_copy(_row_tile(stage.at[slot], t), _row_tile(hs_ref, dest_ref[TOP_K * t + k]),
                                  sems.at[slot]).start(priority=k)
        return carry

    lax.fori_loop(0, tm, start, 0, unroll=8)

    @pl.when(i == n_steps - 1)
    def _():
        @pl.when(n_steps >= 2)
        def _():
            drain(1 - slot)

        drain(slot)
        zeros = stage.at[0]
        zeros[...] = jnp.zeros_like(zeros)
        for e in range(N_EXPERTS):
            def tail(r, e=e):
                return pltpu.make_async_copy(_row_tile(zeros, 0), _row_tile(hs_ref, tail_start_ref[e] + r), sems.at[0])

            lax.fori_loop(0, tail_len_ref[e], lambda r, c: (tail(r).start(), c)[1], 0)
            lax.fori_loop(0, tail_len_ref[e], lambda r, c: (tail(r).wait(), c)[1], 0)
        n_tiles = hs_ref.shape[0] // rows_per_slot

        def spare(j):
            rows = pl.ds(pl.multiple_of(j * rows_per_slot, rows_per_slot), rows_per_slot)
            return pltpu.make_async_copy(zeros, hs_ref.at[rows, :], sems.at[0])

        lax.fori_loop(na_ref[0], n_tiles, lambda j, c: (spare(j).start(), c)[1], 0)
        lax.fori_loop(na_ref[0], n_tiles, lambda j, c: (spare(j).wait(), c)[1], 0)


def _dispatch(x2, g, dest, tail_start, tail_len, n_active, n_rows):
    t = x2.shape[0]
    tm = MOE_TM
    return pl.pallas_call(
        _dispatch_kernel,
        out_shape=jax.ShapeDtypeStruct((n_rows * ROW_SUBLANES, LANES), F32),
        grid_spec=pltpu.PrefetchScalarGridSpec(
            num_scalar_prefetch=3,
            grid=(t // tm,),
            in_specs=[pl.BlockSpec((TOP_K * tm,), lambda i, ts, tl, na: (i,), memory_space=pltpu.SMEM),
                      pl.BlockSpec((tm, D_MODEL), lambda i, ts, tl, na: (i, 0)),
                      pl.BlockSpec(g.shape, lambda i, ts, tl, na: (0, 0))],
            out_specs=pl.BlockSpec(memory_space=pl.ANY),
            scratch_shapes=[pltpu.VMEM((2, tm * ROW_SUBLANES, LANES), F32), pltpu.SemaphoreType.DMA((2,))]),
        compiler_params=_cparams(("arbitrary",)),
        name="dispatch",
    )(tail_start, tail_len, n_active, dest, x2, g)


def _experts_kernel(te_ref, na_ref, hs_ref, wg_ref, wu_ref, wd_ref, ys_ref, h_scr, acc_scr):
    i = pl.program_id(0)
    f = pl.program_id(1)
    active = i < na_ref[0]
    tm = h_scr.shape[0]

    @pl.when(jnp.logical_and(active, f == 0))
    def _():
        for s in range(ROW_SUBLANES):
            h_scr[:, s * LANES:(s + 1) * LANES] = hs_ref[pl.ds(s, tm, stride=ROW_SUBLANES), :].astype(BF16)

    @pl.when(active)
    def _():
        h = h_scr[...]
        act = jax.nn.silu(_dot(h, wg_ref[...])) * _dot(h, wu_ref[...])
        part = _dot(act.astype(BF16), wd_ref[...])

        @pl.when(f == 0)
        def _():
            acc_scr[...] = part

        @pl.when(f != 0)
        def _():
            acc_scr[...] += part

    @pl.when(jnp.logical_and(active, f == pl.num_programs(1) - 1))
    def _():
        for s in range(ROW_SUBLANES):
            ys_ref[pl.ds(s, tm, stride=ROW_SUBLANES), :] = acc_scr[:, s * LANES:(s + 1) * LANES]

    @pl.when(jnp.logical_and(jnp.logical_not(active), f == pl.num_programs(1) - 1))
    def _():
        ys_ref[...] = jnp.zeros_like(ys_ref)


def _experts(hs, tile_expert, n_active, wg, wu, wd):
    tm = MOE_TM
    tf = EXPERT_TF
    n_tiles = hs.shape[0] // (tm * ROW_SUBLANES)
    n_f = wg.shape[2] // tf
    rows = lambda i, f, te, na: (jnp.minimum(i, na[0] - 1), 0)
    col = lambda i, f, na: jnp.where(i < na[0], f, n_f - 1)
    once = pl.Buffered(1)
    return pl.pallas_call(
        _experts_kernel,
        out_shape=jax.ShapeDtypeStruct(hs.shape, F32),
        grid_spec=pltpu.PrefetchScalarGridSpec(
            num_scalar_prefetch=2,
            grid=(n_tiles, n_f),
            in_specs=[pl.BlockSpec((tm * ROW_SUBLANES, LANES), rows),
                      pl.BlockSpec((None, D_MODEL, tf), lambda i, f, te, na: (te[i], 0, col(i, f, na)),
                                   pipeline_mode=once),
                      pl.BlockSpec((None, D_MODEL, tf), lambda i, f, te, na: (te[i], 0, col(i, f, na)),
                                   pipeline_mode=once),
                      pl.BlockSpec((None, tf, D_MODEL), lambda i, f, te, na: (te[i], col(i, f, na), 0),
                                   pipeline_mode=once)],
            out_specs=pl.BlockSpec((tm * ROW_SUBLANES, LANES), lambda i, f, te, na: (i, 0)),
            scratch_shapes=[pltpu.VMEM((tm, D_MODEL), BF16), pltpu.VMEM((tm, D_MODEL), F32)]),
        compiler_params=_cparams(("arbitrary", "arbitrary")),
        name="experts",
    )(tile_expert, n_active, hs, wg, wu, wd)


def _combine_kernel(dest_ref, next_dest_ref, x_ref, gate_ref, gf_ref, ys_ref, o_ref, bufs, sems, *, final):
    i = pl.program_id(0)
    tm = x_ref.shape[0]
    slot = i % 2

    def gather(idx_ref, s):
        def start(t, carry):
            for k in range(TOP_K):
                pltpu.make_async_copy(_row_tile(ys_ref, idx_ref[TOP_K * t + k]), _row_tile(bufs.at[s, k], t),
                                      sems.at[s]).start(priority=k)
            return carry

        lax.fori_loop(0, tm, start, 0, unroll=8)

    @pl.when(i == 0)
    def _():
        gather(dest_ref, slot)

    @pl.when(i + 1 < pl.num_programs(0))
    def _():
        gather(next_dest_ref, 1 - slot)

    for k in range(TOP_K):
        pltpu.make_async_copy(ys_ref.at[pl.ds(0, bufs.shape[2]), :], bufs.at[slot, k], sems.at[slot]).wait()
    gate = gate_ref[...]
    g0 = gate[:, 0:1]
    g1 = gate[:, 1:2]
    ss = jnp.zeros((tm, 1), F32)
    for s in range(ROW_SUBLANES):
        sl = slice(s * LANES, (s + 1) * LANES)
        rows = pl.ds(s, tm, stride=ROW_SUBLANES)
        y = x_ref[:, sl] + (g0 * bufs[slot, 0, rows, :] + g1 * bufs[slot, 1, rows, :])
        o_ref[:, sl] = y
        ss = ss + jnp.sum(y * y, axis=-1, keepdims=True)
    if final:
        o_ref[...] = o_ref[...] * lax.rsqrt(ss / D_MODEL + EPS) * gf_ref[...]


def _combine(x2, dest, gates, gf, ys, *, final):
    t = x2.shape[0]
    tm = MOE_TM
    return pl.pallas_call(
        functools.partial(_combine_kernel, final=final),
        out_shape=jax.ShapeDtypeStruct((t, D_MODEL), F32),
        grid=(t // tm,),
        in_specs=[pl.BlockSpec((TOP_K * tm,), lambda i: (i,), memory_space=pltpu.SMEM),
                  pl.BlockSpec((TOP_K * tm,), lambda i: (jnp.minimum(i + 1, t // tm - 1),),
                               memory_space=pltpu.SMEM),
                  pl.BlockSpec((tm, D_MODEL), lambda i: (i, 0)),
                  pl.BlockSpec((tm, TOP_K), lambda i: (i, 0)),
                  pl.BlockSpec(gf.shape, lambda i: (0, 0)),
                  pl.BlockSpec(memory_space=pl.ANY)],
        out_specs=pl.BlockSpec((tm, D_MODEL), lambda i: (i, 0)),
        scratch_shapes=[pltpu.VMEM((2, TOP_K, tm * ROW_SUBLANES, LANES), F32), pltpu.SemaphoreType.DMA((2,))],
        compiler_params=_cparams(("arbitrary",)),
        name="combine",
    )(dest, dest, x2, gates, gf, ys)


def _moe(x2, g, wr, wg, wu, wd, gf, *, final):
    t = x2.shape[0]
    tm = MOE_TM
    eid, rank, gates, counts = _router(x2, g, wr)
    counts = counts[0]
    padded = (counts + tm - 1) // tm * tm
    group_end = jnp.cumsum(padded)
    group_start = group_end - padded
    dest = (group_start[eid] + rank).reshape(-1)
    n_tiles = TOP_K * t // tm + N_EXPERTS
    n_active = (group_end[-1:] // tm).astype(jnp.int32)
    tile_row = jnp.arange(n_tiles, dtype=jnp.int32) * tm
    tile_row = jnp.minimum(tile_row, group_end[-1] - tm)
    tile_expert = jnp.sum(tile_row[:, None] >= group_end[None, :], axis=1).astype(jnp.int32)
    hs = _dispatch(x2, g, dest, group_start + counts, padded - counts, n_active, n_tiles * tm)
    ys = _experts(hs, tile_expert, n_active, wg, wu, wd)
    return _combine(x2, dest, gates, gf, ys, final=final)


FFN_TM = 512
FFN_TF = D_FF_DENSE


def _ffn_kernel(x_ref, g_ref, wg_ref, wu_ref, wd_ref, gf_ref, o_ref, h_scr, acc_scr, *, final):
    f = pl.program_id(1)

    @pl.when(f == 0)
    def _():
        h_scr[...] = _rms(x_ref[...], g_ref[...]).astype(BF16)

    h = h_scr[...]
    act = jax.nn.silu(_dot(h, wg_ref[...])) * _dot(h, wu_ref[...])
    part = _dot(act.astype(BF16), wd_ref[...])

    @pl.when(f == 0)
    def _():
        acc_scr[...] = part

    @pl.when(f != 0)
    def _():
        acc_scr[...] += part

    @pl.when(f == pl.num_programs(1) - 1)
    def _():
        y = x_ref[...] + acc_scr[...]
        if final:
            y = _rms(y, gf_ref[...])
        o_ref[...] = y


def _ffn(x2, g, wg, wu, wd, gf, *, final):
    t = x2.shape[0]
    tm = FFN_TM
    tf = FFN_TF
    d_ff = wg.shape[1]
    return pl.pallas_call(
        functools.partial(_ffn_kernel, final=final),
        out_shape=jax.ShapeDtypeStruct((t, D_MODEL), F32),
        grid=(t // tm, d_ff // tf),
        in_specs=[pl.BlockSpec((tm, D_MODEL), lambda i, f: (i, 0)),
                  pl.BlockSpec(g.shape, lambda i, f: (0, 0)),
                  pl.BlockSpec((D_MODEL, tf), lambda i, f: (0, f)),
                  pl.BlockSpec((D_MODEL, tf), lambda i, f: (0, f)),
                  pl.BlockSpec((tf, D_MODEL), lambda i, f: (f, 0)),
                  pl.BlockSpec(gf.shape, lambda i, f: (0, 0))],
        out_specs=pl.BlockSpec((tm, D_MODEL), lambda i, f: (i, 0)),
        scratch_shapes=[pltpu.VMEM((tm, D_MODEL), BF16), pltpu.VMEM((tm, D_MODEL), F32)],
        compiler_params=_cparams(("parallel", "arbitrary")),
        name="swiglu",
    )(x2, g, wg, wu, wd, gf)


def _rot_cols(w):
    half = A_ROPE // 2
    return jnp.concatenate([-w[..., half:], w[..., :half]], axis=-1)


def _prep_layer(w_in, w_uq, w_ukv, w_br_a, w_br_b, w_br_c):
    o = 0
    w_cq = w_in[:, o:o + A_Q_LORA]; o += A_Q_LORA
    w_ckv = w_in[:, o:o + A_KV_LORA]; o += A_KV_LORA
    w_kpe = w_in[:, o:o + A_ROPE]; o += A_ROPE
    nb = B_HEADS * B_HEAD_DIM
    w_b = w_in[:, o:o + 3 * nb]; o += 3 * nb
    nq = C_HEADS * C_HEAD_DIM
    nkv = C_KV_HEADS * C_HEAD_DIM
    w_c = w_in[:, o:o + nq + 2 * nkv]; o += nq + 2 * nkv
    w_g = w_in[:, o:]
    pad = jnp.zeros((D_MODEL, LANES - A_ROPE), F32)
    wa = jnp.concatenate([w_cq, w_ckv, w_kpe, pad, _rot_cols(w_kpe), pad], axis=1)
    wb = jnp.concatenate([w_b[:, :nb] * (B_HEAD_DIM ** -0.5), w_b[:, nb:]], axis=1)
    q_c = w_c[:, :nq].reshape(D_MODEL, 2, C_HEADS // 2, C_HEAD_DIM).transpose(0, 2, 1, 3).reshape(D_MODEL, nq)
    wc = jnp.concatenate([q_c * (C_HEAD_DIM ** -0.5), w_c[:, nq:]], axis=1)
    uq = w_uq.reshape(A_Q_LORA, A_HEADS, A_NOPE + A_ROPE)
    z32 = jnp.zeros((A_Q_LORA, A_HEADS, LANES - A_NOPE - A_ROPE), F32)
    wq = jnp.concatenate([uq, z32], axis=-1).reshape(A_Q_LORA, A_HEADS * LANES)
    wqr = jnp.concatenate([jnp.zeros((A_Q_LORA, A_HEADS, A_NOPE), F32), _rot_cols(uq[..., A_NOPE:]), z32],
                          axis=-1).reshape(A_Q_LORA, A_HEADS * LANES)
    ukv = w_ukv.reshape(A_KV_LORA, A_HEADS, A_NOPE + A_V)
    wk = jnp.concatenate([ukv[..., :A_NOPE], jnp.zeros((A_KV_LORA, A_HEADS, LANES - A_NOPE), F32)],
                         axis=-1).reshape(A_KV_LORA, A_HEADS * LANES)
    uv = ukv[..., A_NOPE:].reshape(A_KV_LORA, A_HEADS // 2, 2, A_V)
    zv = jnp.zeros_like(uv[:, :, 0])
    wv = jnp.stack([jnp.concatenate([uv[:, :, 0], zv], axis=-1), jnp.concatenate([zv, uv[:, :, 1]], axis=-1)],
                   axis=2).reshape(A_KV_LORA, A_HEADS * LANES)
    w_br_c = w_br_c.reshape(2, C_HEADS // 2, C_HEAD_DIM, D_MODEL).transpose(1, 0, 2, 3).reshape(nq, D_MODEL)
    wbr = jnp.stack([w_br_a, w_br_b, w_br_c])
    cast = lambda a: a.astype(BF16)
    return dict(wa=cast(wa), wb=cast(wb), wc=cast(wc), wg=cast(w_g), wq=cast(wq), wqr=cast(wqr), wk=cast(wk),
                wv=cast(wv), wbr=cast(wbr))


def _rope_tables():
    half = A_ROPE // 2
    inv = ROPE_THETA ** (-jnp.arange(half, dtype=F32) / half)
    ang = jnp.arange(SEQ, dtype=F32)[:, None] * inv[None, :]
    cos2 = jnp.tile(jnp.cos(ang), (1, 2))
    sin2 = jnp.tile(jnp.sin(ang), (1, 2))
    ones = jnp.ones((SEQ, A_NOPE), F32)
    z = lambda w: jnp.zeros((SEQ, w), F32)
    cq = jnp.concatenate([ones, cos2, z(LANES - A_NOPE - A_ROPE)], axis=1)
    sq = jnp.concatenate([z(A_NOPE), sin2, z(LANES - A_NOPE - A_ROPE)], axis=1)
    ck = jnp.concatenate([cos2, z(LANES - A_ROPE)], axis=1)
    sk = jnp.concatenate([sin2, z(LANES - A_ROPE)], axis=1)
    return cq, sq, ck, sk


def _rope_placement():
    place = np.zeros((LANES, A_HEADS * LANES), np.float32)
    for h in range(A_HEADS):
        for j in range(A_ROPE):
            place[j, h * LANES + A_NOPE + j] = 1.0
    return jnp.asarray(place, dtype=BF16)


def kernel(x, norm1_g, w_in, q_norm_g, w_uq, kv_norm_g, w_ukv, sink_logit, rel_bias, w_branch_a, w_branch_b,
           w_branch_c, w_out, norm2_g, ffn_w_gate, ffn_w_up, ffn_w_down, router_w, exp_w_gate, exp_w_up,
           exp_w_down, final_g):
    batch, seq, d = x.shape
    assert seq == SEQ and d == D_MODEL
    t = batch * seq
    x2 = x.reshape(t, d)

    wide_idx, narrow_idx = _dilated_index_tables()
    bias_wide = _bias_tables(rel_bias, wide_idx, 0, B_HEADS)
    bias_narrow = _bias_tables(rel_bias, narrow_idx, 0, B_HEADS)
    bias_c = _bias_tables(rel_bias, _window_index_tables(), B_HEADS, C_HEADS)
    bias_c = bias_c.reshape(2, C_HEADS // 2, *bias_c.shape[1:]).transpose(1, 0, 2, 3, 4)

    cq, sq, ck, sk = _rope_tables()
    place = _rope_placement()
    row = lambda v: v.reshape(1, -1)
    q_then = lambda nq, rest: jnp.asarray(np.concatenate([np.full(nq, LOG2E), np.ones(rest)])[None], F32)
    bscale = q_then(B_HEADS * B_HEAD_DIM, 2 * B_HEADS * B_HEAD_DIM)
    cscale = q_then(C_HEADS * C_HEAD_DIM, 2 * C_KV_HEADS * C_HEAD_DIM)
    vones = jnp.asarray(np.tile(np.repeat([0.0, 1.0, 1.0, 0.0], A_V), A_HEADS // 2)[None], F32)

    for l in range(DEPTH):
        w = _prep_layer(w_in[l], w_uq[l], w_ukv[l], w_branch_a[l], w_branch_b[l], w_branch_c[l])
        g1 = row(norm1_g[l])
        qa, ka, va, zb, zc = _in_proj(x2, g1, w["wa"], w["wb"], w["wc"], row(q_norm_g[l]), row(kv_norm_g[l]),
                                      w["wq"], w["wqr"], w["wk"], w["wv"], place, bscale, cscale, vones,
                                      cq, sq, ck, sk)
        ya = _mla_attention(qa, ka, va, batch)
        yb = _dilated_attention(zb, bias_wide, bias_narrow, batch)
        yc = _window_attention(zc, sink_logit[l], bias_c, batch)
        x2 = _merge(x2, g1, ya, yb, yc, w["wg"], w["wbr"], w_out[l].astype(BF16))
        g2 = row(norm2_g[l])
        final = l == DEPTH - 1
        i = l // 2
        if l % 2 == 0:
            x2 = _ffn(x2, g2, ffn_w_gate[i].astype(BF16), ffn_w_up[i].astype(BF16), ffn_w_down[i].astype(BF16),
                      row(final_g), final=final)
        else:
            x2 = _moe(x2, g2, router_w[i], exp_w_gate[i].astype(BF16), exp_w_up[i].astype(BF16),
                      exp_w_down[i].astype(BF16), row(final_g), final=final)
    return x2.reshape(batch, seq, d)
```

```python
import functools
import math

import numpy as np
import jax
import jax.numpy as jnp
from jax import lax
from jax.experimental import pallas as pl
from jax.experimental.pallas import tpu as pltpu

F32 = jnp.float32
BF16 = jnp.bfloat16

D_MODEL = 1024
SEQ = 2048
DEPTH = 2
A_HEADS = 8
A_NOPE = 64
A_ROPE = 32
A_V = 64
A_Q_LORA = 256
A_KV_LORA = 128
ROPE_THETA = 10000.0
B_HEADS = 8
B_HEAD_DIM = 64
B_GROUPS = ((128, 1), (512, 4), (2048, 16))
C_HEADS = 8
C_KV_HEADS = 2
C_HEAD_DIM = 64
C_RADIUS = 128
N_BUCKETS = 32
MAX_DISTANCE = 1024
D_FF_DENSE = 2816
N_EXPERTS = 8
D_FF_EXPERT = 3584
EPS = 1e-6
NEG_INF = -1e30

LANES = 128
HALF = 64
QBLK = 128
B_RADIUS = 64
VMEM_LIMIT = 56 * 1024 * 1024

A_SCALE = (A_NOPE + A_ROPE) ** -0.5
LOG2E = math.log2(math.e)


def _cparams(sem):
    return pltpu.CompilerParams(dimension_semantics=sem, vmem_limit_bytes=VMEM_LIMIT)


def _rms(x, g):
    return x * lax.rsqrt(jnp.mean(x * x, axis=-1, keepdims=True) + EPS) * g


def _dot(a, b):
    return jnp.dot(a, b, preferred_element_type=F32)


def _dot_nt(a, b):
    return lax.dot_general(a, b, (((1,), (1,)), ((), ())), preferred_element_type=F32)


def _t5_bucket_np(rel):
    nb = N_BUCKETS // 2
    max_exact = nb // 2
    ret = np.where(rel > 0, nb, 0)
    n = np.abs(rel)
    nf = np.maximum(n, 1).astype(np.float64)
    t = np.log(nf / max_exact) / math.log(MAX_DISTANCE / max_exact) * (nb - max_exact)
    frac = np.abs(t - np.round(t))
    assert np.all((frac > 1e-5) | (n <= max_exact) | (n >= MAX_DISTANCE))
    large = max_exact + np.floor(np.round(t, 9)).astype(np.int64)
    large = np.minimum(large, nb - 1)
    return (ret + np.where(n < max_exact, n, large)).astype(np.int32)


def _band_index(tq, tk, shift, radius, dil):
    rel = np.arange(tk)[None, :] - np.arange(tq)[:, None] + shift
    idx = _t5_bucket_np(rel * dil)
    return np.where(np.abs(rel) <= radius, idx, -1).astype(np.int32)


def _dilated_index_tables():
    wide = []
    for window, dil in B_GROUPS[:2]:
        for shift in (0, -B_RADIUS, -2 * B_RADIUS):
            wide.append(_band_index(QBLK, 2 * QBLK, shift, B_RADIUS, dil))
    narrow = [_band_index(QBLK, QBLK, 0, B_RADIUS, B_GROUPS[2][1])]
    return np.stack(wide), np.stack(narrow)


def _window_index_tables():
    return np.stack([_band_index(QBLK, 3 * QBLK, shift, C_RADIUS, 1) for shift in (0, -QBLK, -2 * QBLK)])


def _bias_kernel(tab_ref, idx_ref, o_ref, *, col0):
    col = pl.program_id(0) + col0
    idx = idx_ref[...]
    acc = jnp.full(idx.shape, NEG_INF, F32)
    for b in range(N_BUCKETS):
        acc = jnp.where(idx == b, tab_ref[b, col] * LOG2E, acc)
    o_ref[...] = acc


def _bias_tables(rel_bias, idx_np, col0, n_heads):
    nv, r, c = idx_np.shape
    return pl.pallas_call(
        functools.partial(_bias_kernel, col0=col0),
        out_shape=jax.ShapeDtypeStruct((n_heads, nv, r, c), F32),
        grid=(n_heads, nv),
        in_specs=[pl.BlockSpec(memory_space=pltpu.SMEM),
                  pl.BlockSpec((None, r, c), lambda h, v: (v, 0, 0))],
        out_specs=pl.BlockSpec((None, None, r, c), lambda h, v: (h, v, 0, 0)),
        compiler_params=_cparams(("arbitrary", "arbitrary")),
        name="bias_tables",
    )(rel_bias, jnp.asarray(idx_np))


IN_TM = 1024
IN_CHAINS = 4


def _in_proj_kernel(x_ref, g1_ref, wa_ref, wb_ref, wc_ref, qg_ref, kvg_ref, wq_ref, wqr_ref, wkv_ref,
                    bscale_ref, cscale_ref, vones_ref, cq_ref, sq_ref, ck_ref, sk_ref,
                    qa_ref, ka_ref, va_ref, zb_ref, zc_ref):
    sub = x_ref.shape[0] // IN_CHAINS

    def rows(r):
        return slice(r * sub, (r + 1) * sub)

    def project(r, _):
        h = _rms(x_ref[rows(r), :], g1_ref[...]).astype(BF16)
        za = _dot(h, wa_ref[...])
        zb_ref[rows(r), :] = (_dot(h, wb_ref[...]) * bscale_ref[...]).astype(BF16)
        zc_ref[rows(r), :] = (_dot(h, wc_ref[...]) * cscale_ref[...]).astype(BF16)
        return za

    def expand(r, za):
        hq = _rms(za[:, :A_Q_LORA], qg_ref[...]).astype(BF16)
        hkv = _rms(za[:, A_Q_LORA:A_Q_LORA + A_KV_LORA], kvg_ref[...]).astype(BF16)
        kpe = za[:, A_Q_LORA + A_KV_LORA:]
        kpe_rot = pltpu.roll(kpe, LANES - A_ROPE, axis=1)
        k_rope = (kpe * ck_ref[rows(r), :] + kpe_rot * sk_ref[rows(r), :]).astype(BF16)
        q = _dot(hq, wq_ref[...]) * (A_SCALE * LOG2E)
        q_rot = _dot(hq, wqr_ref[...]) * (A_SCALE * LOG2E)
        kv = _dot(jnp.concatenate([hkv, k_rope], axis=1), wkv_ref[...])
        return q, q_rot, kv

    def store(r, state):
        q, q_rot, kv = state
        ka_ref[rows(r), :] = kv[:, :A_HEADS * LANES].astype(BF16)
        va_ref[rows(r), :] = (kv[:, A_HEADS * LANES:] + vones_ref[...]).astype(BF16)
        cq = cq_ref[rows(r), :]
        sq = sq_ref[rows(r), :]
        for hh in range(A_HEADS):
            sl = slice(hh * LANES, (hh + 1) * LANES)
            qa_ref[rows(r), sl] = (q[:, sl] * cq + q_rot[:, sl] * sq).astype(BF16)

    _software_pipeline(list(range(IN_CHAINS)), (project, expand, store))


def _in_proj(x2, g1, wa, wb, wc, qg, kvg, wq, wqr, wkv, bscale, cscale, vones, cq, sq, ck, sk):
    t = x2.shape[0]
    tm = IN_TM
    n_pos = SEQ // tm
    full = lambda a: pl.BlockSpec(a.shape, lambda i: (0,) * a.ndim)
    row = lambda w: pl.BlockSpec((tm, w), lambda i: (i, 0))
    pos = pl.BlockSpec((tm, LANES), lambda i: (i % n_pos, 0))
    return pl.pallas_call(
        _in_proj_kernel,
        out_shape=(jax.ShapeDtypeStruct((t, A_HEADS * LANES), BF16),
                   jax.ShapeDtypeStruct((t, A_HEADS * LANES), BF16),
                   jax.ShapeDtypeStruct((t, A_HEADS * LANES), BF16),
                   jax.ShapeDtypeStruct((t, wb.shape[1]), BF16),
                   jax.ShapeDtypeStruct((t, wc.shape[1]), BF16)),
        grid=(t // tm,),
        in_specs=[row(D_MODEL), full(g1), full(wa), full(wb), full(wc), full(qg), full(kvg), full(wq), full(wqr),
                  full(wkv), full(bscale), full(cscale), full(vones), pos, pos, pos, pos],
        out_specs=(row(A_HEADS * LANES), row(A_HEADS * LANES), row(A_HEADS * LANES), row(wb.shape[1]),
                   row(wc.shape[1])),
        compiler_params=_cparams(("parallel",)),
        name="in_proj",
    )(x2, g1, wa, wb, wc, qg, kvg, wq, wqr, wkv, bscale, cscale, vones, cq, sq, ck, sk)


A_TQ = 2048
A_SUB = 256


def _mla_kernel(q_ref, k_ref, v_ref, o_ref):
    first = lax.broadcasted_iota(jnp.int32, (1, LANES), 1) < HALF
    tasks = [(r, hh) for r in range(A_TQ // A_SUB) for hh in range(2)]
    done = {}

    def rows(r):
        return slice(r * A_SUB, (r + 1) * A_SUB)

    def lanes(hh):
        return slice(hh * LANES, (hh + 1) * LANES)

    def scores(task, _):
        r, hh = task
        return _dot_nt(q_ref[rows(r), lanes(hh)], k_ref[:, lanes(hh)])

    def values(task, s):
        p = jnp.exp2(s - jnp.max(s, axis=-1, keepdims=True))
        return _dot(p.astype(BF16), v_ref[:, lanes(task[1])])

    def store(task, a):
        r, hh = task
        done[task] = a
        if hh == 1:
            a0 = done.pop((r, 0))
            a1 = done.pop((r, 1))
            l = pltpu.roll(jnp.where(first, a1, a0), HALF, axis=1)
            o_ref[rows(r), :] = (jnp.where(first, a0, a1) / l).astype(BF16)

    _software_pipeline(tasks, (scores, values, store))


def _mla_attention(qa, ka, va, batch):
    t = qa.shape[0]
    nq = SEQ // A_TQ
    return pl.pallas_call(
        _mla_kernel,
        out_shape=jax.ShapeDtypeStruct((t, A_HEADS * A_V), BF16),
        grid=(batch, A_HEADS // 2, nq),
        in_specs=[pl.BlockSpec((A_TQ, 2 * LANES), lambda b, p, i: (b * nq + i, p)),
                  pl.BlockSpec((SEQ, 2 * LANES), lambda b, p, i: (b, p)),
                  pl.BlockSpec((SEQ, 2 * LANES), lambda b, p, i: (b, p))],
        out_specs=pl.BlockSpec((A_TQ, LANES), lambda b, p, i: (b * nq + i, p)),
        compiler_params=_cparams(("parallel", "parallel", "arbitrary")),
        name="mla_attention",
    )(qa, ka, va)


def _software_pipeline(tasks, stages):
    state = [None] * len(tasks)
    for step in range(len(tasks) + len(stages) - 1):
        for s, stage in enumerate(stages):
            i = step - s
            if 0 <= i < len(tasks):
                state[i] = stage(tasks[i], state[i])


def _first_half():
    return lax.broadcasted_iota(jnp.int32, (1, LANES), 1) < HALF


def _pair_scores(q2, k2, biases):
    first = _first_half()
    zero = jnp.zeros_like(q2)
    tq = q2.shape[0]
    stacked = jnp.concatenate([jnp.where(first, q2, zero), jnp.where(first, zero, q2)], axis=0)
    s = _dot_nt(stacked, k2)
    return [s[:tq] + biases[0], s[tq:] + biases[1]]


def _pair_values(scores, v2, sinks=None):
    first = _first_half()
    one = jnp.ones_like(v2)
    res = []
    for hh, s in enumerate(scores):
        own = first if hh == 0 else jnp.logical_not(first)
        m = jnp.max(s, axis=-1, keepdims=True)
        if sinks is not None:
            m = jnp.maximum(m, sinks[hh])
        a = _dot(jnp.exp2(s - m).astype(BF16), jnp.where(own, v2, one))
        if sinks is not None:
            a = a + jnp.where(own, 0.0, jnp.exp2(sinks[hh] - m))
        res.append((m, a))
    return res


def _pair_merge(res):
    first = _first_half()
    (m0, a0), (m1, a1) = res
    return jnp.where(first, m0, m1), jnp.where(first, a1, a0), jnp.where(first, a0, a1)


def _dilated_kernel(q1, k1, v1, bw_ref, bn_ref, o_ref, nat32, by4_32, by4, by16, stream, by4_out, merged):
    n_blocks = SEQ // QBLK
    d4, d16 = B_GROUPS[1][1], B_GROUPS[2][1]
    len4, len16 = SEQ // d4, SEQ // d16
    for t, ref in enumerate((q1, k1, v1)):
        nat32[t] = ref[...].astype(F32)
        for g in range(d4):
            run = nat32[t, pl.ds(g, len4, stride=d4), :]
            by4_32[t, g * len4:(g + 1) * len4, :] = run
            by4[t, g] = run.astype(BF16)
        for g in range(d16):
            start = (g % d4) * len4 + g // d4
            by16[t, g] = by4_32[t, pl.ds(start, len16, stride=d4), :].astype(BF16)
    sources = ((q1, k1, v1), tuple(by4.at[t] for t in range(3)), tuple(by16.at[t] for t in range(3)))

    def rows(j):
        return slice(j * QBLK, (j + 1) * QBLK)

    def view(ref, gi, sl):
        dil = B_GROUPS[gi][1]
        if dil == 1:
            return ref[sl, :]
        length = SEQ // dil
        g = sl.start // length
        return ref[g, sl.start - g * length:sl.stop - g * length, :]

    def keys(gi, j):
        length = SEQ // B_GROUPS[gi][1]
        per_stream = length // QBLK
        if per_stream == 1:
            return rows(j)
        i = j % per_stream
        start = (j - i) * QBLK + min(max(i * QBLK - B_RADIUS, 0), length - 2 * QBLK)
        return slice(start, start + 2 * QBLK)

    def scores(task, _):
        gi, j = task
        per_stream = SEQ // B_GROUPS[gi][1] // QBLK
        if per_stream == 1:
            biases = (bn_ref[0, 0], bn_ref[1, 0])
        else:
            i = j % per_stream
            var = gi * 3 + (0 if i == 0 else 2 if i == per_stream - 1 else 1)
            biases = (bw_ref[0, var], bw_ref[1, var])
        q_ref, k_ref, _ = sources[gi]
        return _pair_scores(view(q_ref, gi, rows(j)), view(k_ref, gi, keys(gi, j)), biases)

    def values(task, s):
        gi, j = task
        return _pair_values(s, view(sources[gi][2], gi, keys(gi, j)))

    def store(task, res):
        gi, j = task
        for slot, value in enumerate(_pair_merge(res)):
            if gi == 0:
                merged[0, slot, rows(j), :] = value
            else:
                stream[gi - 1, slot, rows(j), :] = value

    _software_pipeline([(gi, j) for gi in range(len(B_GROUPS)) for j in range(n_blocks)], (scores, values, store))
    for slot in range(3):
        for g in range(d16):
            start = (g % d4) * len4 + g // d4
            by4_out[slot, pl.ds(start, len16, stride=d4), :] = stream[1, slot, g * len16:(g + 1) * len16, :]
        for g in range(d4):
            run = slice(g * len4, (g + 1) * len4)
            merged[1, slot, pl.ds(g, len4, stride=d4), :] = stream[0, slot, run, :]
            merged[2, slot, pl.ds(g, len4, stride=d4), :] = by4_out[slot, run, :]

    def merge(c, carry):
        rows = pl.ds(pl.multiple_of(c * QBLK, QBLK), QBLK)
        m = [merged[gi, 0, rows, :] for gi in range(3)]
        top = jnp.maximum(jnp.maximum(m[0], m[1]), m[2])
        w = [jnp.exp2(mg - top) for mg in m]
        l = [pltpu.roll(merged[gi, 1, rows, :], HALF, axis=1) for gi in range(3)]
        num = w[0] * merged[0, 2, rows, :] + w[1] * merged[1, 2, rows, :] + w[2] * merged[2, 2, rows, :]
        den = w[0] * l[0] + w[1] * l[1] + w[2] * l[2]
        o_ref[rows, :] = (num / den).astype(BF16)
        return carry

    lax.fori_loop(0, n_blocks, merge, 0, unroll=4)


def _dilated_attention(zb, bias_wide, bias_narrow, batch):
    t = zb.shape[0]
    n_pairs = B_HEADS // 2
    nw = bias_wide.shape[1]
    d4, d16 = B_GROUPS[1][1], B_GROUPS[2][1]
    seq_block = lambda off: pl.BlockSpec((SEQ, LANES), lambda p, b: (b, off + p))
    return pl.pallas_call(
        _dilated_kernel,
        out_shape=jax.ShapeDtypeStruct((t, B_HEADS * B_HEAD_DIM), BF16),
        grid=(n_pairs, batch),
        in_specs=[seq_block(0), seq_block(n_pairs), seq_block(2 * n_pairs),
                  pl.BlockSpec((2, nw, QBLK, 2 * QBLK), lambda p, b: (p, 0, 0, 0)),
                  pl.BlockSpec((2, 1, QBLK, QBLK), lambda p, b: (p, 0, 0, 0))],
        out_specs=pl.BlockSpec((SEQ, LANES), lambda p, b: (b, p)),
        scratch_shapes=[pltpu.VMEM((3, SEQ, LANES), F32), pltpu.VMEM((3, SEQ, LANES), F32),
                        pltpu.VMEM((3, d4, SEQ // d4, LANES), BF16), pltpu.VMEM((3, d16, SEQ // d16, LANES), BF16),
                        pltpu.VMEM((2, 3, SEQ, LANES), F32), pltpu.VMEM((3, SEQ, LANES), F32),
                        pltpu.VMEM((3, 3, SEQ, LANES), F32)],
        compiler_params=_cparams(("arbitrary", "arbitrary")),
        name="dilated_attention",
    )(zb, zb, zb, bias_wide, bias_narrow)


def _window_kernel(sink_ref, q_ref, k_ref, v_ref, bias_ref, o_ref):
    pair = pl.program_id(0)
    sinks = (sink_ref[pair] * LOG2E, sink_ref[pair + C_HEADS // 2] * LOG2E)
    n_blocks = SEQ // QBLK

    def keys(i):
        start = min(max((i - 1) * QBLK, 0), SEQ - 3 * QBLK)
        return slice(start, start + 3 * QBLK)

    def scores(i, _):
        var = 0 if i == 0 else 2 if i == n_blocks - 1 else 1
        return _pair_scores(q_ref[i * QBLK:(i + 1) * QBLK, :], k_ref[keys(i), :],
                            (bias_ref[0, var], bias_ref[1, var]))

    def values(i, s):
        return _pair_values(s, v_ref[keys(i), :], sinks)

    def store(i, res):
        _, l_swapped, o = _pair_merge(res)
        o_ref[i * QBLK:(i + 1) * QBLK, :] = (o / pltpu.roll(l_swapped, HALF, axis=1)).astype(BF16)

    _software_pipeline(list(range(n_blocks)), (scores, values, store))


def _window_attention(zc, sink, bias, batch):
    t = zc.shape[0]
    n_pairs = C_HEADS // 2
    return pl.pallas_call(
        _window_kernel,
        out_shape=jax.ShapeDtypeStruct((t, C_HEADS * C_HEAD_DIM), BF16),
        grid=(n_pairs, batch),
        in_specs=[pl.BlockSpec(memory_space=pltpu.SMEM),
                  pl.BlockSpec((SEQ, LANES), lambda p, b: (b, p)),
                  pl.BlockSpec((SEQ, LANES), lambda p, b: (b, n_pairs)),
                  pl.BlockSpec((SEQ, LANES), lambda p, b: (b, n_pairs + 1)),
                  pl.BlockSpec((None, 2, 3, QBLK, 3 * QBLK), lambda p, b: (p, 0, 0, 0, 0))],
        out_specs=pl.BlockSpec((SEQ, LANES), lambda p, b: (b, p)),
        compiler_params=_cparams(("arbitrary", "arbitrary")),
        name="window_attention",
    )(sink, zc, zc, zc, bias)


MERGE_TM = 1024
MERGE_CHAINS = 4


def _merge_kernel(x_ref, g1_ref, ya_ref, yb_ref, yc_ref, wg_ref, wbr_ref, wo_ref, o_ref):
    sub = x_ref.shape[0] // MERGE_CHAINS

    def rows(r):
        return slice(r * sub, (r + 1) * sub)

    def gated_sum(r, _):
        h = _rms(x_ref[rows(r), :], g1_ref[...]).astype(BF16)
        merged = None
        for b, y_ref in enumerate((ya_ref, yb_ref, yc_ref)):
            gate = jax.nn.sigmoid(_dot(h, wg_ref[:, b * D_MODEL:(b + 1) * D_MODEL]))
            term = gate * _dot(y_ref[rows(r), :], wbr_ref[b])
            merged = term if merged is None else merged + term
        return merged.astype(BF16)

    def project(r, merged):
        o_ref[rows(r), :] = x_ref[rows(r), :] + _dot(merged, wo_ref[...])

    _software_pipeline(list(range(MERGE_CHAINS)), (gated_sum, project))


def _merge(x2, g1, ya, yb, yc, wg, wbr, wo):
    t = x2.shape[0]
    tm = MERGE_TM
    full = lambda a: pl.BlockSpec(a.shape, lambda i: (0,) * a.ndim)
    row = lambda w: pl.BlockSpec((tm, w), lambda i: (i, 0))
    return pl.pallas_call(
        _merge_kernel,
        out_shape=jax.ShapeDtypeStruct((t, D_MODEL), F32),
        grid=(t // tm,),
        in_specs=[row(D_MODEL), full(g1), row(ya.shape[1]), row(yb.shape[1]), row(yc.shape[1]),
                  full(wg), full(wbr), full(wo)],
        out_specs=row(D_MODEL),
        compiler_params=_cparams(("parallel",)),
        name="gate_merge",
    )(x2, g1, ya, yb, yc, wg, wbr, wo)


TOP_K = 2
ROW_SUBLANES = D_MODEL // LANES
MOE_TM = 512
EXPERT_CHAINS = 2


def _router_kernel(x_ref, g_ref, wr_ref, eid_ref, rank_ref, gate_ref, cnt_ref, carry):
    @pl.when(pl.program_id(0) == 0)
    def _():
        carry[...] = jnp.zeros_like(carry)

    h = _rms(x_ref[...], g_ref[...])
    w = wr_ref[...]
    h_hi = h.astype(BF16)
    w_hi = w.astype(BF16)
    h_lo = (h - h_hi.astype(F32)).astype(BF16)
    w_lo = (w - w_hi.astype(F32)).astype(BF16)
    logits = _dot(h_hi, w_hi) + (_dot(h_lo, w_hi) + _dot(h_hi, w_lo))
    idx = lax.broadcasted_iota(jnp.int32, logits.shape, 1)
    m1 = jnp.max(logits, axis=-1, keepdims=True)
    i1 = jnp.min(jnp.where(logits == m1, idx, N_EXPERTS), axis=-1, keepdims=True)
    first = idx == i1
    rest = jnp.where(first, -jnp.inf, logits)
    m2 = jnp.max(rest, axis=-1, keepdims=True)
    i2 = jnp.min(jnp.where(rest == m2, idx, N_EXPERTS), axis=-1, keepdims=True)
    second = idx == i2
    e = jnp.exp(m2 - m1)
    den = 1.0 + e
    chosen = jnp.where(jnp.logical_or(first, second), 1.0, 0.0)
    tm = chosen.shape[0]
    earlier = (lax.broadcasted_iota(jnp.int32, (tm, tm), 0) > lax.broadcasted_iota(jnp.int32, (tm, tm), 1))
    before = _dot(jnp.where(earlier, 1.0, 0.0).astype(BF16), chosen.astype(BF16)) + carry[...]
    r1 = jnp.sum(jnp.where(first, before, 0.0), axis=-1, keepdims=True)
    r2 = jnp.sum(jnp.where(second, before, 0.0), axis=-1, keepdims=True)
    carry[...] += jnp.sum(chosen, axis=0, keepdims=True)
    slot0 = lax.broadcasted_iota(jnp.int32, (tm, TOP_K), 1) == 0
    eid_ref[...] = jnp.where(slot0, i1, i2)
    rank_ref[...] = jnp.where(slot0, r1, r2).astype(jnp.int32)
    gate_ref[...] = jnp.where(slot0, 1.0 / den, e / den)
    cnt_ref[...] = carry[...].astype(jnp.int32)


def _router(x2, g, wr):
    t = x2.shape[0]
    tm = MOE_TM
    pair = pl.BlockSpec((tm, TOP_K), lambda i: (i, 0))
    return pl.pallas_call(
        _router_kernel,
        out_shape=(jax.ShapeDtypeStruct((t, TOP_K), jnp.int32), jax.ShapeDtypeStruct((t, TOP_K), jnp.int32),
                   jax.ShapeDtypeStruct((t, TOP_K), F32), jax.ShapeDtypeStruct((1, N_EXPERTS), jnp.int32)),
        grid=(t // tm,),
        in_specs=[pl.BlockSpec((tm, D_MODEL), lambda i: (i, 0)),
                  pl.BlockSpec(g.shape, lambda i: (0, 0)),
                  pl.BlockSpec(wr.shape, lambda i: (0, 0))],
        out_specs=(pair, pair, pair, pl.BlockSpec((1, N_EXPERTS), lambda i: (0, 0))),
        scratch_shapes=[pltpu.VMEM((1, N_EXPERTS), F32)],
        compiler_params=_cparams(("arbitrary",)),
        name="router",
    )(x2, g, wr)


def _row_tile(ref, row):
    return ref.at[pl.ds(pl.multiple_of(row * ROW_SUBLANES, ROW_SUBLANES), ROW_SUBLANES), :]


def _dispatch_kernel(tail_start_ref, tail_len_ref, na_ref, dest_ref, x_ref, g_ref, hs_ref, stage, sems):
    i = pl.program_id(0)
    n_steps = pl.num_programs(0)
    tm = x_ref.shape[0]
    slot = i % 2
    rows_per_slot = stage.shape[1]

    def drain(s):
        for _ in range(TOP_K):
            pltpu.make_async_copy(stage.at[s], hs_ref.at[pl.ds(0, rows_per_slot), :], sems.at[s]).wait()

    @pl.when(i >= 2)
    def _():
        drain(slot)

    h = _rms(x_ref[...], g_ref[...])
    for s in range(ROW_SUBLANES):
        stage[slot, pl.ds(s, tm, stride=ROW_SUBLANES), :] = h[:, s * LANES:(s + 1) * LANES]

    def start(t, carry):
        for k in range(TOP_K):
            pltpu.make_async_copy(_row_tile(stage.at[slot], t), _row_tile(hs_ref, dest_ref[TOP_K * t + k]),
                                  sems.at[slot]).start(priority=k)
        return carry

    lax.fori_loop(0, tm, start, 0, unroll=8)

    @pl.when(i == n_steps - 1)
    def _():
        @pl.when(n_steps >= 2)
        def _():
            drain(1 - slot)

        drain(slot)
        zeros = stage.at[0]
        zeros[...] = jnp.zeros_like(zeros)
        for e in range(N_EXPERTS):
            def tail(r, e=e):
                return pltpu.make_async_copy(_row_tile(zeros, 0), _row_tile(hs_ref, tail_start_ref[e] + r), sems.at[0])

            lax.fori_loop(0, tail_len_ref[e], lambda r, c: (tail(r).start(), c)[1], 0)
            lax.fori_loop(0, tail_len_ref[e], lambda r, c: (tail(r).wait(), c)[1], 0)
        n_tiles = hs_ref.shape[0] // rows_per_slot

        def spare(j):
            rows = pl.ds(pl.multiple_of(j * rows_per_slot, rows_per_slot), rows_per_slot)
            return pltpu.make_async_copy(zeros, hs_ref.at[rows, :], sems.at[0])

        lax.fori_loop(na_ref[0], n_tiles, lambda j, c: (spare(j).start(), c)[1], 0)
        lax.fori_loop(na_ref[0], n_tiles, lambda j, c: (spare(j).wait(), c)[1], 0)


def _dispatch(x2, g, dest, tail_start, tail_len, n_active, n_rows):
    t = x2.shape[0]
    tm = MOE_TM
    return pl.pallas_call(
        _dispatch_kernel,
        out_shape=jax.ShapeDtypeStruct((n_rows * ROW_SUBLANES, LANES), F32),
        grid_spec=pltpu.PrefetchScalarGridSpec(
            num_scalar_prefetch=3,
            grid=(t // tm,),
            in_specs=[pl.BlockSpec((TOP_K * tm,), lambda i, ts, tl, na: (i,), memory_space=pltpu.SMEM),
                      pl.BlockSpec((tm, D_MODEL), lambda i, ts, tl, na: (i, 0)),
                      pl.BlockSpec(g.shape, lambda i, ts, tl, na: (0, 0))],
            out_specs=pl.BlockSpec(memory_space=pl.ANY),
            scratch_shapes=[pltpu.VMEM((2, tm * ROW_SUBLANES, LANES), F32), pltpu.SemaphoreType.DMA((2,))]),
        compiler_params=_cparams(("arbitrary",)),
        name="dispatch",
    )(tail_start, tail_len, n_active, dest, x2, g)


def _experts_kernel(te_ref, na_ref, hs_ref, wg_ref, wu_ref, wd_ref, ys_ref):
    active = pl.program_id(0) < na_ref[0]
    sub = hs_ref.shape[0] // ROW_SUBLANES // EXPERT_CHAINS

    def lane_groups(r):
        return [pl.ds(r * sub * ROW_SUBLANES + s, sub, stride=ROW_SUBLANES) for s in range(ROW_SUBLANES)]

    def up(r, _):
        h = jnp.concatenate([hs_ref[g, :].astype(BF16) for g in lane_groups(r)], axis=1)
        return (jax.nn.silu(_dot(h, wg_ref[...])) * _dot(h, wu_ref[...])).astype(BF16)

    def down(r, act):
        return _dot(act, wd_ref[...])

    def store(r, y):
        for s, g in enumerate(lane_groups(r)):
            ys_ref[g, :] = y[:, s * LANES:(s + 1) * LANES]

    @pl.when(active)
    def _():
        _software_pipeline(list(range(EXPERT_CHAINS)), (up, down, store))

    @pl.when(jnp.logical_not(active))
    def _():
        ys_ref[...] = jnp.zeros_like(ys_ref)


def _experts(hs, tile_expert, n_active, wg, wu, wd):
    tm = MOE_TM
    n_tiles = hs.shape[0] // (tm * ROW_SUBLANES)
    d_ff = wg.shape[2]
    once = pl.Buffered(1)
    return pl.pallas_call(
        _experts_kernel,
        out_shape=jax.ShapeDtypeStruct(hs.shape, F32),
        grid_spec=pltpu.PrefetchScalarGridSpec(
            num_scalar_prefetch=2,
            grid=(n_tiles,),
            in_specs=[pl.BlockSpec((tm * ROW_SUBLANES, LANES), lambda i, te, na: (jnp.minimum(i, na[0] - 1), 0)),
                      pl.BlockSpec((None, D_MODEL, d_ff), lambda i, te, na: (te[i], 0, 0), pipeline_mode=once),
                      pl.BlockSpec((None, D_MODEL, d_ff), lambda i, te, na: (te[i], 0, 0), pipeline_mode=once),
                      pl.BlockSpec((None, d_ff, D_MODEL), lambda i, te, na: (te[i], 0, 0), pipeline_mode=once)],
            out_specs=pl.BlockSpec((tm * ROW_SUBLANES, LANES), lambda i, te, na: (i, 0))),
        compiler_params=_cparams(("arbitrary",)),
        name="experts",
    )(tile_expert, n_active, hs, wg, wu, wd)


def _combine_kernel(dest_ref, next_dest_ref, x_ref, gate_ref, gf_ref, ys_ref, o_ref, bufs, sems, *, final):
    i = pl.program_id(0)
    tm = x_ref.shape[0]
    slot = i % 2

    def gather(idx_ref, s):
        def start(t, carry):
            for k in range(TOP_K):
                pltpu.make_async_copy(_row_tile(ys_ref, idx_ref[TOP_K * t + k]), _row_tile(bufs.at[s, k], t),
                                      sems.at[s]).start(priority=k)
            return carry

        lax.fori_loop(0, tm, start, 0, unroll=8)

    @pl.when(i == 0)
    def _():
        gather(dest_ref, slot)

    @pl.when(i + 1 < pl.num_programs(0))
    def _():
        gather(next_dest_ref, 1 - slot)

    for k in range(TOP_K):
        pltpu.make_async_copy(ys_ref.at[pl.ds(0, bufs.shape[2]), :], bufs.at[slot, k], sems.at[slot]).wait()
    gate = gate_ref[...]
    g0 = gate[:, 0:1]
    g1 = gate[:, 1:2]
    ss = jnp.zeros((tm, 1), F32)
    for s in range(ROW_SUBLANES):
        sl = slice(s * LANES, (s + 1) * LANES)
        rows = pl.ds(s, tm, stride=ROW_SUBLANES)
        y = x_ref[:, sl] + (g0 * bufs[slot, 0, rows, :] + g1 * bufs[slot, 1, rows, :])
        o_ref[:, sl] = y
        ss = ss + jnp.sum(y * y, axis=-1, keepdims=True)
    if final:
        o_ref[...] = o_ref[...] * lax.rsqrt(ss / D_MODEL + EPS) * gf_ref[...]


def _combine(x2, dest, gates, gf, ys, *, final):
    t = x2.shape[0]
    tm = MOE_TM
    return pl.pallas_call(
        functools.partial(_combine_kernel, final=final),
        out_shape=jax.ShapeDtypeStruct((t, D_MODEL), F32),
        grid=(t // tm,),
        in_specs=[pl.BlockSpec((TOP_K * tm,), lambda i: (i,), memory_space=pltpu.SMEM),
                  pl.BlockSpec((TOP_K * tm,), lambda i: (jnp.minimum(i + 1, t // tm - 1),),
                               memory_space=pltpu.SMEM),
                  pl.BlockSpec((tm, D_MODEL), lambda i: (i, 0)),
                  pl.BlockSpec((tm, TOP_K), lambda i: (i, 0)),
                  pl.BlockSpec(gf.shape, lambda i: (0, 0)),
                  pl.BlockSpec(memory_space=pl.ANY)],
        out_specs=pl.BlockSpec((tm, D_MODEL), lambda i: (i, 0)),
        scratch_shapes=[pltpu.VMEM((2, TOP_K, tm * ROW_SUBLANES, LANES), F32), pltpu.SemaphoreType.DMA((2,))],
        compiler_params=_cparams(("arbitrary",)),
        name="combine",
    )(dest, dest, x2, gates, gf, ys)


def _moe(x2, g, wr, wg, wu, wd, gf, *, final):
    t = x2.shape[0]
    tm = MOE_TM
    eid, rank, gates, counts = _router(x2, g, wr)
    counts = counts[0]
    padded = (counts + tm - 1) // tm * tm
    group_end = jnp.cumsum(padded)
    group_start = group_end - padded
    dest = (group_start[eid] + rank).reshape(-1)
    n_tiles = TOP_K * t // tm + N_EXPERTS
    n_active = (group_end[-1:] // tm).astype(jnp.int32)
    tile_row = jnp.arange(n_tiles, dtype=jnp.int32) * tm
    tile_row = jnp.minimum(tile_row, group_end[-1] - tm)
    tile_expert = jnp.sum(tile_row[:, None] >= group_end[None, :], axis=1).astype(jnp.int32)
    hs = _dispatch(x2, g, dest, group_start + counts, padded - counts, n_active, n_tiles * tm)
    ys = _experts(hs, tile_expert, n_active, wg, wu, wd)
    return _combine(x2, dest, gates, gf, ys, final=final)


FFN_TM = 1024
FFN_CHAINS = 4


def _ffn_kernel(x_ref, g_ref, wg_ref, wu_ref, wd_ref, gf_ref, o_ref, *, final):
    sub = x_ref.shape[0] // FFN_CHAINS

    def rows(r):
        return slice(r * sub, (r + 1) * sub)

    def up(r, _):
        h = _rms(x_ref[rows(r), :], g_ref[...]).astype(BF16)
        return (jax.nn.silu(_dot(h, wg_ref[...])) * _dot(h, wu_ref[...])).astype(BF16)

    def down(r, act):
        y = x_ref[rows(r), :] + _dot(act, wd_ref[...])
        if final:
            y = _rms(y, gf_ref[...])
        o_ref[rows(r), :] = y

    _software_pipeline(list(range(FFN_CHAINS)), (up, down))


def _ffn(x2, g, wg, wu, wd, gf, *, final):
    t = x2.shape[0]
    tm = FFN_TM
    full = lambda a: pl.BlockSpec(a.shape, lambda i: (0,) * a.ndim)
    return pl.pallas_call(
        functools.partial(_ffn_kernel, final=final),
        out_shape=jax.ShapeDtypeStruct((t, D_MODEL), F32),
        grid=(t // tm,),
        in_specs=[pl.BlockSpec((tm, D_MODEL), lambda i: (i, 0)), full(g), full(wg), full(wu), full(wd), full(gf)],
        out_specs=pl.BlockSpec((tm, D_MODEL), lambda i: (i, 0)),
        compiler_params=_cparams(("parallel",)),
        name="swiglu",
    )(x2, g, wg, wu, wd, gf)


def _rot_cols(w):
    half = A_ROPE // 2
    return jnp.concatenate([-w[..., half:], w[..., :half]], axis=-1)


def _prep_layer(w_in, w_uq, w_ukv, w_br_a, w_br_b, w_br_c):
    o = 0
    w_cq = w_in[:, o:o + A_Q_LORA]; o += A_Q_LORA
    w_ckv = w_in[:, o:o + A_KV_LORA]; o += A_KV_LORA
    w_kpe = w_in[:, o:o + A_ROPE]; o += A_ROPE
    nb = B_HEADS * B_HEAD_DIM
    w_b = w_in[:, o:o + 3 * nb]; o += 3 * nb
    nq = C_HEADS * C_HEAD_DIM
    nkv = C_KV_HEADS * C_HEAD_DIM
    w_c = w_in[:, o:o + nq + 2 * nkv]; o += nq + 2 * nkv
    w_g = w_in[:, o:]
    pad = jnp.zeros((D_MODEL, LANES - 2 * A_ROPE), F32)
    wa = jnp.concatenate([w_cq, w_ckv, w_kpe, _rot_cols(w_kpe), pad], axis=1)
    wb = jnp.concatenate([w_b[:, :nb] * (B_HEAD_DIM ** -0.5), w_b[:, nb:]], axis=1)
    q_c = w_c[:, :nq].reshape(D_MODEL, 2, C_HEADS // 2, C_HEAD_DIM).transpose(0, 2, 1, 3).reshape(D_MODEL, nq)
    wc = jnp.concatenate([q_c * (C_HEAD_DIM ** -0.5), w_c[:, nq:]], axis=1)
    uq = w_uq.reshape(A_Q_LORA, A_HEADS, A_NOPE + A_ROPE)
    z32 = jnp.zeros((A_Q_LORA, A_HEADS, LANES - A_NOPE - A_ROPE), F32)
    wq = jnp.concatenate([uq, z32], axis=-1).reshape(A_Q_LORA, A_HEADS * LANES)
    wqr = jnp.concatenate([jnp.zeros((A_Q_LORA, A_HEADS, A_NOPE), F32), _rot_cols(uq[..., A_NOPE:]), z32],
                          axis=-1).reshape(A_Q_LORA, A_HEADS * LANES)
    ukv = w_ukv.reshape(A_KV_LORA, A_HEADS, A_NOPE + A_V)
    wk = jnp.concatenate([ukv[..., :A_NOPE], jnp.zeros((A_KV_LORA, A_HEADS, LANES - A_NOPE), F32)],
                         axis=-1).reshape(A_KV_LORA, A_HEADS * LANES)
    uv = ukv[..., A_NOPE:].reshape(A_KV_LORA, A_HEADS // 2, 2, A_V)
    zv = jnp.zeros_like(uv[:, :, 0])
    wv = jnp.stack([jnp.concatenate([uv[:, :, 0], zv], axis=-1), jnp.concatenate([zv, uv[:, :, 1]], axis=-1)],
                   axis=2).reshape(A_KV_LORA, A_HEADS * LANES)
    w_br_c = w_br_c.reshape(2, C_HEADS // 2, C_HEAD_DIM, D_MODEL).transpose(1, 0, 2, 3).reshape(nq, D_MODEL)
    wbr = jnp.stack([w_br_a, w_br_b, w_br_c])
    wkv = jnp.concatenate([jnp.concatenate([wk, wv], axis=1),
                           jnp.concatenate([_rope_placement(), jnp.zeros((LANES, A_HEADS * LANES), F32)], axis=1)])
    cast = lambda a: a.astype(BF16)
    return dict(wa=cast(wa), wb=cast(wb), wc=cast(wc), wg=cast(w_g), wq=cast(wq), wqr=cast(wqr), wkv=cast(wkv),
                wbr=cast(wbr))


def _rope_tables():
    half = A_ROPE // 2
    inv = ROPE_THETA ** (-jnp.arange(half, dtype=F32) / half)
    ang = jnp.arange(SEQ, dtype=F32)[:, None] * inv[None, :]
    cos2 = jnp.tile(jnp.cos(ang), (1, 2))
    sin2 = jnp.tile(jnp.sin(ang), (1, 2))
    ones = jnp.ones((SEQ, A_NOPE), F32)
    z = lambda w: jnp.zeros((SEQ, w), F32)
    cq = jnp.concatenate([ones, cos2, z(LANES - A_NOPE - A_ROPE)], axis=1)
    sq = jnp.concatenate([z(A_NOPE), sin2, z(LANES - A_NOPE - A_ROPE)], axis=1)
    ck = jnp.concatenate([cos2, z(LANES - A_ROPE)], axis=1)
    sk = jnp.concatenate([sin2, z(LANES - A_ROPE)], axis=1)
    return cq, sq, ck, sk


def _rope_placement():
    place = np.zeros((LANES, A_HEADS * LANES), np.float32)
    for h in range(A_HEADS):
        for j in range(A_ROPE):
            place[j, h * LANES + A_NOPE + j] = 1.0
    return jnp.asarray(place)


def kernel(x, norm1_g, w_in, q_norm_g, w_uq, kv_norm_g, w_ukv, sink_logit, rel_bias, w_branch_a, w_branch_b,
           w_branch_c, w_out, norm2_g, ffn_w_gate, ffn_w_up, ffn_w_down, router_w, exp_w_gate, exp_w_up,
           exp_w_down, final_g):
    batch, seq, d = x.shape
    assert seq == SEQ and d == D_MODEL
    t = batch * seq
    x2 = x.reshape(t, d)

    wide_idx, narrow_idx = _dilated_index_tables()
    bias_wide = _bias_tables(rel_bias, wide_idx, 0, B_HEADS)
    bias_narrow = _bias_tables(rel_bias, narrow_idx, 0, B_HEADS)
    bias_c = _bias_tables(rel_bias, _window_index_tables(), B_HEADS, C_HEADS)
    bias_c = bias_c.reshape(2, C_HEADS // 2, *bias_c.shape[1:]).transpose(1, 0, 2, 3, 4)

    cq, sq, ck, sk = _rope_tables()
    row = lambda v: v.reshape(1, -1)
    q_then = lambda nq, rest: jnp.asarray(np.concatenate([np.full(nq, LOG2E), np.ones(rest)])[None], F32)
    bscale = q_then(B_HEADS * B_HEAD_DIM, 2 * B_HEADS * B_HEAD_DIM)
    cscale = q_then(C_HEADS * C_HEAD_DIM, 2 * C_KV_HEADS * C_HEAD_DIM)
    vones = jnp.asarray(np.tile(np.repeat([0.0, 1.0, 1.0, 0.0], A_V), A_HEADS // 2)[None], F32)

    for l in range(DEPTH):
        w = _prep_layer(w_in[l], w_uq[l], w_ukv[l], w_branch_a[l], w_branch_b[l], w_branch_c[l])
        g1 = row(norm1_g[l])
        qa, ka, va, zb, zc = _in_proj(x2, g1, w["wa"], w["wb"], w["wc"], row(q_norm_g[l]), row(kv_norm_g[l]),
                                      w["wq"], w["wqr"], w["wkv"], bscale, cscale, vones, cq, sq, ck, sk)
        ya = _mla_attention(qa, ka, va, batch)
        yb = _dilated_attention(zb, bias_wide, bias_narrow, batch)
        yc = _window_attention(zc, sink_logit[l], bias_c, batch)
        x2 = _merge(x2, g1, ya, yb, yc, w["wg"], w["wbr"], w_out[l].astype(BF16))
        g2 = row(norm2_g[l])
        final = l == DEPTH - 1
        i = l // 2
        if l % 2 == 0:
            x2 = _ffn(x2, g2, ffn_w_gate[i].astype(BF16), ffn_w_up[i].astype(BF16), ffn_w_down[i].astype(BF16),
                      row(final_g), final=final)
        else:
            x2 = _moe(x2, g2, router_w[i], exp_w_gate[i].astype(BF16), exp_w_up[i].astype(BF16),
                      exp_w_down[i].astype(BF16), row(final_g), final=final)
    return x2.reshape(batch, seq, d)
```

```python
import functools
import math

import numpy as np
import jax
import jax.numpy as jnp
from jax import lax
from jax.experimental import pallas as pl
from jax.experimental.pallas import tpu as pltpu

F32 = jnp.float32
BF16 = jnp.bfloat16

D_MODEL = 1024
SEQ = 2048
DEPTH = 2
A_HEADS = 8
A_NOPE = 64
A_ROPE = 32
A_V = 64
A_Q_LORA = 256
A_KV_LORA = 128
ROPE_THETA = 10000.0
B_HEADS = 8
B_HEAD_DIM = 64
B_GROUPS = ((128, 1), (512, 4), (2048, 16))
C_HEADS = 8
C_KV_HEADS = 2
C_HEAD_DIM = 64
C_RADIUS = 128
N_BUCKETS = 32
MAX_DISTANCE = 1024
D_FF_DENSE = 2816
N_EXPERTS = 8
D_FF_EXPERT = 3584
EPS = 1e-6
NEG_INF = -1e30

LANES = 128
HALF = 64
QBLK = 128
B_RADIUS = 64
VMEM_LIMIT = 56 * 1024 * 1024

A_SCALE = (A_NOPE + A_ROPE) ** -0.5
LOG2E = math.log2(math.e)


def _cparams(sem):
    return pltpu.CompilerParams(dimension_semantics=sem, vmem_limit_bytes=VMEM_LIMIT)


def _rms(x, g):
    return x * lax.rsqrt(jnp.mean(x * x, axis=-1, keepdims=True) + EPS) * g


def _dot(a, b):
    return jnp.dot(a, b, preferred_element_type=F32)


def _dot_nt(a, b):
    return lax.dot_general(a, b, (((1,), (1,)), ((), ())), preferred_element_type=F32)


def _t5_bucket_np(rel):
    nb = N_BUCKETS // 2
    max_exact = nb // 2
    ret = np.where(rel > 0, nb, 0)
    n = np.abs(rel)
    nf = np.maximum(n, 1).astype(np.float64)
    t = np.log(nf / max_exact) / math.log(MAX_DISTANCE / max_exact) * (nb - max_exact)
    frac = np.abs(t - np.round(t))
    assert np.all((frac > 1e-5) | (n <= max_exact) | (n >= MAX_DISTANCE))
    large = max_exact + np.floor(np.round(t, 9)).astype(np.int64)
    large = np.minimum(large, nb - 1)
    return (ret + np.where(n < max_exact, n, large)).astype(np.int32)


def _band_index(tq, tk, shift, radius, dil):
    rel = np.arange(tk)[None, :] - np.arange(tq)[:, None] + shift
    idx = _t5_bucket_np(rel * dil)
    return np.where(np.abs(rel) <= radius, idx, -1).astype(np.int32)


def _dilated_index_tables():
    wide = []
    for window, dil in B_GROUPS[:2]:
        for shift in (0, -B_RADIUS, -2 * B_RADIUS):
            wide.append(_band_index(QBLK, 2 * QBLK, shift, B_RADIUS, dil))
    narrow = [_band_index(QBLK, QBLK, 0, B_RADIUS, B_GROUPS[2][1])]
    return np.stack(wide), np.stack(narrow)


def _window_index_tables():
    return np.stack([_band_index(QBLK, 3 * QBLK, shift, C_RADIUS, 1) for shift in (0, -QBLK, -2 * QBLK)])


def _bias_kernel(tab_ref, idx_ref, o_ref, *, col0):
    col = pl.program_id(0) + col0
    idx = idx_ref[...]
    acc = jnp.full(idx.shape, NEG_INF, F32)
    for b in range(N_BUCKETS):
        acc = jnp.where(idx == b, tab_ref[b, col] * LOG2E, acc)
    o_ref[...] = acc


def _bias_tables(rel_bias, idx_np, col0, n_heads):
    nv, r, c = idx_np.shape
    return pl.pallas_call(
        functools.partial(_bias_kernel, col0=col0),
        out_shape=jax.ShapeDtypeStruct((n_heads, nv, r, c), F32),
        grid=(n_heads, nv),
        in_specs=[pl.BlockSpec(memory_space=pltpu.SMEM),
                  pl.BlockSpec((None, r, c), lambda h, v: (v, 0, 0))],
        out_specs=pl.BlockSpec((None, None, r, c), lambda h, v: (h, v, 0, 0)),
        compiler_params=_cparams(("arbitrary", "arbitrary")),
        name="bias_tables",
    )(rel_bias, jnp.asarray(idx_np))


IN_TM = 1024
IN_CHAINS = 4


def _in_proj_kernel(x_ref, g1_ref, wa_ref, wb_ref, wc_ref, qg_ref, kvg_ref, wq_ref, wqr_ref, wkv_ref,
                    bscale_ref, cscale_ref, vones_ref, cq_ref, sq_ref, ck_ref, sk_ref,
                    qa_ref, ka_ref, va_ref, zb_ref, zc_ref):
    sub = x_ref.shape[0] // IN_CHAINS

    def rows(r):
        return slice(r * sub, (r + 1) * sub)

    def project(r, _):
        h = _rms(x_ref[rows(r), :], g1_ref[...]).astype(BF16)
        za = _dot(h, wa_ref[...])
        zb_ref[rows(r), :] = (_dot(h, wb_ref[...]) * bscale_ref[...]).astype(BF16)
        zc_ref[rows(r), :] = (_dot(h, wc_ref[...]) * cscale_ref[...]).astype(BF16)
        return za

    def expand(r, za):
        hq = _rms(za[:, :A_Q_LORA], qg_ref[...]).astype(BF16)
        hkv = _rms(za[:, A_Q_LORA:A_Q_LORA + A_KV_LORA], kvg_ref[...]).astype(BF16)
        kpe = za[:, A_Q_LORA + A_KV_LORA:]
        kpe_rot = pltpu.roll(kpe, LANES - A_ROPE, axis=1)
        k_rope = (kpe * ck_ref[rows(r), :] + kpe_rot * sk_ref[rows(r), :]).astype(BF16)
        q = _dot(hq, wq_ref[...]) * (A_SCALE * LOG2E)
        q_rot = _dot(hq, wqr_ref[...]) * (A_SCALE * LOG2E)
        kv = _dot(jnp.concatenate([hkv, k_rope], axis=1), wkv_ref[...])
        return q, q_rot, kv

    def store(r, state):
        q, q_rot, kv = state
        ka_ref[rows(r), :] = kv[:, :A_HEADS * LANES].astype(BF16)
        va_ref[rows(r), :] = (kv[:, A_HEADS * LANES:] + vones_ref[...]).astype(BF16)
        cq = cq_ref[rows(r), :]
        sq = sq_ref[rows(r), :]
        for hh in range(A_HEADS):
            sl = slice(hh * LANES, (hh + 1) * LANES)
            qa_ref[rows(r), sl] = (q[:, sl] * cq + q_rot[:, sl] * sq).astype(BF16)

    _software_pipeline(list(range(IN_CHAINS)), (project, expand, store))


def _in_proj(x2, g1, wa, wb, wc, qg, kvg, wq, wqr, wkv, bscale, cscale, vones, cq, sq, ck, sk):
    t = x2.shape[0]
    tm = IN_TM
    n_pos = SEQ // tm
    full = lambda a: pl.BlockSpec(a.shape, lambda i: (0,) * a.ndim)
    row = lambda w: pl.BlockSpec((tm, w), lambda i: (i, 0))
    pos = pl.BlockSpec((tm, LANES), lambda i: (i % n_pos, 0))
    return pl.pallas_call(
        _in_proj_kernel,
        out_shape=(jax.ShapeDtypeStruct((t, A_HEADS * LANES), BF16),
                   jax.ShapeDtypeStruct((t, A_HEADS * LANES), BF16),
                   jax.ShapeDtypeStruct((t, A_HEADS * LANES), BF16),
                   jax.ShapeDtypeStruct((t, wb.shape[1]), BF16),
                   jax.ShapeDtypeStruct((t, wc.shape[1]), BF16)),
        grid=(t // tm,),
        in_specs=[row(D_MODEL), full(g1), full(wa), full(wb), full(wc), full(qg), full(kvg), full(wq), full(wqr),
                  full(wkv), full(bscale), full(cscale), full(vones), pos, pos, pos, pos],
        out_specs=(row(A_HEADS * LANES), row(A_HEADS * LANES), row(A_HEADS * LANES), row(wb.shape[1]),
                   row(wc.shape[1])),
        compiler_params=_cparams(("parallel",)),
        name="in_proj",
    )(x2, g1, wa, wb, wc, qg, kvg, wq, wqr, wkv, bscale, cscale, vones, cq, sq, ck, sk)


A_TQ = 2048
A_SUB = 256


def _mla_kernel(q_ref, k_ref, v_ref, o_ref):
    first = lax.broadcasted_iota(jnp.int32, (1, LANES), 1) < HALF
    tasks = [(r, hh) for r in range(A_TQ // A_SUB) for hh in range(2)]
    done = {}

    def rows(r):
        return slice(r * A_SUB, (r + 1) * A_SUB)

    def lanes(hh):
        return slice(hh * LANES, (hh + 1) * LANES)

    def scores(task, _):
        r, hh = task
        return _dot_nt(q_ref[rows(r), lanes(hh)], k_ref[:, lanes(hh)])

    def values(task, s):
        p = jnp.exp2(s - jnp.max(s, axis=-1, keepdims=True))
        return _dot(p.astype(BF16), v_ref[:, lanes(task[1])])

    def store(task, a):
        r, hh = task
        done[task] = a
        if hh == 1:
            a0 = done.pop((r, 0))
            a1 = done.pop((r, 1))
            l = pltpu.roll(jnp.where(first, a1, a0), HALF, axis=1)
            o_ref[rows(r), :] = (jnp.where(first, a0, a1) / l).astype(BF16)

    _software_pipeline(tasks, (scores, values, store))


def _mla_attention(qa, ka, va, batch):
    t = qa.shape[0]
    nq = SEQ // A_TQ
    return pl.pallas_call(
        _mla_kernel,
        out_shape=jax.ShapeDtypeStruct((t, A_HEADS * A_V), BF16),
        grid=(batch, A_HEADS // 2, nq),
        in_specs=[pl.BlockSpec((A_TQ, 2 * LANES), lambda b, p, i: (b * nq + i, p)),
                  pl.BlockSpec((SEQ, 2 * LANES), lambda b, p, i: (b, p)),
                  pl.BlockSpec((SEQ, 2 * LANES), lambda b, p, i: (b, p))],
        out_specs=pl.BlockSpec((A_TQ, LANES), lambda b, p, i: (b * nq + i, p)),
        compiler_params=_cparams(("parallel", "parallel", "arbitrary")),
        name="mla_attention",
    )(qa, ka, va)


def _software_pipeline(tasks, stages):
    state = [None] * len(tasks)
    for step in range(len(tasks) + len(stages) - 1):
        for s, stage in enumerate(stages):
            i = step - s
            if 0 <= i < len(tasks):
                state[i] = stage(tasks[i], state[i])


def _first_half():
    return lax.broadcasted_iota(jnp.int32, (1, LANES), 1) < HALF


def _pair_scores(q2, k2, biases):
    first = _first_half()
    zero = jnp.zeros_like(q2)
    tq = q2.shape[0]
    stacked = jnp.concatenate([jnp.where(first, q2, zero), jnp.where(first, zero, q2)], axis=0)
    s = _dot_nt(stacked, k2)
    return [s[:tq] + biases[0], s[tq:] + biases[1]]


def _pair_values(scores, v2, sinks=None):
    first = _first_half()
    one = jnp.ones_like(v2)
    res = []
    for hh, s in enumerate(scores):
        own = first if hh == 0 else jnp.logical_not(first)
        m = jnp.max(s, axis=-1, keepdims=True)
        if sinks is not None:
            m = jnp.maximum(m, sinks[hh])
        a = _dot(jnp.exp2(s - m).astype(BF16), jnp.where(own, v2, one))
        if sinks is not None:
            a = a + jnp.where(own, 0.0, jnp.exp2(sinks[hh] - m))
        res.append((m, a))
    return res


def _pair_merge(res):
    first = _first_half()
    (m0, a0), (m1, a1) = res
    return jnp.where(first, m0, m1), jnp.where(first, a1, a0), jnp.where(first, a0, a1)


def _dilated_kernel(q1, k1, v1, bw_ref, bn_ref, o_ref, nat32, by4_32, by4, by16, stream, by4_out, merged):
    n_blocks = SEQ // QBLK
    d4, d16 = B_GROUPS[1][1], B_GROUPS[2][1]
    len4, len16 = SEQ // d4, SEQ // d16
    for t, ref in enumerate((q1, k1, v1)):
        nat32[t] = ref[...].astype(F32)
        for g in range(d4):
            run = nat32[t, pl.ds(g, len4, stride=d4), :]
            by4_32[t, g * len4:(g + 1) * len4, :] = run
            by4[t, g] = run.astype(BF16)
        for g in range(d16):
            start = (g % d4) * len4 + g // d4
            by16[t, g] = by4_32[t, pl.ds(start, len16, stride=d4), :].astype(BF16)
    sources = ((q1, k1, v1), tuple(by4.at[t] for t in range(3)), tuple(by16.at[t] for t in range(3)))

    def rows(j):
        return slice(j * QBLK, (j + 1) * QBLK)

    def view(ref, gi, sl):
        dil = B_GROUPS[gi][1]
        if dil == 1:
            return ref[sl, :]
        length = SEQ // dil
        g = sl.start // length
        return ref[g, sl.start - g * length:sl.stop - g * length, :]

    def keys(gi, j):
        length = SEQ // B_GROUPS[gi][1]
        per_stream = length // QBLK
        if per_stream == 1:
            return rows(j)
        i = j % per_stream
        start = (j - i) * QBLK + min(max(i * QBLK - B_RADIUS, 0), length - 2 * QBLK)
        return slice(start, start + 2 * QBLK)

    def scores(task, _):
        gi, j = task
        per_stream = SEQ // B_GROUPS[gi][1] // QBLK
        if per_stream == 1:
            biases = (bn_ref[0, 0], bn_ref[1, 0])
        else:
            i = j % per_stream
            var = gi * 3 + (0 if i == 0 else 2 if i == per_stream - 1 else 1)
            biases = (bw_ref[0, var], bw_ref[1, var])
        q_ref, k_ref, _ = sources[gi]
        return _pair_scores(view(q_ref, gi, rows(j)), view(k_ref, gi, keys(gi, j)), biases)

    def values(task, s):
        gi, j = task
        return _pair_values(s, view(sources[gi][2], gi, keys(gi, j)))

    def store(task, res):
        gi, j = task
        for slot, value in enumerate(_pair_merge(res)):
            if gi == 0:
                merged[0, slot, rows(j), :] = value
            else:
                stream[gi - 1, slot, rows(j), :] = value

    _software_pipeline([(gi, j) for gi in range(len(B_GROUPS)) for j in range(n_blocks)], (scores, values, store))
    for slot in range(3):
        for g in range(d16):
            start = (g % d4) * len4 + g // d4
            by4_out[slot, pl.ds(start, len16, stride=d4), :] = stream[1, slot, g * len16:(g + 1) * len16, :]
        for g in range(d4):
            run = slice(g * len4, (g + 1) * len4)
            merged[1, slot, pl.ds(g, len4, stride=d4), :] = stream[0, slot, run, :]
            merged[2, slot, pl.ds(g, len4, stride=d4), :] = by4_out[slot, run, :]

    def merge(c, carry):
        rows = pl.ds(pl.multiple_of(c * QBLK, QBLK), QBLK)
        m = [merged[gi, 0, rows, :] for gi in range(3)]
        top = jnp.maximum(jnp.maximum(m[0], m[1]), m[2])
        w = [jnp.exp2(mg - top) for mg in m]
        l = [pltpu.roll(merged[gi, 1, rows, :], HALF, axis=1) for gi in range(3)]
        num = w[0] * merged[0, 2, rows, :] + w[1] * merged[1, 2, rows, :] + w[2] * merged[2, 2, rows, :]
        den = w[0] * l[0] + w[1] * l[1] + w[2] * l[2]
        o_ref[rows, :] = (num / den).astype(BF16)
        return carry

    lax.fori_loop(0, n_blocks, merge, 0, unroll=4)


def _dilated_attention(zb, bias_wide, bias_narrow, batch):
    t = zb.shape[0]
    n_pairs = B_HEADS // 2
    nw = bias_wide.shape[1]
    d4, d16 = B_GROUPS[1][1], B_GROUPS[2][1]
    seq_block = lambda off: pl.BlockSpec((SEQ, LANES), lambda p, b: (b, off + p))
    return pl.pallas_call(
        _dilated_kernel,
        out_shape=jax.ShapeDtypeStruct((t, B_HEADS * B_HEAD_DIM), BF16),
        grid=(n_pairs, batch),
        in_specs=[seq_block(0), seq_block(n_pairs), seq_block(2 * n_pairs),
                  pl.BlockSpec((2, nw, QBLK, 2 * QBLK), lambda p, b: (p, 0, 0, 0)),
                  pl.BlockSpec((2, 1, QBLK, QBLK), lambda p, b: (p, 0, 0, 0))],
        out_specs=pl.BlockSpec((SEQ, LANES), lambda p, b: (b, p)),
        scratch_shapes=[pltpu.VMEM((3, SEQ, LANES), F32), pltpu.VMEM((3, SEQ, LANES), F32),
                        pltpu.VMEM((3, d4, SEQ // d4, LANES), BF16), pltpu.VMEM((3, d16, SEQ // d16, LANES), BF16),
                        pltpu.VMEM((2, 3, SEQ, LANES), F32), pltpu.VMEM((3, SEQ, LANES), F32),
                        pltpu.VMEM((3, 3, SEQ, LANES), F32)],
        compiler_params=_cparams(("arbitrary", "arbitrary")),
        name="dilated_attention",
    )(zb, zb, zb, bias_wide, bias_narrow)


def _window_kernel(sink_ref, q_ref, k_ref, v_ref, bias_ref, o_ref):
    pair = pl.program_id(0)
    sinks = (sink_ref[pair] * LOG2E, sink_ref[pair + C_HEADS // 2] * LOG2E)
    n_blocks = SEQ // QBLK

    def keys(i):
        start = min(max((i - 1) * QBLK, 0), SEQ - 3 * QBLK)
        return slice(start, start + 3 * QBLK)

    def scores(i, _):
        var = 0 if i == 0 else 2 if i == n_blocks - 1 else 1
        return _pair_scores(q_ref[i * QBLK:(i + 1) * QBLK, :], k_ref[keys(i), :],
                            (bias_ref[0, var], bias_ref[1, var]))

    def values(i, s):
        return _pair_values(s, v_ref[keys(i), :], sinks)

    def store(i, res):
        _, l_swapped, o = _pair_merge(res)
        o_ref[i * QBLK:(i + 1) * QBLK, :] = (o / pltpu.roll(l_swapped, HALF, axis=1)).astype(BF16)

    _software_pipeline(list(range(n_blocks)), (scores, values, store))


def _window_attention(zc, sink, bias, batch):
    t = zc.shape[0]
    n_pairs = C_HEADS // 2
    return pl.pallas_call(
        _window_kernel,
        out_shape=jax.ShapeDtypeStruct((t, C_HEADS * C_HEAD_DIM), BF16),
        grid=(n_pairs, batch),
        in_specs=[pl.BlockSpec(memory_space=pltpu.SMEM),
                  pl.BlockSpec((SEQ, LANES), lambda p, b: (b, p)),
                  pl.BlockSpec((SEQ, LANES), lambda p, b: (b, n_pairs)),
                  pl.BlockSpec((SEQ, LANES), lambda p, b: (b, n_pairs + 1)),
                  pl.BlockSpec((None, 2, 3, QBLK, 3 * QBLK), lambda p, b: (p, 0, 0, 0, 0))],
        out_specs=pl.BlockSpec((SEQ, LANES), lambda p, b: (b, p)),
        compiler_params=_cparams(("arbitrary", "arbitrary")),
        name="window_attention",
    )(sink, zc, zc, zc, bias)


MERGE_TM = 1024
MERGE_CHAINS = 4


def _merge_kernel(x_ref, g1_ref, ya_ref, yb_ref, yc_ref, wg_ref, wbr_ref, wo_ref, o_ref):
    sub = x_ref.shape[0] // MERGE_CHAINS

    def rows(r):
        return slice(r * sub, (r + 1) * sub)

    def gated_sum(r, _):
        h = _rms(x_ref[rows(r), :], g1_ref[...]).astype(BF16)
        merged = None
        for b, y_ref in enumerate((ya_ref, yb_ref, yc_ref)):
            gate = jax.nn.sigmoid(_dot(h, wg_ref[:, b * D_MODEL:(b + 1) * D_MODEL]))
            term = gate * _dot(y_ref[rows(r), :], wbr_ref[b])
            merged = term if merged is None else merged + term
        return merged.astype(BF16)

    def project(r, merged):
        o_ref[rows(r), :] = x_ref[rows(r), :] + _dot(merged, wo_ref[...])

    _software_pipeline(list(range(MERGE_CHAINS)), (gated_sum, project))


def _merge(x2, g1, ya, yb, yc, wg, wbr, wo):
    t = x2.shape[0]
    tm = MERGE_TM
    full = lambda a: pl.BlockSpec(a.shape, lambda i: (0,) * a.ndim)
    row = lambda w: pl.BlockSpec((tm, w), lambda i: (i, 0))
    return pl.pallas_call(
        _merge_kernel,
        out_shape=jax.ShapeDtypeStruct((t, D_MODEL), F32),
        grid=(t // tm,),
        in_specs=[row(D_MODEL), full(g1), row(ya.shape[1]), row(yb.shape[1]), row(yc.shape[1]),
                  full(wg), full(wbr), full(wo)],
        out_specs=row(D_MODEL),
        compiler_params=_cparams(("parallel",)),
        name="gate_merge",
    )(x2, g1, ya, yb, yc, wg, wbr, wo)


TOP_K = 2
ROW_SUBLANES = D_MODEL // LANES
MOE_TM = 512
ROUTER_TM = 2048
ROUTER_CHAINS = 4
EXPERT_CHAINS = 2


def _router_kernel(x_ref, g_ref, wr_ref, eid_ref, rank_ref, gate_ref, cnt_ref, carry):
    @pl.when(pl.program_id(0) == 0)
    def _():
        carry[...] = jnp.zeros_like(carry)

    sub = x_ref.shape[0] // ROUTER_CHAINS
    w = wr_ref[...]
    w_hi = w.astype(BF16)
    w_lo = (w - w_hi.astype(F32)).astype(BF16)
    earlier = (lax.broadcasted_iota(jnp.int32, (sub, sub), 0) > lax.broadcasted_iota(jnp.int32, (sub, sub), 1))
    earlier = jnp.where(earlier, 1.0, 0.0).astype(BF16)

    def rows(r):
        return slice(r * sub, (r + 1) * sub)

    def logits(r, _):
        h = _rms(x_ref[rows(r), :], g_ref[...])
        h_hi = h.astype(BF16)
        h_lo = (h - h_hi.astype(F32)).astype(BF16)
        return _dot(h_hi, w_hi) + (_dot(h_lo, w_hi) + _dot(h_hi, w_lo))

    def select(r, lg):
        idx = lax.broadcasted_iota(jnp.int32, lg.shape, 1)
        m1 = jnp.max(lg, axis=-1, keepdims=True)
        i1 = jnp.min(jnp.where(lg == m1, idx, N_EXPERTS), axis=-1, keepdims=True)
        first = idx == i1
        rest = jnp.where(first, -jnp.inf, lg)
        m2 = jnp.max(rest, axis=-1, keepdims=True)
        i2 = jnp.min(jnp.where(rest == m2, idx, N_EXPERTS), axis=-1, keepdims=True)
        second = idx == i2
        e = jnp.exp(m2 - m1)
        chosen = jnp.where(jnp.logical_or(first, second), 1.0, 0.0)
        return i1, i2, first, second, e, chosen, _dot(earlier, chosen.astype(BF16))

    def emit(r, state):
        i1, i2, first, second, e, chosen, before = state
        before = before + carry[...]
        r1 = jnp.sum(jnp.where(first, before, 0.0), axis=-1, keepdims=True)
        r2 = jnp.sum(jnp.where(second, before, 0.0), axis=-1, keepdims=True)
        carry[...] += jnp.sum(chosen, axis=0, keepdims=True)
        den = 1.0 + e
        slot0 = lax.broadcasted_iota(jnp.int32, (sub, TOP_K), 1) == 0
        eid_ref[rows(r), :] = jnp.where(slot0, i1, i2)
        rank_ref[rows(r), :] = jnp.where(slot0, r1, r2).astype(jnp.int32)
        gate_ref[rows(r), :] = jnp.where(slot0, 1.0 / den, e / den)

    _software_pipeline(list(range(ROUTER_CHAINS)), (logits, select, emit))
    cnt_ref[...] = carry[...].astype(jnp.int32)


def _router(x2, g, wr):
    t = x2.shape[0]
    tm = ROUTER_TM
    pair = pl.BlockSpec((tm, TOP_K), lambda i: (i, 0))
    return pl.pallas_call(
        _router_kernel,
        out_shape=(jax.ShapeDtypeStruct((t, TOP_K), jnp.int32), jax.ShapeDtypeStruct((t, TOP_K), jnp.int32),
                   jax.ShapeDtypeStruct((t, TOP_K), F32), jax.ShapeDtypeStruct((1, N_EXPERTS), jnp.int32)),
        grid=(t // tm,),
        in_specs=[pl.BlockSpec((tm, D_MODEL), lambda i: (i, 0)),
                  pl.BlockSpec(g.shape, lambda i: (0, 0)),
                  pl.BlockSpec(wr.shape, lambda i: (0, 0))],
        out_specs=(pair, pair, pair, pl.BlockSpec((1, N_EXPERTS), lambda i: (0, 0))),
        scratch_shapes=[pltpu.VMEM((1, N_EXPERTS), F32)],
        compiler_params=_cparams(("arbitrary",)),
        name="router",
    )(x2, g, wr)


def _row_tile(ref, row):
    return ref.at[pl.ds(pl.multiple_of(row * ROW_SUBLANES, ROW_SUBLANES), ROW_SUBLANES), :]


def _dispatch_kernel(tail_start_ref, tail_len_ref, na_ref, dest_ref, x_ref, g_ref, hs_ref, stage, sems):
    i = pl.program_id(0)
    n_steps = pl.num_programs(0)
    tm = x_ref.shape[0]
    slot = i % 2
    rows_per_slot = stage.shape[1]

    def drain(s):
        for _ in range(TOP_K):
            pltpu.make_async_copy(stage.at[s], hs_ref.at[pl.ds(0, rows_per_slot), :], sems.at[s]).wait()

    @pl.when(i >= 2)
    def _():
        drain(slot)

    h = _rms(x_ref[...], g_ref[...])
    for s in range(ROW_SUBLANES):
        stage[slot, pl.ds(s, tm, stride=ROW_SUBLANES), :] = h[:, s * LANES:(s + 1) * LANES]

    def start(t, carry):
        for k in range(TOP_K):
            pltpu.make_async_copy(_row_tile(stage.at[slot], t), _row_tile(hs_ref, dest_ref[TOP_K * t + k]),
                                  sems.at[slot]).start(priority=k)
        return carry

    lax.fori_loop(0, tm, start, 0, unroll=8)

    @pl.when(i == n_steps - 1)
    def _():
        @pl.when(n_steps >= 2)
        def _():
            drain(1 - slot)

        drain(slot)
        zeros = stage.at[0]
        zeros[...] = jnp.zeros_like(zeros)
        for e in range(N_EXPERTS):
            def tail(r, e=e):
                return pltpu.make_async_copy(_row_tile(zeros, 0), _row_tile(hs_ref, tail_start_ref[e] + r), sems.at[0])

            lax.fori_loop(0, tail_len_ref[e], lambda r, c: (tail(r).start(), c)[1], 0)
            lax.fori_loop(0, tail_len_ref[e], lambda r, c: (tail(r).wait(), c)[1], 0)
        n_tiles = hs_ref.shape[0] // rows_per_slot

        def spare(j):
            rows = pl.ds(pl.multiple_of(j * rows_per_slot, rows_per_slot), rows_per_slot)
            return pltpu.make_async_copy(zeros, hs_ref.at[rows, :], sems.at[0])

        lax.fori_loop(na_ref[0], n_tiles, lambda j, c: (spare(j).start(), c)[1], 0)
        lax.fori_loop(na_ref[0], n_tiles, lambda j, c: (spare(j).wait(), c)[1], 0)


def _dispatch(x2, g, dest, tail_start, tail_len, n_active, n_rows):
    t = x2.shape[0]
    tm = MOE_TM
    return pl.pallas_call(
        _dispatch_kernel,
        out_shape=jax.ShapeDtypeStruct((n_rows * ROW_SUBLANES, LANES), F32),
        grid_spec=pltpu.PrefetchScalarGridSpec(
            num_scalar_prefetch=3,
            grid=(t // tm,),
            in_specs=[pl.BlockSpec((TOP_K * tm,), lambda i, ts, tl, na: (i,), memory_space=pltpu.SMEM),
                      pl.BlockSpec((tm, D_MODEL), lambda i, ts, tl, na: (i, 0)),
                      pl.BlockSpec(g.shape, lambda i, ts, tl, na: (0, 0))],
            out_specs=pl.BlockSpec(memory_space=pl.ANY),
            scratch_shapes=[pltpu.VMEM((2, tm * ROW_SUBLANES, LANES), F32), pltpu.SemaphoreType.DMA((2,))]),
        compiler_params=_cparams(("arbitrary",)),
        name="dispatch",
    )(tail_start, tail_len, n_active, dest, x2, g)


def _experts_kernel(te_ref, na_ref, hs_ref, wg_ref, wu_ref, wd_ref, ys_ref):
    active = pl.program_id(0) < na_ref[0]
    sub = hs_ref.shape[0] // ROW_SUBLANES // EXPERT_CHAINS

    def lane_groups(r):
        return [pl.ds(r * sub * ROW_SUBLANES + s, sub, stride=ROW_SUBLANES) for s in range(ROW_SUBLANES)]

    def up(r, _):
        h = jnp.concatenate([hs_ref[g, :].astype(BF16) for g in lane_groups(r)], axis=1)
        return (jax.nn.silu(_dot(h, wg_ref[...])) * _dot(h, wu_ref[...])).astype(BF16)

    def down(r, act):
        return _dot(act, wd_ref[...])

    def store(r, y):
        for s, g in enumerate(lane_groups(r)):
            ys_ref[g, :] = y[:, s * LANES:(s + 1) * LANES]

    @pl.when(active)
    def _():
        _software_pipeline(list(range(EXPERT_CHAINS)), (up, down, store))

    @pl.when(jnp.logical_not(active))
    def _():
        ys_ref[...] = jnp.zeros_like(ys_ref)


def _experts(hs, tile_expert, n_active, wg, wu, wd):
    tm = MOE_TM
    n_tiles = hs.shape[0] // (tm * ROW_SUBLANES)
    d_ff = wg.shape[2]
    once = pl.Buffered(1)
    return pl.pallas_call(
        _experts_kernel,
        out_shape=jax.ShapeDtypeStruct(hs.shape, F32),
        grid_spec=pltpu.PrefetchScalarGridSpec(
            num_scalar_prefetch=2,
            grid=(n_tiles,),
            in_specs=[pl.BlockSpec((tm * ROW_SUBLANES, LANES), lambda i, te, na: (jnp.minimum(i, na[0] - 1), 0)),
                      pl.BlockSpec((None, D_MODEL, d_ff), lambda i, te, na: (te[i], 0, 0), pipeline_mode=once),
                      pl.BlockSpec((None, D_MODEL, d_ff), lambda i, te, na: (te[i], 0, 0), pipeline_mode=once),
                      pl.BlockSpec((None, d_ff, D_MODEL), lambda i, te, na: (te[i], 0, 0), pipeline_mode=once)],
            out_specs=pl.BlockSpec((tm * ROW_SUBLANES, LANES), lambda i, te, na: (i, 0))),
        compiler_params=_cparams(("arbitrary",)),
        name="experts",
    )(tile_expert, n_active, hs, wg, wu, wd)


def _combine_kernel(dest_ref, next_dest_ref, x_ref, gate_ref, gf_ref, ys_ref, o_ref, bufs, sems, *, final):
    i = pl.program_id(0)
    tm = x_ref.shape[0]
    slot = i % 2

    def gather(idx_ref, s):
        def start(t, carry):
            for k in range(TOP_K):
                pltpu.make_async_copy(_row_tile(ys_ref, idx_ref[TOP_K * t + k]), _row_tile(bufs.at[s, k], t),
                                      sems.at[s]).start(priority=k)
            return carry

        lax.fori_loop(0, tm, start, 0, unroll=8)

    @pl.when(i == 0)
    def _():
        gather(dest_ref, slot)

    @pl.when(i + 1 < pl.num_programs(0))
    def _():
        gather(next_dest_ref, 1 - slot)

    for k in range(TOP_K):
        pltpu.make_async_copy(ys_ref.at[pl.ds(0, bufs.shape[2]), :], bufs.at[slot, k], sems.at[slot]).wait()
    gate = gate_ref[...]
    g0 = gate[:, 0:1]
    g1 = gate[:, 1:2]
    ss = jnp.zeros((tm, 1), F32)
    for s in range(ROW_SUBLANES):
        sl = slice(s * LANES, (s + 1) * LANES)
        rows = pl.ds(s, tm, stride=ROW_SUBLANES)
        y = x_ref[:, sl] + (g0 * bufs[slot, 0, rows, :] + g1 * bufs[slot, 1, rows, :])
        o_ref[:, sl] = y
        ss = ss + jnp.sum(y * y, axis=-1, keepdims=True)
    if final:
        o_ref[...] = o_ref[...] * lax.rsqrt(ss / D_MODEL + EPS) * gf_ref[...]


def _combine(x2, dest, gates, gf, ys, *, final):
    t = x2.shape[0]
    tm = MOE_TM
    return pl.pallas_call(
        functools.partial(_combine_kernel, final=final),
        out_shape=jax.ShapeDtypeStruct((t, D_MODEL), F32),
        grid=(t // tm,),
        in_specs=[pl.BlockSpec((TOP_K * tm,), lambda i: (i,), memory_space=pltpu.SMEM),
                  pl.BlockSpec((TOP_K * tm,), lambda i: (jnp.minimum(i + 1, t // tm - 1),),
                               memory_space=pltpu.SMEM),
                  pl.BlockSpec((tm, D_MODEL), lambda i: (i, 0)),
                  pl.BlockSpec((tm, TOP_K), lambda i: (i, 0)),
                  pl.BlockSpec(gf.shape, lambda i: (0, 0)),
                  pl.BlockSpec(memory_space=pl.ANY)],
        out_specs=pl.BlockSpec((tm, D_MODEL), lambda i: (i, 0)),
        scratch_shapes=[pltpu.VMEM((2, TOP_K, tm * ROW_SUBLANES, LANES), F32), pltpu.SemaphoreType.DMA((2,))],
        compiler_params=_cparams(("arbitrary",)),
        name="combine",
    )(dest, dest, x2, gates, gf, ys)


def _moe(x2, g, wr, wg, wu, wd, gf, *, final):
    t = x2.shape[0]
    tm = MOE_TM
    eid, rank, gates, counts = _router(x2, g, wr)
    counts = counts[0]
    padded = (counts + tm - 1) // tm * tm
    group_end = jnp.cumsum(padded)
    group_start = group_end - padded
    dest = (group_start[eid] + rank).reshape(-1)
    n_tiles = TOP_K * t // tm + N_EXPERTS
    n_active = (group_end[-1:] // tm).astype(jnp.int32)
    tile_row = jnp.arange(n_tiles, dtype=jnp.int32) * tm
    tile_row = jnp.minimum(tile_row, group_end[-1] - tm)
    tile_expert = jnp.sum(tile_row[:, None] >= group_end[None, :], axis=1).astype(jnp.int32)
    hs = _dispatch(x2, g, dest, group_start + counts, padded - counts, n_active, n_tiles * tm)
    ys = _experts(hs, tile_expert, n_active, wg, wu, wd)
    return _combine(x2, dest, gates, gf, ys, final=final)


FFN_TM = 1024
FFN_CHAINS = 4


def _ffn_kernel(x_ref, g_ref, wg_ref, wu_ref, wd_ref, gf_ref, o_ref, *, final):
    sub = x_ref.shape[0] // FFN_CHAINS

    def rows(r):
        return slice(r * sub, (r + 1) * sub)

    def up(r, _):
        h = _rms(x_ref[rows(r), :], g_ref[...]).astype(BF16)
        return (jax.nn.silu(_dot(h, wg_ref[...])) * _dot(h, wu_ref[...])).astype(BF16)

    def down(r, act):
        y = x_ref[rows(r), :] + _dot(act, wd_ref[...])
        if final:
            y = _rms(y, gf_ref[...])
        o_ref[rows(r), :] = y

    _software_pipeline(list(range(FFN_CHAINS)), (up, down))


def _ffn(x2, g, wg, wu, wd, gf, *, final):
    t = x2.shape[0]
    tm = FFN_TM
    full = lambda a: pl.BlockSpec(a.shape, lambda i: (0,) * a.ndim)
    return pl.pallas_call(
        functools.partial(_ffn_kernel, final=final),
        out_shape=jax.ShapeDtypeStruct((t, D_MODEL), F32),
        grid=(t // tm,),
        in_specs=[pl.BlockSpec((tm, D_MODEL), lambda i: (i, 0)), full(g), full(wg), full(wu), full(wd), full(gf)],
        out_specs=pl.BlockSpec((tm, D_MODEL), lambda i: (i, 0)),
        compiler_params=_cparams(("parallel",)),
        name="swiglu",
    )(x2, g, wg, wu, wd, gf)


def _rot_cols(w):
    half = A_ROPE // 2
    return jnp.concatenate([-w[..., half:], w[..., :half]], axis=-1)


def _prep_layer(w_in, w_uq, w_ukv, w_br_a, w_br_b, w_br_c):
    o = 0
    w_cq = w_in[:, o:o + A_Q_LORA]; o += A_Q_LORA
    w_ckv = w_in[:, o:o + A_KV_LORA]; o += A_KV_LORA
    w_kpe = w_in[:, o:o + A_ROPE]; o += A_ROPE
    nb = B_HEADS * B_HEAD_DIM
    w_b = w_in[:, o:o + 3 * nb]; o += 3 * nb
    nq = C_HEADS * C_HEAD_DIM
    nkv = C_KV_HEADS * C_HEAD_DIM
    w_c = w_in[:, o:o + nq + 2 * nkv]; o += nq + 2 * nkv
    w_g = w_in[:, o:]
    pad = jnp.zeros((D_MODEL, LANES - 2 * A_ROPE), F32)
    wa = jnp.concatenate([w_cq, w_ckv, w_kpe, _rot_cols(w_kpe), pad], axis=1)
    wb = jnp.concatenate([w_b[:, :nb] * (B_HEAD_DIM ** -0.5), w_b[:, nb:]], axis=1)
    q_c = w_c[:, :nq].reshape(D_MODEL, 2, C_HEADS // 2, C_HEAD_DIM).transpose(0, 2, 1, 3).reshape(D_MODEL, nq)
    wc = jnp.concatenate([q_c * (C_HEAD_DIM ** -0.5), w_c[:, nq:]], axis=1)
    uq = w_uq.reshape(A_Q_LORA, A_HEADS, A_NOPE + A_ROPE)
    z32 = jnp.zeros((A_Q_LORA, A_HEADS, LANES - A_NOPE - A_ROPE), F32)
    wq = jnp.concatenate([uq, z32], axis=-1).reshape(A_Q_LORA, A_HEADS * LANES)
    wqr = jnp.concatenate([jnp.zeros((A_Q_LORA, A_HEADS, A_NOPE), F32), _rot_cols(uq[..., A_NOPE:]), z32],
                          axis=-1).reshape(A_Q_LORA, A_HEADS * LANES)
    ukv = w_ukv.reshape(A_KV_LORA, A_HEADS, A_NOPE + A_V)
    wk = jnp.concatenate([ukv[..., :A_NOPE], jnp.zeros((A_KV_LORA, A_HEADS, LANES - A_NOPE), F32)],
                         axis=-1).reshape(A_KV_LORA, A_HEADS * LANES)
    uv = ukv[..., A_NOPE:].reshape(A_KV_LORA, A_HEADS // 2, 2, A_V)
    zv = jnp.zeros_like(uv[:, :, 0])
    wv = jnp.stack([jnp.concatenate([uv[:, :, 0], zv], axis=-1), jnp.concatenate([zv, uv[:, :, 1]], axis=-1)],
                   axis=2).reshape(A_KV_LORA, A_HEADS * LANES)
    w_br_c = w_br_c.reshape(2, C_HEADS // 2, C_HEAD_DIM, D_MODEL).transpose(1, 0, 2, 3).reshape(nq, D_MODEL)
    wbr = jnp.stack([w_br_a, w_br_b, w_br_c])
    wkv = jnp.concatenate([jnp.concatenate([wk, wv], axis=1),
                           jnp.concatenate([_rope_placement(), jnp.zeros((LANES, A_HEADS * LANES), F32)], axis=1)])
    cast = lambda a: a.astype(BF16)
    return dict(wa=cast(wa), wb=cast(wb), wc=cast(wc), wg=cast(w_g), wq=cast(wq), wqr=cast(wqr), wkv=cast(wkv),
                wbr=cast(wbr))


def _rope_tables():
    half = A_ROPE // 2
    inv = ROPE_THETA ** (-jnp.arange(half, dtype=F32) / half)
    ang = jnp.arange(SEQ, dtype=F32)[:, None] * inv[None, :]
    cos2 = jnp.tile(jnp.cos(ang), (1, 2))
    sin2 = jnp.tile(jnp.sin(ang), (1, 2))
    ones = jnp.ones((SEQ, A_NOPE), F32)
    z = lambda w: jnp.zeros((SEQ, w), F32)
    cq = jnp.concatenate([ones, cos2, z(LANES - A_NOPE - A_ROPE)], axis=1)
    sq = jnp.concatenate([z(A_NOPE), sin2, z(LANES - A_NOPE - A_ROPE)], axis=1)
    ck = jnp.concatenate([cos2, z(LANES - A_ROPE)], axis=1)
    sk = jnp.concatenate([sin2, z(LANES - A_ROPE)], axis=1)
    return cq, sq, ck, sk


def _rope_placement():
    place = np.zeros((LANES, A_HEADS * LANES), np.float32)
    for h in range(A_HEADS):
        for j in range(A_ROPE):
            place[j, h * LANES + A_NOPE + j] = 1.0
    return jnp.asarray(place)


def kernel(x, norm1_g, w_in, q_norm_g, w_uq, kv_norm_g, w_ukv, sink_logit, rel_bias, w_branch_a, w_branch_b,
           w_branch_c, w_out, norm2_g, ffn_w_gate, ffn_w_up, ffn_w_down, router_w, exp_w_gate, exp_w_up,
           exp_w_down, final_g):
    batch, seq, d = x.shape
    assert seq == SEQ and d == D_MODEL
    t = batch * seq
    x2 = x.reshape(t, d)

    wide_idx, narrow_idx = _dilated_index_tables()
    bias_wide = _bias_tables(rel_bias, wide_idx, 0, B_HEADS)
    bias_narrow = _bias_tables(rel_bias, narrow_idx, 0, B_HEADS)
    bias_c = _bias_tables(rel_bias, _window_index_tables(), B_HEADS, C_HEADS)
    bias_c = bias_c.reshape(2, C_HEADS // 2, *bias_c.shape[1:]).transpose(1, 0, 2, 3, 4)

    cq, sq, ck, sk = _rope_tables()
    row = lambda v: v.reshape(1, -1)
    q_then = lambda nq, rest: jnp.asarray(np.concatenate([np.full(nq, LOG2E), np.ones(rest)])[None], F32)
    bscale = q_then(B_HEADS * B_HEAD_DIM, 2 * B_HEADS * B_HEAD_DIM)
    cscale = q_then(C_HEADS * C_HEAD_DIM, 2 * C_KV_HEADS * C_HEAD_DIM)
    vones = jnp.asarray(np.tile(np.repeat([0.0, 1.0, 1.0, 0.0], A_V), A_HEADS // 2)[None], F32)

    for l in range(DEPTH):
        w = _prep_layer(w_in[l], w_uq[l], w_ukv[l], w_branch_a[l], w_branch_b[l], w_branch_c[l])
        g1 = row(norm1_g[l])
        qa, ka, va, zb, zc = _in_proj(x2, g1, w["wa"], w["wb"], w["wc"], row(q_norm_g[l]), row(kv_norm_g[l]),
                                      w["wq"], w["wqr"], w["wkv"], bscale, cscale, vones, cq, sq, ck, sk)
        ya = _mla_attention(qa, ka, va, batch)
        yb = _dilated_attention(zb, bias_wide, bias_narrow, batch)
        yc = _window_attention(zc, sink_logit[l], bias_c, batch)
        x2 = _merge(x2, g1, ya, yb, yc, w["wg"], w["wbr"], w_out[l].astype(BF16))
        g2 = row(norm2_g[l])
        final = l == DEPTH - 1
        i = l // 2
        if l % 2 == 0:
            x2 = _ffn(x2, g2, ffn_w_gate[i].astype(BF16), ffn_w_up[i].astype(BF16), ffn_w_down[i].astype(BF16),
                      row(final_g), final=final)
        else:
            x2 = _moe(x2, g2, router_w[i], exp_w_gate[i].astype(BF16), exp_w_up[i].astype(BF16),
                      exp_w_down[i].astype(BF16), row(final_g), final=final)
    return x2.reshape(batch, seq, d)
```

```python
import functools
import math

import numpy as np
import jax
import jax.numpy as jnp
from jax import lax
from jax.experimental import pallas as pl
from jax.experimental.pallas import tpu as pltpu

F32 = jnp.float32
BF16 = jnp.bfloat16

D_MODEL = 1024
SEQ = 2048
DEPTH = 2
A_HEADS = 8
A_NOPE = 64
A_ROPE = 32
A_V = 64
A_Q_LORA = 256
A_KV_LORA = 128
ROPE_THETA = 10000.0
B_HEADS = 8
B_HEAD_DIM = 64
B_GROUPS = ((128, 1), (512, 4), (2048, 16))
C_HEADS = 8
C_KV_HEADS = 2
C_HEAD_DIM = 64
C_RADIUS = 128
N_BUCKETS = 32
MAX_DISTANCE = 1024
D_FF_DENSE = 2816
N_EXPERTS = 8
D_FF_EXPERT = 3584
EPS = 1e-6
NEG_INF = -1e30

LANES = 128
HALF = 64
QBLK = 128
B_RADIUS = 64
VMEM_LIMIT = 56 * 1024 * 1024

A_SCALE = (A_NOPE + A_ROPE) ** -0.5
LOG2E = math.log2(math.e)


def _cparams(sem):
    return pltpu.CompilerParams(dimension_semantics=sem, vmem_limit_bytes=VMEM_LIMIT)


def _rms(x, g):
    return x * lax.rsqrt(jnp.mean(x * x, axis=-1, keepdims=True) + EPS) * g


def _dot(a, b):
    return jnp.dot(a, b, preferred_element_type=F32)


def _dot_nt(a, b):
    return lax.dot_general(a, b, (((1,), (1,)), ((), ())), preferred_element_type=F32)


def _t5_bucket_np(rel):
    nb = N_BUCKETS // 2
    max_exact = nb // 2
    ret = np.where(rel > 0, nb, 0)
    n = np.abs(rel)
    nf = np.maximum(n, 1).astype(np.float64)
    t = np.log(nf / max_exact) / math.log(MAX_DISTANCE / max_exact) * (nb - max_exact)
    frac = np.abs(t - np.round(t))
    assert np.all((frac > 1e-5) | (n <= max_exact) | (n >= MAX_DISTANCE))
    large = max_exact + np.floor(np.round(t, 9)).astype(np.int64)
    large = np.minimum(large, nb - 1)
    return (ret + np.where(n < max_exact, n, large)).astype(np.int32)


def _band_index(tq, tk, shift, radius, dil):
    rel = np.arange(tk)[None, :] - np.arange(tq)[:, None] + shift
    idx = _t5_bucket_np(rel * dil)
    return np.where(np.abs(rel) <= radius, idx, -1).astype(np.int32)


def _dilated_index_tables():
    wide = []
    for window, dil in B_GROUPS[:2]:
        for shift in (0, -B_RADIUS, -2 * B_RADIUS):
            wide.append(_band_index(QBLK, 2 * QBLK, shift, B_RADIUS, dil))
    narrow = [_band_index(QBLK, QBLK, 0, B_RADIUS, B_GROUPS[2][1])]
    return np.stack(wide), np.stack(narrow)


def _window_index_tables():
    return np.stack([_band_index(QBLK, 3 * QBLK, shift, C_RADIUS, 1) for shift in (0, -QBLK, -2 * QBLK)])


def _bias_kernel(tab_ref, idx_ref, o_ref, *, col0):
    col = pl.program_id(0) + col0
    idx = idx_ref[...]
    acc = jnp.full(idx.shape, NEG_INF, F32)
    for b in range(N_BUCKETS):
        acc = jnp.where(idx == b, tab_ref[b, col] * LOG2E, acc)
    o_ref[...] = acc


def _bias_tables(rel_bias, idx_np, col0, n_heads):
    nv, r, c = idx_np.shape
    return pl.pallas_call(
        functools.partial(_bias_kernel, col0=col0),
        out_shape=jax.ShapeDtypeStruct((n_heads, nv, r, c), F32),
        grid=(n_heads, nv),
        in_specs=[pl.BlockSpec(memory_space=pltpu.SMEM),
                  pl.BlockSpec((None, r, c), lambda h, v: (v, 0, 0))],
        out_specs=pl.BlockSpec((None, None, r, c), lambda h, v: (h, v, 0, 0)),
        compiler_params=_cparams(("arbitrary", "arbitrary")),
        name="bias_tables",
    )(rel_bias, jnp.asarray(idx_np))


IN_TM = 1024
IN_CHAINS = 4


def _in_proj_kernel(x_ref, g1_ref, wa_ref, wb_ref, wc_ref, qg_ref, kvg_ref, wq_ref, wqr_ref, wkv_ref,
                    bscale_ref, cscale_ref, vones_ref, cq_ref, sq_ref, ck_ref, sk_ref,
                    qa_ref, ka_ref, va_ref, zb_ref, zc_ref):
    sub = x_ref.shape[0] // IN_CHAINS

    def rows(r):
        return slice(r * sub, (r + 1) * sub)

    def project(r, _):
        h = _rms(x_ref[rows(r), :], g1_ref[...]).astype(BF16)
        za = _dot(h, wa_ref[...])
        zb_ref[rows(r), :] = (_dot(h, wb_ref[...]) * bscale_ref[...]).astype(BF16)
        zc_ref[rows(r), :] = (_dot(h, wc_ref[...]) * cscale_ref[...]).astype(BF16)
        return za

    def expand(r, za):
        hq = _rms(za[:, :A_Q_LORA], qg_ref[...]).astype(BF16)
        hkv = _rms(za[:, A_Q_LORA:A_Q_LORA + A_KV_LORA], kvg_ref[...]).astype(BF16)
        kpe = za[:, A_Q_LORA + A_KV_LORA:]
        kpe_rot = pltpu.roll(kpe, LANES - A_ROPE, axis=1)
        k_rope = (kpe * ck_ref[rows(r), :] + kpe_rot * sk_ref[rows(r), :]).astype(BF16)
        q = _dot(hq, wq_ref[...]) * (A_SCALE * LOG2E)
        q_rot = _dot(hq, wqr_ref[...]) * (A_SCALE * LOG2E)
        kv = _dot(jnp.concatenate([hkv, k_rope], axis=1), wkv_ref[...])
        return q, q_rot, kv

    def store(r, state):
        q, q_rot, kv = state
        ka_ref[rows(r), :] = kv[:, :A_HEADS * LANES].astype(BF16)
        va_ref[rows(r), :] = (kv[:, A_HEADS * LANES:] + vones_ref[...]).astype(BF16)
        cq = cq_ref[rows(r), :]
        sq = sq_ref[rows(r), :]
        for hh in range(A_HEADS):
            sl = slice(hh * LANES, (hh + 1) * LANES)
            qa_ref[rows(r), sl] = (q[:, sl] * cq + q_rot[:, sl] * sq).astype(BF16)

    _software_pipeline(list(range(IN_CHAINS)), (project, expand, store))


def _in_proj(x2, g1, wa, wb, wc, qg, kvg, wq, wqr, wkv, bscale, cscale, vones, cq, sq, ck, sk):
    t = x2.shape[0]
    tm = IN_TM
    n_pos = SEQ // tm
    full = lambda a: pl.BlockSpec(a.shape, lambda i: (0,) * a.ndim)
    row = lambda w: pl.BlockSpec((tm, w), lambda i: (i, 0))
    pos = pl.BlockSpec((tm, LANES), lambda i: (i % n_pos, 0))
    return pl.pallas_call(
        _in_proj_kernel,
        out_shape=(jax.ShapeDtypeStruct((t, A_HEADS * LANES), BF16),
                   jax.ShapeDtypeStruct((t, A_HEADS * LANES), BF16),
                   jax.ShapeDtypeStruct((t, A_HEADS * LANES), BF16),
                   jax.ShapeDtypeStruct((t, wb.shape[1]), BF16),
                   jax.ShapeDtypeStruct((t, wc.shape[1]), BF16)),
        grid=(t // tm,),
        in_specs=[row(D_MODEL), full(g1), full(wa), full(wb), full(wc), full(qg), full(kvg), full(wq), full(wqr),
                  full(wkv), full(bscale), full(cscale), full(vones), pos, pos, pos, pos],
        out_specs=(row(A_HEADS * LANES), row(A_HEADS * LANES), row(A_HEADS * LANES), row(wb.shape[1]),
                   row(wc.shape[1])),
        compiler_params=_cparams(("parallel",)),
        name="in_proj",
    )(x2, g1, wa, wb, wc, qg, kvg, wq, wqr, wkv, bscale, cscale, vones, cq, sq, ck, sk)


A_TQ = 2048
A_SUB = 512


def _mla_kernel(q_ref, k_ref, v_ref, o_ref):
    first = lax.broadcasted_iota(jnp.int32, (1, LANES), 1) < HALF
    tasks = [(r, hh) for r in range(A_TQ // A_SUB) for hh in range(2)]
    done = {}

    def rows(r):
        return slice(r * A_SUB, (r + 1) * A_SUB)

    def lanes(hh):
        return slice(hh * LANES, (hh + 1) * LANES)

    def scores(task, _):
        r, hh = task
        return _dot_nt(q_ref[rows(r), lanes(hh)], k_ref[:, lanes(hh)])

    def values(task, s):
        p = jnp.exp2(s - jnp.max(s, axis=-1, keepdims=True))
        return _dot(p.astype(BF16), v_ref[:, lanes(task[1])])

    def store(task, a):
        r, hh = task
        done[task] = a
        if hh == 1:
            a0 = done.pop((r, 0))
            a1 = done.pop((r, 1))
            l = pltpu.roll(jnp.where(first, a1, a0), HALF, axis=1)
            o_ref[rows(r), :] = (jnp.where(first, a0, a1) / l).astype(BF16)

    _software_pipeline(tasks, (scores, values, store))


def _mla_attention(qa, ka, va, batch):
    t = qa.shape[0]
    nq = SEQ // A_TQ
    return pl.pallas_call(
        _mla_kernel,
        out_shape=jax.ShapeDtypeStruct((t, A_HEADS * A_V), BF16),
        grid=(batch, A_HEADS // 2, nq),
        in_specs=[pl.BlockSpec((A_TQ, 2 * LANES), lambda b, p, i: (b * nq + i, p)),
                  pl.BlockSpec((SEQ, 2 * LANES), lambda b, p, i: (b, p)),
                  pl.BlockSpec((SEQ, 2 * LANES), lambda b, p, i: (b, p))],
        out_specs=pl.BlockSpec((A_TQ, LANES), lambda b, p, i: (b * nq + i, p)),
        compiler_params=_cparams(("parallel", "parallel", "arbitrary")),
        name="mla_attention",
    )(qa, ka, va)


def _software_pipeline(tasks, stages):
    state = [None] * len(tasks)
    for step in range(len(tasks) + len(stages) - 1):
        for s, stage in enumerate(stages):
            i = step - s
            if 0 <= i < len(tasks):
                state[i] = stage(tasks[i], state[i])


def _first_half():
    return lax.broadcasted_iota(jnp.int32, (1, LANES), 1) < HALF


def _pair_scores(q2, k2, biases):
    first = _first_half()
    zero = jnp.zeros_like(q2)
    tq = q2.shape[0]
    stacked = jnp.concatenate([jnp.where(first, q2, zero), jnp.where(first, zero, q2)], axis=0)
    s = _dot_nt(stacked, k2)
    return [s[:tq] + biases[0], s[tq:] + biases[1]]


def _pair_values(scores, v2, sinks=None):
    first = _first_half()
    one = jnp.ones_like(v2)
    res = []
    for hh, s in enumerate(scores):
        own = first if hh == 0 else jnp.logical_not(first)
        m = jnp.max(s, axis=-1, keepdims=True)
        if sinks is not None:
            m = jnp.maximum(m, sinks[hh])
        a = _dot(jnp.exp2(s - m).astype(BF16), jnp.where(own, v2, one))
        if sinks is not None:
            a = a + jnp.where(own, 0.0, jnp.exp2(sinks[hh] - m))
        res.append((m, a))
    return res


def _pair_merge(res):
    first = _first_half()
    (m0, a0), (m1, a1) = res
    return jnp.where(first, m0, m1), jnp.where(first, a1, a0), jnp.where(first, a0, a1)


def _dilated_kernel(q1, k1, v1, bw_ref, bn_ref, o_ref, nat32, by4_32, by4, by16, stream, by4_out, merged):
    n_blocks = SEQ // QBLK
    d4, d16 = B_GROUPS[1][1], B_GROUPS[2][1]
    len4, len16 = SEQ // d4, SEQ // d16
    for t, ref in enumerate((q1, k1, v1)):
        nat32[t] = ref[...].astype(F32)
        for g in range(d4):
            run = nat32[t, pl.ds(g, len4, stride=d4), :]
            by4_32[t, g * len4:(g + 1) * len4, :] = run
            by4[t, g] = run.astype(BF16)
        for g in range(d16):
            start = (g % d4) * len4 + g // d4
            by16[t, g] = by4_32[t, pl.ds(start, len16, stride=d4), :].astype(BF16)
    sources = ((q1, k1, v1), tuple(by4.at[t] for t in range(3)), tuple(by16.at[t] for t in range(3)))

    def rows(j):
        return slice(j * QBLK, (j + 1) * QBLK)

    def view(ref, gi, sl):
        dil = B_GROUPS[gi][1]
        if dil == 1:
            return ref[sl, :]
        length = SEQ // dil
        g = sl.start // length
        return ref[g, sl.start - g * length:sl.stop - g * length, :]

    def keys(gi, j):
        length = SEQ // B_GROUPS[gi][1]
        per_stream = length // QBLK
        if per_stream == 1:
            return rows(j)
        i = j % per_stream
        start = (j - i) * QBLK + min(max(i * QBLK - B_RADIUS, 0), length - 2 * QBLK)
        return slice(start, start + 2 * QBLK)

    def scores(task, _):
        gi, j = task
        per_stream = SEQ // B_GROUPS[gi][1] // QBLK
        if per_stream == 1:
            biases = (bn_ref[0, 0], bn_ref[1, 0])
        else:
            i = j % per_stream
            var = gi * 3 + (0 if i == 0 else 2 if i == per_stream - 1 else 1)
            biases = (bw_ref[0, var], bw_ref[1, var])
        q_ref, k_ref, _ = sources[gi]
        return _pair_scores(view(q_ref, gi, rows(j)), view(k_ref, gi, keys(gi, j)), biases)

    def values(task, s):
        gi, j = task
        return _pair_values(s, view(sources[gi][2], gi, keys(gi, j)))

    def store(task, res):
        gi, j = task
        for slot, value in enumerate(_pair_merge(res)):
            if gi == 0:
                merged[0, slot, rows(j), :] = value
            else:
                stream[gi - 1, slot, rows(j), :] = value

    _software_pipeline([(gi, j) for gi in range(len(B_GROUPS)) for j in range(n_blocks)], (scores, values, store))
    for slot in range(3):
        for g in range(d16):
            start = (g % d4) * len4 + g // d4
            by4_out[slot, pl.ds(start, len16, stride=d4), :] = stream[1, slot, g * len16:(g + 1) * len16, :]
        for g in range(d4):
            run = slice(g * len4, (g + 1) * len4)
            merged[1, slot, pl.ds(g, len4, stride=d4), :] = stream[0, slot, run, :]
            merged[2, slot, pl.ds(g, len4, stride=d4), :] = by4_out[slot, run, :]

    def merge(c, carry):
        rows = pl.ds(pl.multiple_of(c * QBLK, QBLK), QBLK)
        m = [merged[gi, 0, rows, :] for gi in range(3)]
        top = jnp.maximum(jnp.maximum(m[0], m[1]), m[2])
        w = [jnp.exp2(mg - top) for mg in m]
        l = [pltpu.roll(merged[gi, 1, rows, :], HALF, axis=1) for gi in range(3)]
        num = w[0] * merged[0, 2, rows, :] + w[1] * merged[1, 2, rows, :] + w[2] * merged[2, 2, rows, :]
        den = w[0] * l[0] + w[1] * l[1] + w[2] * l[2]
        o_ref[rows, :] = (num / den).astype(BF16)
        return carry

    lax.fori_loop(0, n_blocks, merge, 0, unroll=4)


def _dilated_attention(zb, bias_wide, bias_narrow, batch):
    t = zb.shape[0]
    n_pairs = B_HEADS // 2
    nw = bias_wide.shape[1]
    d4, d16 = B_GROUPS[1][1], B_GROUPS[2][1]
    seq_block = lambda off: pl.BlockSpec((SEQ, LANES), lambda p, b: (b, off + p))
    return pl.pallas_call(
        _dilated_kernel,
        out_shape=jax.ShapeDtypeStruct((t, B_HEADS * B_HEAD_DIM), BF16),
        grid=(n_pairs, batch),
        in_specs=[seq_block(0), seq_block(n_pairs), seq_block(2 * n_pairs),
                  pl.BlockSpec((2, nw, QBLK, 2 * QBLK), lambda p, b: (p, 0, 0, 0)),
                  pl.BlockSpec((2, 1, QBLK, QBLK), lambda p, b: (p, 0, 0, 0))],
        out_specs=pl.BlockSpec((SEQ, LANES), lambda p, b: (b, p)),
        scratch_shapes=[pltpu.VMEM((3, SEQ, LANES), F32), pltpu.VMEM((3, SEQ, LANES), F32),
                        pltpu.VMEM((3, d4, SEQ // d4, LANES), BF16), pltpu.VMEM((3, d16, SEQ // d16, LANES), BF16),
                        pltpu.VMEM((2, 3, SEQ, LANES), F32), pltpu.VMEM((3, SEQ, LANES), F32),
                        pltpu.VMEM((3, 3, SEQ, LANES), F32)],
        compiler_params=_cparams(("arbitrary", "arbitrary")),
        name="dilated_attention",
    )(zb, zb, zb, bias_wide, bias_narrow)


def _window_kernel(sink_ref, q_ref, k_ref, v_ref, bias_ref, o_ref):
    pair = pl.program_id(0)
    sinks = (sink_ref[pair] * LOG2E, sink_ref[pair + C_HEADS // 2] * LOG2E)
    n_blocks = SEQ // QBLK

    def keys(i):
        start = min(max((i - 1) * QBLK, 0), SEQ - 3 * QBLK)
        return slice(start, start + 3 * QBLK)

    def scores(i, _):
        var = 0 if i == 0 else 2 if i == n_blocks - 1 else 1
        return _pair_scores(q_ref[i * QBLK:(i + 1) * QBLK, :], k_ref[keys(i), :],
                            (bias_ref[0, var], bias_ref[1, var]))

    def values(i, s):
        return _pair_values(s, v_ref[keys(i), :], sinks)

    def store(i, res):
        _, l_swapped, o = _pair_merge(res)
        o_ref[i * QBLK:(i + 1) * QBLK, :] = (o / pltpu.roll(l_swapped, HALF, axis=1)).astype(BF16)

    _software_pipeline(list(range(n_blocks)), (scores, values, store))


def _window_attention(zc, sink, bias, batch):
    t = zc.shape[0]
    n_pairs = C_HEADS // 2
    return pl.pallas_call(
        _window_kernel,
        out_shape=jax.ShapeDtypeStruct((t, C_HEADS * C_HEAD_DIM), BF16),
        grid=(n_pairs, batch),
        in_specs=[pl.BlockSpec(memory_space=pltpu.SMEM),
                  pl.BlockSpec((SEQ, LANES), lambda p, b: (b, p)),
                  pl.BlockSpec((SEQ, LANES), lambda p, b: (b, n_pairs)),
                  pl.BlockSpec((SEQ, LANES), lambda p, b: (b, n_pairs + 1)),
                  pl.BlockSpec((None, 2, 3, QBLK, 3 * QBLK), lambda p, b: (p, 0, 0, 0, 0))],
        out_specs=pl.BlockSpec((SEQ, LANES), lambda p, b: (b, p)),
        compiler_params=_cparams(("arbitrary", "arbitrary")),
        name="window_attention",
    )(sink, zc, zc, zc, bias)


MERGE_TM = 1024
MERGE_CHAINS = 4


def _merge_kernel(x_ref, g1_ref, ya_ref, yb_ref, yc_ref, wg_ref, wbr_ref, wo_ref, o_ref):
    sub = x_ref.shape[0] // MERGE_CHAINS

    def rows(r):
        return slice(r * sub, (r + 1) * sub)

    def gated_sum(r, _):
        h = _rms(x_ref[rows(r), :], g1_ref[...]).astype(BF16)
        merged = None
        for b, y_ref in enumerate((ya_ref, yb_ref, yc_ref)):
            gate = jax.nn.sigmoid(_dot(h, wg_ref[:, b * D_MODEL:(b + 1) * D_MODEL]))
            term = gate * _dot(y_ref[rows(r), :], wbr_ref[b])
            merged = term if merged is None else merged + term
        return merged.astype(BF16)

    def project(r, merged):
        o_ref[rows(r), :] = x_ref[rows(r), :] + _dot(merged, wo_ref[...])

    _software_pipeline(list(range(MERGE_CHAINS)), (gated_sum, project))


def _merge(x2, g1, ya, yb, yc, wg, wbr, wo):
    t = x2.shape[0]
    tm = MERGE_TM
    full = lambda a: pl.BlockSpec(a.shape, lambda i: (0,) * a.ndim)
    row = lambda w: pl.BlockSpec((tm, w), lambda i: (i, 0))
    return pl.pallas_call(
        _merge_kernel,
        out_shape=jax.ShapeDtypeStruct((t, D_MODEL), F32),
        grid=(t // tm,),
        in_specs=[row(D_MODEL), full(g1), row(ya.shape[1]), row(yb.shape[1]), row(yc.shape[1]),
                  full(wg), full(wbr), full(wo)],
        out_specs=row(D_MODEL),
        compiler_params=_cparams(("parallel",)),
        name="gate_merge",
    )(x2, g1, ya, yb, yc, wg, wbr, wo)


TOP_K = 2
ROW_SUBLANES = D_MODEL // LANES
MOE_TM = 512
ROUTER_TM = 2048
ROUTER_CHAINS = 4
EXPERT_CHAINS = 2


def _router_kernel(x_ref, g_ref, wr_ref, eid_ref, rank_ref, gate_ref, cnt_ref, carry):
    @pl.when(pl.program_id(0) == 0)
    def _():
        carry[...] = jnp.zeros_like(carry)

    sub = x_ref.shape[0] // ROUTER_CHAINS
    w = wr_ref[...]
    w_hi = w.astype(BF16)
    w_lo = (w - w_hi.astype(F32)).astype(BF16)
    earlier = (lax.broadcasted_iota(jnp.int32, (sub, sub), 0) > lax.broadcasted_iota(jnp.int32, (sub, sub), 1))
    earlier = jnp.where(earlier, 1.0, 0.0).astype(BF16)

    def rows(r):
        return slice(r * sub, (r + 1) * sub)

    def logits(r, _):
        h = _rms(x_ref[rows(r), :], g_ref[...])
        h_hi = h.astype(BF16)
        h_lo = (h - h_hi.astype(F32)).astype(BF16)
        return _dot(h_hi, w_hi) + (_dot(h_lo, w_hi) + _dot(h_hi, w_lo))

    def select(r, lg):
        idx = lax.broadcasted_iota(jnp.int32, lg.shape, 1)
        m1 = jnp.max(lg, axis=-1, keepdims=True)
        i1 = jnp.min(jnp.where(lg == m1, idx, N_EXPERTS), axis=-1, keepdims=True)
        first = idx == i1
        rest = jnp.where(first, -jnp.inf, lg)
        m2 = jnp.max(rest, axis=-1, keepdims=True)
        i2 = jnp.min(jnp.where(rest == m2, idx, N_EXPERTS), axis=-1, keepdims=True)
        second = idx == i2
        e = jnp.exp(m2 - m1)
        chosen = jnp.where(jnp.logical_or(first, second), 1.0, 0.0)
        return i1, i2, first, second, e, chosen, _dot(earlier, chosen.astype(BF16))

    def emit(r, state):
        i1, i2, first, second, e, chosen, before = state
        before = before + carry[...]
        r1 = jnp.sum(jnp.where(first, before, 0.0), axis=-1, keepdims=True)
        r2 = jnp.sum(jnp.where(second, before, 0.0), axis=-1, keepdims=True)
        carry[...] += jnp.sum(chosen, axis=0, keepdims=True)
        den = 1.0 + e
        slot0 = lax.broadcasted_iota(jnp.int32, (sub, TOP_K), 1) == 0
        eid_ref[rows(r), :] = jnp.where(slot0, i1, i2)
        rank_ref[rows(r), :] = jnp.where(slot0, r1, r2).astype(jnp.int32)
        gate_ref[rows(r), :] = jnp.where(slot0, 1.0 / den, e / den)

    _software_pipeline(list(range(ROUTER_CHAINS)), (logits, select, emit))
    cnt_ref[...] = carry[...].astype(jnp.int32)


def _router(x2, g, wr):
    t = x2.shape[0]
    tm = ROUTER_TM
    pair = pl.BlockSpec((tm, TOP_K), lambda i: (i, 0))
    return pl.pallas_call(
        _router_kernel,
        out_shape=(jax.ShapeDtypeStruct((t, TOP_K), jnp.int32), jax.ShapeDtypeStruct((t, TOP_K), jnp.int32),
                   jax.ShapeDtypeStruct((t, TOP_K), F32), jax.ShapeDtypeStruct((1, N_EXPERTS), jnp.int32)),
        grid=(t // tm,),
        in_specs=[pl.BlockSpec((tm, D_MODEL), lambda i: (i, 0)),
                  pl.BlockSpec(g.shape, lambda i: (0, 0)),
                  pl.BlockSpec(wr.shape, lambda i: (0, 0))],
        out_specs=(pair, pair, pair, pl.BlockSpec((1, N_EXPERTS), lambda i: (0, 0))),
        scratch_shapes=[pltpu.VMEM((1, N_EXPERTS), F32)],
        compiler_params=_cparams(("arbitrary",)),
        name="router",
    )(x2, g, wr)


def _row_tile(ref, row):
    return ref.at[pl.ds(pl.multiple_of(row * ROW_SUBLANES, ROW_SUBLANES), ROW_SUBLANES), :]


def _dispatch_kernel(tail_start_ref, tail_len_ref, na_ref, dest_ref, x_ref, g_ref, hs_ref, stage, sems):
    i = pl.program_id(0)
    n_steps = pl.num_programs(0)
    tm = x_ref.shape[0]
    slot = i % 2
    rows_per_slot = stage.shape[1]

    def drain(s):
        for _ in range(TOP_K):
            pltpu.make_async_copy(stage.at[s], hs_ref.at[pl.ds(0, rows_per_slot), :], sems.at[s]).wait()

    @pl.when(i >= 2)
    def _():
        drain(slot)

    h = _rms(x_ref[...], g_ref[...])
    for s in range(ROW_SUBLANES):
        stage[slot, pl.ds(s, tm, stride=ROW_SUBLANES), :] = h[:, s * LANES:(s + 1) * LANES]

    def start(t, carry):
        for k in range(TOP_K):
            pltpu.make_async_copy(_row_tile(stage.at[slot], t), _row_tile(hs_ref, dest_ref[TOP_K * t + k]),
                                  sems.at[slot]).start(priority=k)
        return carry

    lax.fori_loop(0, tm, start, 0, unroll=8)

    @pl.when(i == n_steps - 1)
    def _():
        @pl.when(n_steps >= 2)
        def _():
            drain(1 - slot)

        drain(slot)
        zeros = stage.at[0]
        zeros[...] = jnp.zeros_like(zeros)
        for e in range(N_EXPERTS):
            def tail(r, e=e):
                return pltpu.make_async_copy(_row_tile(zeros, 0), _row_tile(hs_ref, tail_start_ref[e] + r), sems.at[0])

            lax.fori_loop(0, tail_len_ref[e], lambda r, c: (tail(r).start(), c)[1], 0)
            lax.fori_loop(0, tail_len_ref[e], lambda r, c: (tail(r).wait(), c)[1], 0)
        n_tiles = hs_ref.shape[0] // rows_per_slot

        def spare(j):
            rows = pl.ds(pl.multiple_of(j * rows_per_slot, rows_per_slot), rows_per_slot)
            return pltpu.make_async_copy(zeros, hs_ref.at[rows, :], sems.at[0])

        lax.fori_loop(na_ref[0], n_tiles, lambda j, c: (spare(j).start(), c)[1], 0)
        lax.fori_loop(na_ref[0], n_tiles, lambda j, c: (spare(j).wait(), c)[1], 0)


def _dispatch(x2, g, dest, tail_start, tail_len, n_active, n_rows):
    t = x2.shape[0]
    tm = MOE_TM
    return pl.pallas_call(
        _dispatch_kernel,
        out_shape=jax.ShapeDtypeStruct((n_rows * ROW_SUBLANES, LANES), F32),
        grid_spec=pltpu.PrefetchScalarGridSpec(
            num_scalar_prefetch=3,
            grid=(t // tm,),
            in_specs=[pl.BlockSpec((TOP_K * tm,), lambda i, ts, tl, na: (i,), memory_space=pltpu.SMEM),
                      pl.BlockSpec((tm, D_MODEL), lambda i, ts, tl, na: (i, 0)),
                      pl.BlockSpec(g.shape, lambda i, ts, tl, na: (0, 0))],
            out_specs=pl.BlockSpec(memory_space=pl.ANY),
            scratch_shapes=[pltpu.VMEM((2, tm * ROW_SUBLANES, LANES), F32), pltpu.SemaphoreType.DMA((2,))]),
        compiler_params=_cparams(("arbitrary",)),
        name="dispatch",
    )(tail_start, tail_len, n_active, dest, x2, g)


def _experts_kernel(te_ref, na_ref, hs_ref, wg_ref, wu_ref, wd_ref, ys_ref):
    active = pl.program_id(0) < na_ref[0]
    sub = hs_ref.shape[0] // ROW_SUBLANES // EXPERT_CHAINS

    def lane_groups(r):
        return [pl.ds(r * sub * ROW_SUBLANES + s, sub, stride=ROW_SUBLANES) for s in range(ROW_SUBLANES)]

    def up(r, _):
        h = jnp.concatenate([hs_ref[g, :].astype(BF16) for g in lane_groups(r)], axis=1)
        return (jax.nn.silu(_dot(h, wg_ref[...])) * _dot(h, wu_ref[...])).astype(BF16)

    def down(r, act):
        return _dot(act, wd_ref[...])

    def store(r, y):
        for s, g in enumerate(lane_groups(r)):
            ys_ref[g, :] = y[:, s * LANES:(s + 1) * LANES]

    @pl.when(active)
    def _():
        _software_pipeline(list(range(EXPERT_CHAINS)), (up, down, store))

    @pl.when(jnp.logical_not(active))
    def _():
        ys_ref[...] = jnp.zeros_like(ys_ref)


def _experts(hs, tile_expert, n_active, wg, wu, wd):
    tm = MOE_TM
    n_tiles = hs.shape[0] // (tm * ROW_SUBLANES)
    d_ff = wg.shape[2]
    once = pl.Buffered(1)
    return pl.pallas_call(
        _experts_kernel,
        out_shape=jax.ShapeDtypeStruct(hs.shape, F32),
        grid_spec=pltpu.PrefetchScalarGridSpec(
            num_scalar_prefetch=2,
            grid=(n_tiles,),
            in_specs=[pl.BlockSpec((tm * ROW_SUBLANES, LANES), lambda i, te, na: (jnp.minimum(i, na[0] - 1), 0)),
                      pl.BlockSpec((None, D_MODEL, d_ff), lambda i, te, na: (te[i], 0, 0), pipeline_mode=once),
                      pl.BlockSpec((None, D_MODEL, d_ff), lambda i, te, na: (te[i], 0, 0), pipeline_mode=once),
                      pl.BlockSpec((None, d_ff, D_MODEL), lambda i, te, na: (te[i], 0, 0), pipeline_mode=once)],
            out_specs=pl.BlockSpec((tm * ROW_SUBLANES, LANES), lambda i, te, na: (i, 0))),
        compiler_params=_cparams(("arbitrary",)),
        name="experts",
    )(tile_expert, n_active, hs, wg, wu, wd)


def _combine_kernel(dest_ref, next_dest_ref, x_ref, gate_ref, gf_ref, ys_ref, o_ref, bufs, sems, *, final):
    i = pl.program_id(0)
    tm = x_ref.shape[0]
    slot = i % 2

    def gather(idx_ref, s):
        def start(t, carry):
            for k in range(TOP_K):
                pltpu.make_async_copy(_row_tile(ys_ref, idx_ref[TOP_K * t + k]), _row_tile(bufs.at[s, k], t),
                                      sems.at[s]).start(priority=k)
            return carry

        lax.fori_loop(0, tm, start, 0, unroll=8)

    @pl.when(i == 0)
    def _():
        gather(dest_ref, slot)

    @pl.when(i + 1 < pl.num_programs(0))
    def _():
        gather(next_dest_ref, 1 - slot)

    for k in range(TOP_K):
        pltpu.make_async_copy(ys_ref.at[pl.ds(0, bufs.shape[2]), :], bufs.at[slot, k], sems.at[slot]).wait()
    gate = gate_ref[...]
    g0 = gate[:, 0:1]
    g1 = gate[:, 1:2]
    ss = jnp.zeros((tm, 1), F32)
    for s in range(ROW_SUBLANES):
        sl = slice(s * LANES, (s + 1) * LANES)
        rows = pl.ds(s, tm, stride=ROW_SUBLANES)
        y = x_ref[:, sl] + (g0 * bufs[slot, 0, rows, :] + g1 * bufs[slot, 1, rows, :])
        o_ref[:, sl] = y
        ss = ss + jnp.sum(y * y, axis=-1, keepdims=True)
    if final:
        o_ref[...] = o_ref[...] * lax.rsqrt(ss / D_MODEL + EPS) * gf_ref[...]


def _combine(x2, dest, gates, gf, ys, *, final):
    t = x2.shape[0]
    tm = MOE_TM
    return pl.pallas_call(
        functools.partial(_combine_kernel, final=final),
        out_shape=jax.ShapeDtypeStruct((t, D_MODEL), F32),
        grid=(t // tm,),
        in_specs=[pl.BlockSpec((TOP_K * tm,), lambda i: (i,), memory_space=pltpu.SMEM),
                  pl.BlockSpec((TOP_K * tm,), lambda i: (jnp.minimum(i + 1, t // tm - 1),),
                               memory_space=pltpu.SMEM),
                  pl.BlockSpec((tm, D_MODEL), lambda i: (i, 0)),
                  pl.BlockSpec((tm, TOP_K), lambda i: (i, 0)),
                  pl.BlockSpec(gf.shape, lambda i: (0, 0)),
                  pl.BlockSpec(memory_space=pl.ANY)],
        out_specs=pl.BlockSpec((tm, D_MODEL), lambda i: (i, 0)),
        scratch_shapes=[pltpu.VMEM((2, TOP_K, tm * ROW_SUBLANES, LANES), F32), pltpu.SemaphoreType.DMA((2,))],
        compiler_params=_cparams(("arbitrary",)),
        name="combine",
    )(dest, dest, x2, gates, gf, ys)


def _moe(x2, g, wr, wg, wu, wd, gf, *, final):
    t = x2.shape[0]
    tm = MOE_TM
    eid, rank, gates, counts = _router(x2, g, wr)
    counts = counts[0]
    padded = (counts + tm - 1) // tm * tm
    group_end = jnp.cumsum(padded)
    group_start = group_end - padded
    dest = (group_start[eid] + rank).reshape(-1)
    n_tiles = TOP_K * t // tm + N_EXPERTS
    n_active = (group_end[-1:] // tm).astype(jnp.int32)
    tile_row = jnp.arange(n_tiles, dtype=jnp.int32) * tm
    tile_row = jnp.minimum(tile_row, group_end[-1] - tm)
    tile_expert = jnp.sum(tile_row[:, None] >= group_end[None, :], axis=1).astype(jnp.int32)
    hs = _dispatch(x2, g, dest, group_start + counts, padded - counts, n_active, n_tiles * tm)
    ys = _experts(hs, tile_expert, n_active, wg, wu, wd)
    return _combine(x2, dest, gates, gf, ys, final=final)


FFN_TM = 1024
FFN_CHAINS = 4


def _ffn_kernel(x_ref, g_ref, wg_ref, wu_ref, wd_ref, gf_ref, o_ref, *, final):
    sub = x_ref.shape[0] // FFN_CHAINS

    def rows(r):
        return slice(r * sub, (r + 1) * sub)

    def up(r, _):
        h = _rms(x_ref[rows(r), :], g_ref[...]).astype(BF16)
        return (jax.nn.silu(_dot(h, wg_ref[...])) * _dot(h, wu_ref[...])).astype(BF16)

    def down(r, act):
        y = x_ref[rows(r), :] + _dot(act, wd_ref[...])
        if final:
            y = _rms(y, gf_ref[...])
        o_ref[rows(r), :] = y

    _software_pipeline(list(range(FFN_CHAINS)), (up, down))


def _ffn(x2, g, wg, wu, wd, gf, *, final):
    t = x2.shape[0]
    tm = FFN_TM
    full = lambda a: pl.BlockSpec(a.shape, lambda i: (0,) * a.ndim)
    return pl.pallas_call(
        functools.partial(_ffn_kernel, final=final),
        out_shape=jax.ShapeDtypeStruct((t, D_MODEL), F32),
        grid=(t // tm,),
        in_specs=[pl.BlockSpec((tm, D_MODEL), lambda i: (i, 0)), full(g), full(wg), full(wu), full(wd), full(gf)],
        out_specs=pl.BlockSpec((tm, D_MODEL), lambda i: (i, 0)),
        compiler_params=_cparams(("parallel",)),
        name="swiglu",
    )(x2, g, wg, wu, wd, gf)


def _rot_cols(w):
    half = A_ROPE // 2
    return jnp.concatenate([-w[..., half:], w[..., :half]], axis=-1)


def _prep_layer(w_in, w_uq, w_ukv, w_br_a, w_br_b, w_br_c):
    o = 0
    w_cq = w_in[:, o:o + A_Q_LORA]; o += A_Q_LORA
    w_ckv = w_in[:, o:o + A_KV_LORA]; o += A_KV_LORA
    w_kpe = w_in[:, o:o + A_ROPE]; o += A_ROPE
    nb = B_HEADS * B_HEAD_DIM
    w_b = w_in[:, o:o + 3 * nb]; o += 3 * nb
    nq = C_HEADS * C_HEAD_DIM
    nkv = C_KV_HEADS * C_HEAD_DIM
    w_c = w_in[:, o:o + nq + 2 * nkv]; o += nq + 2 * nkv
    w_g = w_in[:, o:]
    pad = jnp.zeros((D_MODEL, LANES - 2 * A_ROPE), F32)
    wa = jnp.concatenate([w_cq, w_ckv, w_kpe, _rot_cols(w_kpe), pad], axis=1)
    wb = jnp.concatenate([w_b[:, :nb] * (B_HEAD_DIM ** -0.5), w_b[:, nb:]], axis=1)
    q_c = w_c[:, :nq].reshape(D_MODEL, 2, C_HEADS // 2, C_HEAD_DIM).transpose(0, 2, 1, 3).reshape(D_MODEL, nq)
    wc = jnp.concatenate([q_c * (C_HEAD_DIM ** -0.5), w_c[:, nq:]], axis=1)
    uq = w_uq.reshape(A_Q_LORA, A_HEADS, A_NOPE + A_ROPE)
    z32 = jnp.zeros((A_Q_LORA, A_HEADS, LANES - A_NOPE - A_ROPE), F32)
    wq = jnp.concatenate([uq, z32], axis=-1).reshape(A_Q_LORA, A_HEADS * LANES)
    wqr = jnp.concatenate([jnp.zeros((A_Q_LORA, A_HEADS, A_NOPE), F32), _rot_cols(uq[..., A_NOPE:]), z32],
                          axis=-1).reshape(A_Q_LORA, A_HEADS * LANES)
    ukv = w_ukv.reshape(A_KV_LORA, A_HEADS, A_NOPE + A_V)
    wk = jnp.concatenate([ukv[..., :A_NOPE], jnp.zeros((A_KV_LORA, A_HEADS, LANES - A_NOPE), F32)],
                         axis=-1).reshape(A_KV_LORA, A_HEADS * LANES)
    uv = ukv[..., A_NOPE:].reshape(A_KV_LORA, A_HEADS // 2, 2, A_V)
    zv = jnp.zeros_like(uv[:, :, 0])
    wv = jnp.stack([jnp.concatenate([uv[:, :, 0], zv], axis=-1), jnp.concatenate([zv, uv[:, :, 1]], axis=-1)],
                   axis=2).reshape(A_KV_LORA, A_HEADS * LANES)
    w_br_c = w_br_c.reshape(2, C_HEADS // 2, C_HEAD_DIM, D_MODEL).transpose(1, 0, 2, 3).reshape(nq, D_MODEL)
    wbr = jnp.stack([w_br_a, w_br_b, w_br_c])
    wkv = jnp.concatenate([jnp.concatenate([wk, wv], axis=1),
                           jnp.concatenate([_rope_placement(), jnp.zeros((LANES, A_HEADS * LANES), F32)], axis=1)])
    cast = lambda a: a.astype(BF16)
    return dict(wa=cast(wa), wb=cast(wb), wc=cast(wc), wg=cast(w_g), wq=cast(wq), wqr=cast(wqr), wkv=cast(wkv),
                wbr=cast(wbr))


def _rope_tables():
    half = A_ROPE // 2
    inv = ROPE_THETA ** (-jnp.arange(half, dtype=F32) / half)
    ang = jnp.arange(SEQ, dtype=F32)[:, None] * inv[None, :]
    cos2 = jnp.tile(jnp.cos(ang), (1, 2))
    sin2 = jnp.tile(jnp.sin(ang), (1, 2))
    ones = jnp.ones((SEQ, A_NOPE), F32)
    z = lambda w: jnp.zeros((SEQ, w), F32)
    cq = jnp.concatenate([ones, cos2, z(LANES - A_NOPE - A_ROPE)], axis=1)
    sq = jnp.concatenate([z(A_NOPE), sin2, z(LANES - A_NOPE - A_ROPE)], axis=1)
    ck = jnp.concatenate([cos2, z(LANES - A_ROPE)], axis=1)
    sk = jnp.concatenate([sin2, z(LANES - A_ROPE)], axis=1)
    return cq, sq, ck, sk


def _rope_placement():
    place = np.zeros((LANES, A_HEADS * LANES), np.float32)
    for h in range(A_HEADS):
        for j in range(A_ROPE):
            place[j, h * LANES + A_NOPE + j] = 1.0
    return jnp.asarray(place)


def kernel(x, norm1_g, w_in, q_norm_g, w_uq, kv_norm_g, w_ukv, sink_logit, rel_bias, w_branch_a, w_branch_b,
           w_branch_c, w_out, norm2_g, ffn_w_gate, ffn_w_up, ffn_w_down, router_w, exp_w_gate, exp_w_up,
           exp_w_down, final_g):
    batch, seq, d = x.shape
    assert seq == SEQ and d == D_MODEL
    t = batch * seq
    x2 = x.reshape(t, d)

    wide_idx, narrow_idx = _dilated_index_tables()
    bias_wide = _bias_tables(rel_bias, wide_idx, 0, B_HEADS)
    bias_narrow = _bias_tables(rel_bias, narrow_idx, 0, B_HEADS)
    bias_c = _bias_tables(rel_bias, _window_index_tables(), B_HEADS, C_HEADS)
    bias_c = bias_c.reshape(2, C_HEADS // 2, *bias_c.shape[1:]).transpose(1, 0, 2, 3, 4)

    cq, sq, ck, sk = _rope_tables()
    row = lambda v: v.reshape(1, -1)
    q_then = lambda nq, rest: jnp.asarray(np.concatenate([np.full(nq, LOG2E), np.ones(rest)])[None], F32)
    bscale = q_then(B_HEADS * B_HEAD_DIM, 2 * B_HEADS * B_HEAD_DIM)
    cscale = q_then(C_HEADS * C_HEAD_DIM, 2 * C_KV_HEADS * C_HEAD_DIM)
    vones = jnp.asarray(np.tile(np.repeat([0.0, 1.0, 1.0, 0.0], A_V), A_HEADS // 2)[None], F32)

    for l in range(DEPTH):
        w = _prep_layer(w_in[l], w_uq[l], w_ukv[l], w_branch_a[l], w_branch_b[l], w_branch_c[l])
        g1 = row(norm1_g[l])
        qa, ka, va, zb, zc = _in_proj(x2, g1, w["wa"], w["wb"], w["wc"], row(q_norm_g[l]), row(kv_norm_g[l]),
                                      w["wq"], w["wqr"], w["wkv"], bscale, cscale, vones, cq, sq, ck, sk)
        ya = _mla_attention(qa, ka, va, batch)
        yb = _dilated_attention(zb, bias_wide, bias_narrow, batch)
        yc = _window_attention(zc, sink_logit[l], bias_c, batch)
        x2 = _merge(x2, g1, ya, yb, yc, w["wg"], w["wbr"], w_out[l].astype(BF16))
        g2 = row(norm2_g[l])
        final = l == DEPTH - 1
        i = l // 2
        if l % 2 == 0:
            x2 = _ffn(x2, g2, ffn_w_gate[i].astype(BF16), ffn_w_up[i].astype(BF16), ffn_w_down[i].astype(BF16),
                      row(final_g), final=final)
        else:
            x2 = _moe(x2, g2, router_w[i], exp_w_gate[i].astype(BF16), exp_w_up[i].astype(BF16),
                      exp_w_down[i].astype(BF16), row(final_g), final=final)
    return x2.reshape(batch, seq, d)
```

```python
import functools
import math

import numpy as np
import jax
import jax.numpy as jnp
from jax import lax
from jax.experimental import pallas as pl
from jax.experimental.pallas import tpu as pltpu

F32 = jnp.float32
BF16 = jnp.bfloat16

D_MODEL = 1024
SEQ = 2048
DEPTH = 2
A_HEADS = 8
A_NOPE = 64
A_ROPE = 32
A_V = 64
A_Q_LORA = 256
A_KV_LORA = 128
ROPE_THETA = 10000.0
B_HEADS = 8
B_HEAD_DIM = 64
B_GROUPS = ((128, 1), (512, 4), (2048, 16))
C_HEADS = 8
C_KV_HEADS = 2
C_HEAD_DIM = 64
C_RADIUS = 128
N_BUCKETS = 32
MAX_DISTANCE = 1024
D_FF_DENSE = 2816
N_EXPERTS = 8
D_FF_EXPERT = 3584
EPS = 1e-6
NEG_INF = -1e30

LANES = 128
HALF = 64
QBLK = 128
B_RADIUS = 64
VMEM_LIMIT = 56 * 1024 * 1024

assert all(w // (2 * d) == B_RADIUS and SEQ % (d * QBLK) == 0 for w, d in B_GROUPS)
assert B_GROUPS[0][1] == 1 and B_GROUPS[2][1] == B_GROUPS[1][1] ** 2
assert B_HEAD_DIM == C_HEAD_DIM == A_V == HALF and C_RADIUS == QBLK == 2 * B_RADIUS

A_SCALE = (A_NOPE + A_ROPE) ** -0.5
LOG2E = math.log2(math.e)


def _cparams(sem):
    return pltpu.CompilerParams(dimension_semantics=sem, vmem_limit_bytes=VMEM_LIMIT)


def _rms(x, g):
    return x * lax.rsqrt(jnp.mean(x * x, axis=-1, keepdims=True) + EPS) * g


def _dot(a, b):
    return jnp.dot(a, b, preferred_element_type=F32)


def _dot_nt(a, b):
    return lax.dot_general(a, b, (((1,), (1,)), ((), ())), preferred_element_type=F32)


def _t5_bucket_np(rel):
    nb = N_BUCKETS // 2
    max_exact = nb // 2
    ret = np.where(rel > 0, nb, 0)
    n = np.abs(rel)
    nf = np.maximum(n, 1).astype(np.float64)
    t = np.log(nf / max_exact) / math.log(MAX_DISTANCE / max_exact) * (nb - max_exact)
    frac = np.abs(t - np.round(t))
    assert np.all((frac > 1e-5) | (n <= max_exact) | (n >= MAX_DISTANCE))
    large = max_exact + np.floor(np.round(t, 9)).astype(np.int64)
    large = np.minimum(large, nb - 1)
    return (ret + np.where(n < max_exact, n, large)).astype(np.int32)


def _band_index(tq, tk, shift, radius, dil):
    rel = np.arange(tk)[None, :] - np.arange(tq)[:, None] + shift
    idx = _t5_bucket_np(rel * dil)
    return np.where(np.abs(rel) <= radius, idx, -1).astype(np.int32)


def _dilated_index_tables():
    wide = []
    for window, dil in B_GROUPS[:2]:
        for shift in (0, -B_RADIUS, -2 * B_RADIUS):
            wide.append(_band_index(QBLK, 2 * QBLK, shift, B_RADIUS, dil))
    narrow = [_band_index(QBLK, QBLK, 0, B_RADIUS, B_GROUPS[2][1])]
    return np.stack(wide), np.stack(narrow)


def _window_index_tables():
    return np.stack([_band_index(QBLK, 3 * QBLK, shift, C_RADIUS, 1) for shift in (0, -QBLK, -2 * QBLK)])


def _bias_kernel(tab_ref, idx_ref, o_ref, *, col0):
    col = pl.program_id(0) + col0
    idx = idx_ref[...]
    acc = jnp.full(idx.shape, NEG_INF, F32)
    for b in range(N_BUCKETS):
        acc = jnp.where(idx == b, tab_ref[b, col] * LOG2E, acc)
    o_ref[...] = acc


def _bias_tables(rel_bias, idx_np, col0, n_heads):
    nv, r, c = idx_np.shape
    return pl.pallas_call(
        functools.partial(_bias_kernel, col0=col0),
        out_shape=jax.ShapeDtypeStruct((n_heads, nv, r, c), F32),
        grid=(n_heads, nv),
        in_specs=[pl.BlockSpec(memory_space=pltpu.SMEM),
                  pl.BlockSpec((None, r, c), lambda h, v: (v, 0, 0))],
        out_specs=pl.BlockSpec((None, None, r, c), lambda h, v: (h, v, 0, 0)),
        compiler_params=_cparams(("arbitrary", "arbitrary")),
        name="bias_tables",
    )(rel_bias, jnp.asarray(idx_np))


IN_TM = 1024
IN_CHAINS = 4


def _in_proj_kernel(x_ref, g1_ref, wa_ref, wb_ref, wc_ref, qg_ref, kvg_ref, wq_ref, wqr_ref, wkv_ref,
                    bscale_ref, cscale_ref, vones_ref, cq_ref, sq_ref, ck_ref, sk_ref,
                    qa_ref, ka_ref, va_ref, zb_ref, zc_ref):
    sub = x_ref.shape[0] // IN_CHAINS

    def rows(r):
        return slice(r * sub, (r + 1) * sub)

    def project(r, _):
        h = _rms(x_ref[rows(r), :], g1_ref[...]).astype(BF16)
        za = _dot(h, wa_ref[...])
        zb_ref[rows(r), :] = (_dot(h, wb_ref[...]) * bscale_ref[...]).astype(BF16)
        zc_ref[rows(r), :] = (_dot(h, wc_ref[...]) * cscale_ref[...]).astype(BF16)
        return za

    def expand(r, za):
        hq = _rms(za[:, :A_Q_LORA], qg_ref[...]).astype(BF16)
        hkv = _rms(za[:, A_Q_LORA:A_Q_LORA + A_KV_LORA], kvg_ref[...]).astype(BF16)
        kpe = za[:, A_Q_LORA + A_KV_LORA:]
        kpe_rot = pltpu.roll(kpe, LANES - A_ROPE, axis=1)
        k_rope = (kpe * ck_ref[rows(r), :] + kpe_rot * sk_ref[rows(r), :]).astype(BF16)
        q = _dot(hq, wq_ref[...]) * (A_SCALE * LOG2E)
        q_rot = _dot(hq, wqr_ref[...]) * (A_SCALE * LOG2E)
        kv = _dot(jnp.concatenate([hkv, k_rope], axis=1), wkv_ref[...])
        return q, q_rot, kv

    def store(r, state):
        q, q_rot, kv = state
        ka_ref[rows(r), :] = kv[:, :A_HEADS * LANES].astype(BF16)
        va_ref[rows(r), :] = (kv[:, A_HEADS * LANES:] + vones_ref[...]).astype(BF16)
        cq = cq_ref[rows(r), :]
        sq = sq_ref[rows(r), :]
        for hh in range(A_HEADS):
            sl = slice(hh * LANES, (hh + 1) * LANES)
            qa_ref[rows(r), sl] = (q[:, sl] * cq + q_rot[:, sl] * sq).astype(BF16)

    _software_pipeline(list(range(IN_CHAINS)), (project, expand, store))


def _in_proj(x2, g1, wa, wb, wc, qg, kvg, wq, wqr, wkv, bscale, cscale, vones, cq, sq, ck, sk):
    t = x2.shape[0]
    tm = IN_TM
    n_pos = SEQ // tm
    full = lambda a: pl.BlockSpec(a.shape, lambda i: (0,) * a.ndim)
    row = lambda w: pl.BlockSpec((tm, w), lambda i: (i, 0))
    pos = pl.BlockSpec((tm, LANES), lambda i: (i % n_pos, 0))
    return pl.pallas_call(
        _in_proj_kernel,
        out_shape=(jax.ShapeDtypeStruct((t, A_HEADS * LANES), BF16),
                   jax.ShapeDtypeStruct((t, A_HEADS * LANES), BF16),
                   jax.ShapeDtypeStruct((t, A_HEADS * LANES), BF16),
                   jax.ShapeDtypeStruct((t, wb.shape[1]), BF16),
                   jax.ShapeDtypeStruct((t, wc.shape[1]), BF16)),
        grid=(t // tm,),
        in_specs=[row(D_MODEL), full(g1), full(wa), full(wb), full(wc), full(qg), full(kvg), full(wq), full(wqr),
                  full(wkv), full(bscale), full(cscale), full(vones), pos, pos, pos, pos],
        out_specs=(row(A_HEADS * LANES), row(A_HEADS * LANES), row(A_HEADS * LANES), row(wb.shape[1]),
                   row(wc.shape[1])),
        compiler_params=_cparams(("parallel",)),
        name="in_proj",
    )(x2, g1, wa, wb, wc, qg, kvg, wq, wqr, wkv, bscale, cscale, vones, cq, sq, ck, sk)


A_TQ = 2048
A_SUB = 512


def _mla_kernel(q_ref, k_ref, v_ref, o_ref):
    first = lax.broadcasted_iota(jnp.int32, (1, LANES), 1) < HALF
    tasks = [(r, hh) for r in range(A_TQ // A_SUB) for hh in range(2)]
    done = {}

    def rows(r):
        return slice(r * A_SUB, (r + 1) * A_SUB)

    def lanes(hh):
        return slice(hh * LANES, (hh + 1) * LANES)

    def scores(task, _):
        r, hh = task
        return _dot_nt(q_ref[rows(r), lanes(hh)], k_ref[:, lanes(hh)])

    def values(task, s):
        p = jnp.exp2(s - jnp.max(s, axis=-1, keepdims=True))
        return _dot(p.astype(BF16), v_ref[:, lanes(task[1])])

    def store(task, a):
        r, hh = task
        done[task] = a
        if hh == 1:
            a0 = done.pop((r, 0))
            a1 = done.pop((r, 1))
            l = pltpu.roll(jnp.where(first, a1, a0), HALF, axis=1)
            o_ref[rows(r), :] = (jnp.where(first, a0, a1) / l).astype(BF16)

    _software_pipeline(tasks, (scores, values, store))


def _mla_attention(qa, ka, va, batch):
    t = qa.shape[0]
    nq = SEQ // A_TQ
    return pl.pallas_call(
        _mla_kernel,
        out_shape=jax.ShapeDtypeStruct((t, A_HEADS * A_V), BF16),
        grid=(batch, A_HEADS // 2, nq),
        in_specs=[pl.BlockSpec((A_TQ, 2 * LANES), lambda b, p, i: (b * nq + i, p)),
                  pl.BlockSpec((SEQ, 2 * LANES), lambda b, p, i: (b, p)),
                  pl.BlockSpec((SEQ, 2 * LANES), lambda b, p, i: (b, p))],
        out_specs=pl.BlockSpec((A_TQ, LANES), lambda b, p, i: (b * nq + i, p)),
        compiler_params=_cparams(("parallel", "parallel", "arbitrary")),
        name="mla_attention",
    )(qa, ka, va)


def _software_pipeline(tasks, stages):
    state = [None] * len(tasks)
    for step in range(len(tasks) + len(stages) - 1):
        for s, stage in enumerate(stages):
            i = step - s
            if 0 <= i < len(tasks):
                state[i] = stage(tasks[i], state[i])


def _first_half():
    return lax.broadcasted_iota(jnp.int32, (1, LANES), 1) < HALF


def _pair_scores(q2, k2, biases):
    first = _first_half()
    zero = jnp.zeros_like(q2)
    tq = q2.shape[0]
    stacked = jnp.concatenate([jnp.where(first, q2, zero), jnp.where(first, zero, q2)], axis=0)
    s = _dot_nt(stacked, k2)
    return [s[:tq] + biases[0], s[tq:] + biases[1]]


def _pair_values(scores, v2, sinks=None):
    first = _first_half()
    one = jnp.ones_like(v2)
    res = []
    for hh, s in enumerate(scores):
        own = first if hh == 0 else jnp.logical_not(first)
        m = jnp.max(s, axis=-1, keepdims=True)
        if sinks is not None:
            m = jnp.maximum(m, sinks[hh])
        a = _dot(jnp.exp2(s - m).astype(BF16), jnp.where(own, v2, one))
        if sinks is not None:
            a = a + jnp.where(own, 0.0, jnp.exp2(sinks[hh] - m))
        res.append((m, a))
    return res


def _pair_merge(res):
    first = _first_half()
    (m0, a0), (m1, a1) = res
    return jnp.where(first, m0, m1), jnp.where(first, a1, a0), jnp.where(first, a0, a1)


def _dilated_kernel(q1, k1, v1, bw_ref, bn_ref, o_ref, nat32, by4_32, by4, by16, stream, by4_out, merged):
    n_blocks = SEQ // QBLK
    d4, d16 = B_GROUPS[1][1], B_GROUPS[2][1]
    len4, len16 = SEQ // d4, SEQ // d16
    for t, ref in enumerate((q1, k1, v1)):
        nat32[t] = ref[...].astype(F32)
        for g in range(d4):
            run = nat32[t, pl.ds(g, len4, stride=d4), :]
            by4_32[t, g * len4:(g + 1) * len4, :] = run
            by4[t, g] = run.astype(BF16)
        for g in range(d16):
            start = (g % d4) * len4 + g // d4
            by16[t, g] = by4_32[t, pl.ds(start, len16, stride=d4), :].astype(BF16)
    sources = ((q1, k1, v1), tuple(by4.at[t] for t in range(3)), tuple(by16.at[t] for t in range(3)))

    def rows(j):
        return slice(j * QBLK, (j + 1) * QBLK)

    def view(ref, gi, sl):
        dil = B_GROUPS[gi][1]
        if dil == 1:
            return ref[sl, :]
        length = SEQ // dil
        g = sl.start // length
        return ref[g, sl.start - g * length:sl.stop - g * length, :]

    def keys(gi, j):
        length = SEQ // B_GROUPS[gi][1]
        per_stream = length // QBLK
        if per_stream == 1:
            return rows(j)
        i = j % per_stream
        start = (j - i) * QBLK + min(max(i * QBLK - B_RADIUS, 0), length - 2 * QBLK)
        return slice(start, start + 2 * QBLK)

    def scores(task, _):
        gi, j = task
        per_stream = SEQ // B_GROUPS[gi][1] // QBLK
        if per_stream == 1:
            biases = (bn_ref[0, 0], bn_ref[1, 0])
        else:
            i = j % per_stream
            var = gi * 3 + (0 if i == 0 else 2 if i == per_stream - 1 else 1)
            biases = (bw_ref[0, var], bw_ref[1, var])
        q_ref, k_ref, _ = sources[gi]
        return _pair_scores(view(q_ref, gi, rows(j)), view(k_ref, gi, keys(gi, j)), biases)

    def values(task, s):
        gi, j = task
        return _pair_values(s, view(sources[gi][2], gi, keys(gi, j)))

    def store(task, res):
        gi, j = task
        for slot, value in enumerate(_pair_merge(res)):
            if gi == 0:
                merged[0, slot, rows(j), :] = value
            else:
                stream[gi - 1, slot, rows(j), :] = value

    _software_pipeline([(gi, j) for gi in range(len(B_GROUPS)) for j in range(n_blocks)], (scores, values, store))
    for slot in range(3):
        for g in range(d16):
            start = (g % d4) * len4 + g // d4
            by4_out[slot, pl.ds(start, len16, stride=d4), :] = stream[1, slot, g * len16:(g + 1) * len16, :]
        for g in range(d4):
            run = slice(g * len4, (g + 1) * len4)
            merged[1, slot, pl.ds(g, len4, stride=d4), :] = stream[0, slot, run, :]
            merged[2, slot, pl.ds(g, len4, stride=d4), :] = by4_out[slot, run, :]

    def merge(c, carry):
        rows = pl.ds(pl.multiple_of(c * QBLK, QBLK), QBLK)
        m = [merged[gi, 0, rows, :] for gi in range(3)]
        top = jnp.maximum(jnp.maximum(m[0], m[1]), m[2])
        w = [jnp.exp2(mg - top) for mg in m]
        l = [pltpu.roll(merged[gi, 1, rows, :], HALF, axis=1) for gi in range(3)]
        num = w[0] * merged[0, 2, rows, :] + w[1] * merged[1, 2, rows, :] + w[2] * merged[2, 2, rows, :]
        den = w[0] * l[0] + w[1] * l[1] + w[2] * l[2]
        o_ref[rows, :] = (num / den).astype(BF16)
        return carry

    lax.fori_loop(0, n_blocks, merge, 0, unroll=4)


def _dilated_attention(zb, bias_wide, bias_narrow, batch):
    t = zb.shape[0]
    n_pairs = B_HEADS // 2
    nw = bias_wide.shape[1]
    d4, d16 = B_GROUPS[1][1], B_GROUPS[2][1]
    seq_block = lambda off: pl.BlockSpec((SEQ, LANES), lambda p, b: (b, off + p))
    return pl.pallas_call(
        _dilated_kernel,
        out_shape=jax.ShapeDtypeStruct((t, B_HEADS * B_HEAD_DIM), BF16),
        grid=(n_pairs, batch),
        in_specs=[seq_block(0), seq_block(n_pairs), seq_block(2 * n_pairs),
                  pl.BlockSpec((2, nw, QBLK, 2 * QBLK), lambda p, b: (p, 0, 0, 0)),
                  pl.BlockSpec((2, 1, QBLK, QBLK), lambda p, b: (p, 0, 0, 0))],
        out_specs=pl.BlockSpec((SEQ, LANES), lambda p, b: (b, p)),
        scratch_shapes=[pltpu.VMEM((3, SEQ, LANES), F32), pltpu.VMEM((3, SEQ, LANES), F32),
                        pltpu.VMEM((3, d4, SEQ // d4, LANES), BF16), pltpu.VMEM((3, d16, SEQ // d16, LANES), BF16),
                        pltpu.VMEM((2, 3, SEQ, LANES), F32), pltpu.VMEM((3, SEQ, LANES), F32),
                        pltpu.VMEM((3, 3, SEQ, LANES), F32)],
        compiler_params=_cparams(("arbitrary", "arbitrary")),
        name="dilated_attention",
    )(zb, zb, zb, bias_wide, bias_narrow)


C_SEQS = 2


def _window_kernel(sink_ref, q_ref, k_ref, v_ref, bias_ref, o_ref):
    pair = pl.program_id(0)
    sinks = (sink_ref[pair] * LOG2E, sink_ref[pair + C_HEADS // 2] * LOG2E)
    n_blocks = SEQ // QBLK

    def keys(i):
        base, local = (i // n_blocks) * SEQ, i % n_blocks
        start = base + min(max((local - 1) * QBLK, 0), SEQ - 3 * QBLK)
        return slice(start, start + 3 * QBLK)

    def scores(i, _):
        local = i % n_blocks
        var = 0 if local == 0 else 2 if local == n_blocks - 1 else 1
        return _pair_scores(q_ref[i * QBLK:(i + 1) * QBLK, :], k_ref[keys(i), :],
                            (bias_ref[0, var], bias_ref[1, var]))

    def values(i, s):
        return _pair_values(s, v_ref[keys(i), :], sinks)

    def store(i, res):
        _, l_swapped, o = _pair_merge(res)
        o_ref[i * QBLK:(i + 1) * QBLK, :] = (o / pltpu.roll(l_swapped, HALF, axis=1)).astype(BF16)

    _software_pipeline(list(range(C_SEQS * n_blocks)), (scores, values, store))


def _window_attention(zc, sink, bias, batch):
    assert batch % C_SEQS == 0
    t = zc.shape[0]
    n_pairs = C_HEADS // 2
    return pl.pallas_call(
        _window_kernel,
        out_shape=jax.ShapeDtypeStruct((t, C_HEADS * C_HEAD_DIM), BF16),
        grid=(n_pairs, batch // C_SEQS),
        in_specs=[pl.BlockSpec(memory_space=pltpu.SMEM),
                  pl.BlockSpec((C_SEQS * SEQ, LANES), lambda p, b: (b, p)),
                  pl.BlockSpec((C_SEQS * SEQ, LANES), lambda p, b: (b, n_pairs)),
                  pl.BlockSpec((C_SEQS * SEQ, LANES), lambda p, b: (b, n_pairs + 1)),
                  pl.BlockSpec((None, 2, 3, QBLK, 3 * QBLK), lambda p, b: (p, 0, 0, 0, 0))],
        out_specs=pl.BlockSpec((C_SEQS * SEQ, LANES), lambda p, b: (b, p)),
        compiler_params=_cparams(("arbitrary", "arbitrary")),
        name="window_attention",
    )(sink, zc, zc, zc, bias)


MERGE_TM = 1024
MERGE_CHAINS = 4


def _merge_kernel(x_ref, g1_ref, ya_ref, yb_ref, yc_ref, wg_ref, wbr_ref, wo_ref, o_ref):
    sub = x_ref.shape[0] // MERGE_CHAINS

    def rows(r):
        return slice(r * sub, (r + 1) * sub)

    def gated_sum(r, _):
        h = _rms(x_ref[rows(r), :], g1_ref[...]).astype(BF16)
        merged = None
        for b, y_ref in enumerate((ya_ref, yb_ref, yc_ref)):
            gate = jax.nn.sigmoid(_dot(h, wg_ref[:, b * D_MODEL:(b + 1) * D_MODEL]))
            term = gate * _dot(y_ref[rows(r), :], wbr_ref[b])
            merged = term if merged is None else merged + term
        return merged.astype(BF16)

    def project(r, merged):
        o_ref[rows(r), :] = x_ref[rows(r), :] + _dot(merged, wo_ref[...])

    _software_pipeline(list(range(MERGE_CHAINS)), (gated_sum, project))


def _merge(x2, g1, ya, yb, yc, wg, wbr, wo):
    t = x2.shape[0]
    tm = MERGE_TM
    full = lambda a: pl.BlockSpec(a.shape, lambda i: (0,) * a.ndim)
    row = lambda w: pl.BlockSpec((tm, w), lambda i: (i, 0))
    return pl.pallas_call(
        _merge_kernel,
        out_shape=jax.ShapeDtypeStruct((t, D_MODEL), F32),
        grid=(t // tm,),
        in_specs=[row(D_MODEL), full(g1), row(ya.shape[1]), row(yb.shape[1]), row(yc.shape[1]),
                  full(wg), full(wbr), full(wo)],
        out_specs=row(D_MODEL),
        compiler_params=_cparams(("parallel",)),
        name="gate_merge",
    )(x2, g1, ya, yb, yc, wg, wbr, wo)


TOP_K = 2
ROW_SUBLANES = D_MODEL // LANES
MOE_TM = 512
ROUTER_TM = 2048
ROUTER_CHAINS = 4
EXPERT_CHAINS = 2


def _router_kernel(x_ref, g_ref, wr_ref, eid_ref, rank_ref, gate_ref, cnt_ref, carry):
    @pl.when(pl.program_id(0) == 0)
    def _():
        carry[...] = jnp.zeros_like(carry)

    sub = x_ref.shape[0] // ROUTER_CHAINS
    w = wr_ref[...]
    w_hi = w.astype(BF16)
    w_lo = (w - w_hi.astype(F32)).astype(BF16)
    earlier = (lax.broadcasted_iota(jnp.int32, (sub, sub), 0) > lax.broadcasted_iota(jnp.int32, (sub, sub), 1))
    earlier = jnp.where(earlier, 1.0, 0.0).astype(BF16)

    def rows(r):
        return slice(r * sub, (r + 1) * sub)

    def logits(r, _):
        h = _rms(x_ref[rows(r), :], g_ref[...])
        h_hi = h.astype(BF16)
        h_lo = (h - h_hi.astype(F32)).astype(BF16)
        return _dot(h_hi, w_hi) + (_dot(h_lo, w_hi) + _dot(h_hi, w_lo))

    def select(r, lg):
        idx = lax.broadcasted_iota(jnp.int32, lg.shape, 1)
        m1 = jnp.max(lg, axis=-1, keepdims=True)
        i1 = jnp.min(jnp.where(lg == m1, idx, N_EXPERTS), axis=-1, keepdims=True)
        first = idx == i1
        rest = jnp.where(first, -jnp.inf, lg)
        m2 = jnp.max(rest, axis=-1, keepdims=True)
        i2 = jnp.min(jnp.where(rest == m2, idx, N_EXPERTS), axis=-1, keepdims=True)
        second = idx == i2
        e = jnp.exp(m2 - m1)
        chosen = jnp.where(jnp.logical_or(first, second), 1.0, 0.0)
        return i1, i2, first, second, e, chosen, _dot(earlier, chosen.astype(BF16))

    def emit(r, state):
        i1, i2, first, second, e, chosen, before = state
        before = before + carry[...]
        r1 = jnp.sum(jnp.where(first, before, 0.0), axis=-1, keepdims=True)
        r2 = jnp.sum(jnp.where(second, before, 0.0), axis=-1, keepdims=True)
        carry[...] += jnp.sum(chosen, axis=0, keepdims=True)
        den = 1.0 + e
        slot0 = lax.broadcasted_iota(jnp.int32, (sub, TOP_K), 1) == 0
        eid_ref[rows(r), :] = jnp.where(slot0, i1, i2)
        rank_ref[rows(r), :] = jnp.where(slot0, r1, r2).astype(jnp.int32)
        gate_ref[rows(r), :] = jnp.where(slot0, 1.0 / den, e / den)

    _software_pipeline(list(range(ROUTER_CHAINS)), (logits, select, emit))
    cnt_ref[...] = carry[...].astype(jnp.int32)


def _router(x2, g, wr):
    t = x2.shape[0]
    tm = ROUTER_TM
    pair = pl.BlockSpec((tm, TOP_K), lambda i: (i, 0))
    return pl.pallas_call(
        _router_kernel,
        out_shape=(jax.ShapeDtypeStruct((t, TOP_K), jnp.int32), jax.ShapeDtypeStruct((t, TOP_K), jnp.int32),
                   jax.ShapeDtypeStruct((t, TOP_K), F32), jax.ShapeDtypeStruct((1, N_EXPERTS), jnp.int32)),
        grid=(t // tm,),
        in_specs=[pl.BlockSpec((tm, D_MODEL), lambda i: (i, 0)),
                  pl.BlockSpec(g.shape, lambda i: (0, 0)),
                  pl.BlockSpec(wr.shape, lambda i: (0, 0))],
        out_specs=(pair, pair, pair, pl.BlockSpec((1, N_EXPERTS), lambda i: (0, 0))),
        scratch_shapes=[pltpu.VMEM((1, N_EXPERTS), F32)],
        compiler_params=_cparams(("arbitrary",)),
        name="router",
    )(x2, g, wr)


def _row_tile(ref, row):
    return ref.at[pl.ds(pl.multiple_of(row * ROW_SUBLANES, ROW_SUBLANES), ROW_SUBLANES), :]


def _dispatch_kernel(tail_start_ref, tail_len_ref, na_ref, dest_ref, x_ref, g_ref, hs_ref, stage, sems):
    i = pl.program_id(0)
    n_steps = pl.num_programs(0)
    tm = x_ref.shape[0]
    slot = i % 2
    rows_per_slot = stage.shape[1]

    def drain(s):
        for _ in range(TOP_K):
            pltpu.make_async_copy(stage.at[s], hs_ref.at[pl.ds(0, rows_per_slot), :], sems.at[s]).wait()

    @pl.when(i >= 2)
    def _():
        drain(slot)

    h = _rms(x_ref[...], g_ref[...])
    for s in range(ROW_SUBLANES):
        stage[slot, pl.ds(s, tm, stride=ROW_SUBLANES), :] = h[:, s * LANES:(s + 1) * LANES]

    def start(t, carry):
        for k in range(TOP_K):
            pltpu.make_async_copy(_row_tile(stage.at[slot], t), _row_tile(hs_ref, dest_ref[TOP_K * t + k]),
                                  sems.at[slot]).start(priority=k)
        return carry

    lax.fori_loop(0, tm, start, 0, unroll=8)

    @pl.when(i == n_steps - 1)
    def _():
        @pl.when(n_steps >= 2)
        def _():
            drain(1 - slot)

        drain(slot)
        zeros = stage.at[0]
        zeros[...] = jnp.zeros_like(zeros)
        for e in range(N_EXPERTS):
            def tail(r, e=e):
                return pltpu.make_async_copy(_row_tile(zeros, 0), _row_tile(hs_ref, tail_start_ref[e] + r), sems.at[0])

            lax.fori_loop(0, tail_len_ref[e], lambda r, c: (tail(r).start(), c)[1], 0)
            lax.fori_loop(0, tail_len_ref[e], lambda r, c: (tail(r).wait(), c)[1], 0)
        n_tiles = hs_ref.shape[0] // rows_per_slot

        def spare(j):
            rows = pl.ds(pl.multiple_of(j * rows_per_slot, rows_per_slot), rows_per_slot)
            return pltpu.make_async_copy(zeros, hs_ref.at[rows, :], sems.at[0])

        lax.fori_loop(na_ref[0], n_tiles, lambda j, c: (spare(j).start(), c)[1], 0)
        lax.fori_loop(na_ref[0], n_tiles, lambda j, c: (spare(j).wait(), c)[1], 0)


def _dispatch(x2, g, dest, tail_start, tail_len, n_active, n_rows):
    t = x2.shape[0]
    tm = MOE_TM
    return pl.pallas_call(
        _dispatch_kernel,
        out_shape=jax.ShapeDtypeStruct((n_rows * ROW_SUBLANES, LANES), F32),
        grid_spec=pltpu.PrefetchScalarGridSpec(
            num_scalar_prefetch=3,
            grid=(t // tm,),
            in_specs=[pl.BlockSpec((TOP_K * tm,), lambda i, ts, tl, na: (i,), memory_space=pltpu.SMEM),
                      pl.BlockSpec((tm, D_MODEL), lambda i, ts, tl, na: (i, 0)),
                      pl.BlockSpec(g.shape, lambda i, ts, tl, na: (0, 0))],
            out_specs=pl.BlockSpec(memory_space=pl.ANY),
            scratch_shapes=[pltpu.VMEM((2, tm * ROW_SUBLANES, LANES), F32), pltpu.SemaphoreType.DMA((2,))]),
        compiler_params=_cparams(("arbitrary",)),
        name="dispatch",
    )(tail_start, tail_len, n_active, dest, x2, g)


def _experts_kernel(te_ref, na_ref, hs_ref, wg_ref, wu_ref, wd_ref, ys_ref):
    active = pl.program_id(0) < na_ref[0]
    sub = hs_ref.shape[0] // ROW_SUBLANES // EXPERT_CHAINS

    def lane_groups(r):
        return [pl.ds(r * sub * ROW_SUBLANES + s, sub, stride=ROW_SUBLANES) for s in range(ROW_SUBLANES)]

    def up(r, _):
        h = jnp.concatenate([hs_ref[g, :].astype(BF16) for g in lane_groups(r)], axis=1)
        return (jax.nn.silu(_dot(h, wg_ref[...])) * _dot(h, wu_ref[...])).astype(BF16)

    def down(r, act):
        return _dot(act, wd_ref[...])

    def store(r, y):
        for s, g in enumerate(lane_groups(r)):
            ys_ref[g, :] = y[:, s * LANES:(s + 1) * LANES]

    @pl.when(active)
    def _():
        _software_pipeline(list(range(EXPERT_CHAINS)), (up, down, store))

    @pl.when(jnp.logical_not(active))
    def _():
        ys_ref[...] = jnp.zeros_like(ys_ref)


def _experts(hs, tile_expert, n_active, wg, wu, wd):
    tm = MOE_TM
    n_tiles = hs.shape[0] // (tm * ROW_SUBLANES)
    d_ff = wg.shape[2]
    once = pl.Buffered(1)
    return pl.pallas_call(
        _experts_kernel,
        out_shape=jax.ShapeDtypeStruct(hs.shape, F32),
        grid_spec=pltpu.PrefetchScalarGridSpec(
            num_scalar_prefetch=2,
            grid=(n_tiles,),
            in_specs=[pl.BlockSpec((tm * ROW_SUBLANES, LANES), lambda i, te, na: (jnp.minimum(i, na[0] - 1), 0)),
                      pl.BlockSpec((None, D_MODEL, d_ff), lambda i, te, na: (te[i], 0, 0), pipeline_mode=once),
                      pl.BlockSpec((None, D_MODEL, d_ff), lambda i, te, na: (te[i], 0, 0), pipeline_mode=once),
                      pl.BlockSpec((None, d_ff, D_MODEL), lambda i, te, na: (te[i], 0, 0), pipeline_mode=once)],
            out_specs=pl.BlockSpec((tm * ROW_SUBLANES, LANES), lambda i, te, na: (i, 0))),
        compiler_params=_cparams(("arbitrary",)),
        name="experts",
    )(tile_expert, n_active, hs, wg, wu, wd)


def _combine_kernel(dest_ref, next_dest_ref, x_ref, gate_ref, gf_ref, ys_ref, o_ref, bufs, sems, *, final):
    i = pl.program_id(0)
    tm = x_ref.shape[0]
    slot = i % 2

    def gather(idx_ref, s):
        def start(t, carry):
            for k in range(TOP_K):
                pltpu.make_async_copy(_row_tile(ys_ref, idx_ref[TOP_K * t + k]), _row_tile(bufs.at[s, k], t),
                                      sems.at[s]).start(priority=k)
            return carry

        lax.fori_loop(0, tm, start, 0, unroll=8)

    @pl.when(i == 0)
    def _():
        gather(dest_ref, slot)

    @pl.when(i + 1 < pl.num_programs(0))
    def _():
        gather(next_dest_ref, 1 - slot)

    for k in range(TOP_K):
        pltpu.make_async_copy(ys_ref.at[pl.ds(0, bufs.shape[2]), :], bufs.at[slot, k], sems.at[slot]).wait()
    gate = gate_ref[...]
    g0 = gate[:, 0:1]
    g1 = gate[:, 1:2]
    ss = jnp.zeros((tm, 1), F32)
    for s in range(ROW_SUBLANES):
        sl = slice(s * LANES, (s + 1) * LANES)
        rows = pl.ds(s, tm, stride=ROW_SUBLANES)
        y = x_ref[:, sl] + (g0 * bufs[slot, 0, rows, :] + g1 * bufs[slot, 1, rows, :])
        o_ref[:, sl] = y
        ss = ss + jnp.sum(y * y, axis=-1, keepdims=True)
    if final:
        o_ref[...] = o_ref[...] * lax.rsqrt(ss / D_MODEL + EPS) * gf_ref[...]


def _combine(x2, dest, gates, gf, ys, *, final):
    t = x2.shape[0]
    tm = MOE_TM
    return pl.pallas_call(
        functools.partial(_combine_kernel, final=final),
        out_shape=jax.ShapeDtypeStruct((t, D_MODEL), F32),
        grid=(t // tm,),
        in_specs=[pl.BlockSpec((TOP_K * tm,), lambda i: (i,), memory_space=pltpu.SMEM),
                  pl.BlockSpec((TOP_K * tm,), lambda i: (jnp.minimum(i + 1, t // tm - 1),),
                               memory_space=pltpu.SMEM),
                  pl.BlockSpec((tm, D_MODEL), lambda i: (i, 0)),
                  pl.BlockSpec((tm, TOP_K), lambda i: (i, 0)),
                  pl.BlockSpec(gf.shape, lambda i: (0, 0)),
                  pl.BlockSpec(memory_space=pl.ANY)],
        out_specs=pl.BlockSpec((tm, D_MODEL), lambda i: (i, 0)),
        scratch_shapes=[pltpu.VMEM((2, TOP_K, tm * ROW_SUBLANES, LANES), F32), pltpu.SemaphoreType.DMA((2,))],
        compiler_params=_cparams(("arbitrary",)),
        name="combine",
    )(dest, dest, x2, gates, gf, ys)


def _moe(x2, g, wr, wg, wu, wd, gf, *, final):
    t = x2.shape[0]
    tm = MOE_TM
    eid, rank, gates, counts = _router(x2, g, wr)
    counts = counts[0]
    padded = (counts + tm - 1) // tm * tm
    group_end = jnp.cumsum(padded)
    group_start = group_end - padded
    dest = (group_start[eid] + rank).reshape(-1)
    n_tiles = TOP_K * t // tm + N_EXPERTS
    n_active = (group_end[-1:] // tm).astype(jnp.int32)
    tile_row = jnp.arange(n_tiles, dtype=jnp.int32) * tm
    tile_row = jnp.minimum(tile_row, group_end[-1] - tm)
    tile_expert = jnp.sum(tile_row[:, None] >= group_end[None, :], axis=1).astype(jnp.int32)
    hs = _dispatch(x2, g, dest, group_start + counts, padded - counts, n_active, n_tiles * tm)
    ys = _experts(hs, tile_expert, n_active, wg, wu, wd)
    return _combine(x2, dest, gates, gf, ys, final=final)


FFN_TM = 1024
FFN_CHAINS = 4


def _ffn_kernel(x_ref, g_ref, wg_ref, wu_ref, wd_ref, gf_ref, o_ref, *, final):
    sub = x_ref.shape[0] // FFN_CHAINS

    def rows(r):
        return slice(r * sub, (r + 1) * sub)

    def up(r, _):
        h = _rms(x_ref[rows(r), :], g_ref[...]).astype(BF16)
        return (jax.nn.silu(_dot(h, wg_ref[...])) * _dot(h, wu_ref[...])).astype(BF16)

    def down(r, act):
        y = x_ref[rows(r), :] + _dot(act, wd_ref[...])
        if final:
            y = _rms(y, gf_ref[...])
        o_ref[rows(r), :] = y

    _software_pipeline(list(range(FFN_CHAINS)), (up, down))


def _ffn(x2, g, wg, wu, wd, gf, *, final):
    t = x2.shape[0]
    tm = FFN_TM
    full = lambda a: pl.BlockSpec(a.shape, lambda i: (0,) * a.ndim)
    return pl.pallas_call(
        functools.partial(_ffn_kernel, final=final),
        out_shape=jax.ShapeDtypeStruct((t, D_MODEL), F32),
        grid=(t // tm,),
        in_specs=[pl.BlockSpec((tm, D_MODEL), lambda i: (i, 0)), full(g), full(wg), full(wu), full(wd), full(gf)],
        out_specs=pl.BlockSpec((tm, D_MODEL), lambda i: (i, 0)),
        compiler_params=_cparams(("parallel",)),
        name="swiglu",
    )(x2, g, wg, wu, wd, gf)


def _rot_cols(w):
    half = A_ROPE // 2
    return jnp.concatenate([-w[..., half:], w[..., :half]], axis=-1)


def _prep_layer(w_in, w_uq, w_ukv, w_br_a, w_br_b, w_br_c):
    o = 0
    w_cq = w_in[:, o:o + A_Q_LORA]; o += A_Q_LORA
    w_ckv = w_in[:, o:o + A_KV_LORA]; o += A_KV_LORA
    w_kpe = w_in[:, o:o + A_ROPE]; o += A_ROPE
    nb = B_HEADS * B_HEAD_DIM
    w_b = w_in[:, o:o + 3 * nb]; o += 3 * nb
    nq = C_HEADS * C_HEAD_DIM
    nkv = C_KV_HEADS * C_HEAD_DIM
    w_c = w_in[:, o:o + nq + 2 * nkv]; o += nq + 2 * nkv
    w_g = w_in[:, o:]
    pad = jnp.zeros((D_MODEL, LANES - 2 * A_ROPE), F32)
    wa = jnp.concatenate([w_cq, w_ckv, w_kpe, _rot_cols(w_kpe), pad], axis=1)
    wb = jnp.concatenate([w_b[:, :nb] * (B_HEAD_DIM ** -0.5), w_b[:, nb:]], axis=1)
    q_c = w_c[:, :nq].reshape(D_MODEL, 2, C_HEADS // 2, C_HEAD_DIM).transpose(0, 2, 1, 3).reshape(D_MODEL, nq)
    wc = jnp.concatenate([q_c * (C_HEAD_DIM ** -0.5), w_c[:, nq:]], axis=1)
    uq = w_uq.reshape(A_Q_LORA, A_HEADS, A_NOPE + A_ROPE)
    z32 = jnp.zeros((A_Q_LORA, A_HEADS, LANES - A_NOPE - A_ROPE), F32)
    wq = jnp.concatenate([uq, z32], axis=-1).reshape(A_Q_LORA, A_HEADS * LANES)
    wqr = jnp.concatenate([jnp.zeros((A_Q_LORA, A_HEADS, A_NOPE), F32), _rot_cols(uq[..., A_NOPE:]), z32],
                          axis=-1).reshape(A_Q_LORA, A_HEADS * LANES)
    ukv = w_ukv.reshape(A_KV_LORA, A_HEADS, A_NOPE + A_V)
    wk = jnp.concatenate([ukv[..., :A_NOPE], jnp.zeros((A_KV_LORA, A_HEADS, LANES - A_NOPE), F32)],
                         axis=-1).reshape(A_KV_LORA, A_HEADS * LANES)
    uv = ukv[..., A_NOPE:].reshape(A_KV_LORA, A_HEADS // 2, 2, A_V)
    zv = jnp.zeros_like(uv[:, :, 0])
    wv = jnp.stack([jnp.concatenate([uv[:, :, 0], zv], axis=-1), jnp.concatenate([zv, uv[:, :, 1]], axis=-1)],
                   axis=2).reshape(A_KV_LORA, A_HEADS * LANES)
    w_br_c = w_br_c.reshape(2, C_HEADS // 2, C_HEAD_DIM, D_MODEL).transpose(1, 0, 2, 3).reshape(nq, D_MODEL)
    wbr = jnp.stack([w_br_a, w_br_b, w_br_c])
    wkv = jnp.concatenate([jnp.concatenate([wk, wv], axis=1),
                           jnp.concatenate([_rope_placement(), jnp.zeros((LANES, A_HEADS * LANES), F32)], axis=1)])
    cast = lambda a: a.astype(BF16)
    return dict(wa=cast(wa), wb=cast(wb), wc=cast(wc), wg=cast(w_g), wq=cast(wq), wqr=cast(wqr), wkv=cast(wkv),
                wbr=cast(wbr))


def _rope_tables():
    half = A_ROPE // 2
    inv = ROPE_THETA ** (-jnp.arange(half, dtype=F32) / half)
    ang = jnp.arange(SEQ, dtype=F32)[:, None] * inv[None, :]
    cos2 = jnp.tile(jnp.cos(ang), (1, 2))
    sin2 = jnp.tile(jnp.sin(ang), (1, 2))
    ones = jnp.ones((SEQ, A_NOPE), F32)
    z = lambda w: jnp.zeros((SEQ, w), F32)
    cq = jnp.concatenate([ones, cos2, z(LANES - A_NOPE - A_ROPE)], axis=1)
    sq = jnp.concatenate([z(A_NOPE), sin2, z(LANES - A_NOPE - A_ROPE)], axis=1)
    ck = jnp.concatenate([cos2, z(LANES - A_ROPE)], axis=1)
    sk = jnp.concatenate([sin2, z(LANES - A_ROPE)], axis=1)
    return cq, sq, ck, sk


def _rope_placement():
    place = np.zeros((LANES, A_HEADS * LANES), np.float32)
    for h in range(A_HEADS):
        for j in range(A_ROPE):
            place[j, h * LANES + A_NOPE + j] = 1.0
    return jnp.asarray(place)


def kernel(x, norm1_g, w_in, q_norm_g, w_uq, kv_norm_g, w_ukv, sink_logit, rel_bias, w_branch_a, w_branch_b,
           w_branch_c, w_out, norm2_g, ffn_w_gate, ffn_w_up, ffn_w_down, router_w, exp_w_gate, exp_w_up,
           exp_w_down, final_g):
    batch, seq, d = x.shape
    assert seq == SEQ and d == D_MODEL
    t = batch * seq
    x2 = x.reshape(t, d)

    wide_idx, narrow_idx = _dilated_index_tables()
    bias_wide = _bias_tables(rel_bias, wide_idx, 0, B_HEADS)
    bias_narrow = _bias_tables(rel_bias, narrow_idx, 0, B_HEADS)
    bias_c = _bias_tables(rel_bias, _window_index_tables(), B_HEADS, C_HEADS)
    bias_c = bias_c.reshape(2, C_HEADS // 2, *bias_c.shape[1:]).transpose(1, 0, 2, 3, 4)

    cq, sq, ck, sk = _rope_tables()
    row = lambda v: v.reshape(1, -1)
    q_then = lambda nq, rest: jnp.asarray(np.concatenate([np.full(nq, LOG2E), np.ones(rest)])[None], F32)
    bscale = q_then(B_HEADS * B_HEAD_DIM, 2 * B_HEADS * B_HEAD_DIM)
    cscale = q_then(C_HEADS * C_HEAD_DIM, 2 * C_KV_HEADS * C_HEAD_DIM)
    vones = jnp.asarray(np.tile(np.repeat([0.0, 1.0, 1.0, 0.0], A_V), A_HEADS // 2)[None], F32)

    for l in range(DEPTH):
        w = _prep_layer(w_in[l], w_uq[l], w_ukv[l], w_branch_a[l], w_branch_b[l], w_branch_c[l])
        g1 = row(norm1_g[l])
        qa, ka, va, zb, zc = _in_proj(x2, g1, w["wa"], w["wb"], w["wc"], row(q_norm_g[l]), row(kv_norm_g[l]),
                                      w["wq"], w["wqr"], w["wkv"], bscale, cscale, vones, cq, sq, ck, sk)
        ya = _mla_attention(qa, ka, va, batch)
        yb = _dilated_attention(zb, bias_wide, bias_narrow, batch)
        yc = _window_attention(zc, sink_logit[l], bias_c, batch)
        x2 = _merge(x2, g1, ya, yb, yc, w["wg"], w["wbr"], w_out[l].astype(BF16))
        g2 = row(norm2_g[l])
        final = l == DEPTH - 1
        i = l // 2
        if l % 2 == 0:
            x2 = _ffn(x2, g2, ffn_w_gate[i].astype(BF16), ffn_w_up[i].astype(BF16), ffn_w_down[i].astype(BF16),
                      row(final_g), final=final)
        else:
            x2 = _moe(x2, g2, router_w[i], exp_w_gate[i].astype(BF16), exp_w_up[i].astype(BF16),
                      exp_w_down[i].astype(BF16), row(final_g), final=final)
    return x2.reshape(batch, seq, d)
```

```python
import functools
import math

import numpy as np
import jax
import jax.numpy as jnp
from jax import lax
from jax.experimental import pallas as pl
from jax.experimental.pallas import tpu as pltpu

F32 = jnp.float32
BF16 = jnp.bfloat16

D_MODEL = 1024
SEQ = 2048
DEPTH = 2
A_HEADS = 8
A_NOPE = 64
A_ROPE = 32
A_V = 64
A_Q_LORA = 256
A_KV_LORA = 128
ROPE_THETA = 10000.0
B_HEADS = 8
B_HEAD_DIM = 64
B_GROUPS = ((128, 1), (512, 4), (2048, 16))
C_HEADS = 8
C_KV_HEADS = 2
C_HEAD_DIM = 64
C_RADIUS = 128
N_BUCKETS = 32
MAX_DISTANCE = 1024
D_FF_DENSE = 2816
N_EXPERTS = 8
D_FF_EXPERT = 3584
EPS = 1e-6
NEG_INF = -1e30

LANES = 128
HALF = 64
QBLK = 128
B_RADIUS = 64
VMEM_LIMIT = 56 * 1024 * 1024

assert all(w // (2 * d) == B_RADIUS and SEQ % (d * QBLK) == 0 for w, d in B_GROUPS)
assert B_GROUPS[0][1] == 1 and B_GROUPS[2][1] == B_GROUPS[1][1] ** 2
assert B_HEAD_DIM == C_HEAD_DIM == A_V == HALF and C_RADIUS == QBLK == 2 * B_RADIUS

A_SCALE = (A_NOPE + A_ROPE) ** -0.5
LOG2E = math.log2(math.e)


def _cparams(sem):
    return pltpu.CompilerParams(dimension_semantics=sem, vmem_limit_bytes=VMEM_LIMIT)


def _rms(x, g):
    return x * lax.rsqrt(jnp.mean(x * x, axis=-1, keepdims=True) + EPS) * g


def _dot(a, b):
    return jnp.dot(a, b, preferred_element_type=F32)


def _dot_nt(a, b):
    return lax.dot_general(a, b, (((1,), (1,)), ((), ())), preferred_element_type=F32)


def _t5_bucket_np(rel):
    nb = N_BUCKETS // 2
    max_exact = nb // 2
    ret = np.where(rel > 0, nb, 0)
    n = np.abs(rel)
    nf = np.maximum(n, 1).astype(np.float64)
    t = np.log(nf / max_exact) / math.log(MAX_DISTANCE / max_exact) * (nb - max_exact)
    frac = np.abs(t - np.round(t))
    assert np.all((frac > 1e-5) | (n <= max_exact) | (n >= MAX_DISTANCE))
    large = max_exact + np.floor(np.round(t, 9)).astype(np.int64)
    large = np.minimum(large, nb - 1)
    return (ret + np.where(n < max_exact, n, large)).astype(np.int32)


def _band_index(tq, tk, shift, radius, dil):
    rel = np.arange(tk)[None, :] - np.arange(tq)[:, None] + shift
    idx = _t5_bucket_np(rel * dil)
    return np.where(np.abs(rel) <= radius, idx, -1).astype(np.int32)


def _dilated_index_tables():
    wide = []
    for window, dil in B_GROUPS[:2]:
        for shift in (0, -B_RADIUS, -2 * B_RADIUS):
            wide.append(_band_index(QBLK, 2 * QBLK, shift, B_RADIUS, dil))
    narrow = [_band_index(QBLK, QBLK, 0, B_RADIUS, B_GROUPS[2][1])]
    return np.stack(wide), np.stack(narrow)


def _window_index_tables():
    return np.stack([_band_index(QBLK, 3 * QBLK, shift, C_RADIUS, 1) for shift in (0, -QBLK, -2 * QBLK)])


def _bias_kernel(tab_ref, idx_ref, o_ref, *, col0):
    col = pl.program_id(0) + col0
    idx = idx_ref[...]
    acc = jnp.full(idx.shape, NEG_INF, F32)
    for b in range(N_BUCKETS):
        acc = jnp.where(idx == b, tab_ref[b, col] * LOG2E, acc)
    o_ref[...] = acc


def _bias_tables(rel_bias, idx_np, col0, n_heads):
    nv, r, c = idx_np.shape
    return pl.pallas_call(
        functools.partial(_bias_kernel, col0=col0),
        out_shape=jax.ShapeDtypeStruct((n_heads, nv, r, c), F32),
        grid=(n_heads, nv),
        in_specs=[pl.BlockSpec(memory_space=pltpu.SMEM),
                  pl.BlockSpec((None, r, c), lambda h, v: (v, 0, 0))],
        out_specs=pl.BlockSpec((None, None, r, c), lambda h, v: (h, v, 0, 0)),
        compiler_params=_cparams(("arbitrary", "arbitrary")),
        name="bias_tables",
    )(rel_bias, jnp.asarray(idx_np))


IN_TM = 1024
IN_CHAINS = 4


def _in_proj_kernel(x_ref, g1_ref, wa_ref, wb_ref, wc_ref, qg_ref, kvg_ref, wq_ref, wqr_ref, wkv_ref,
                    bscale_ref, cscale_ref, vones_ref, cq_ref, sq_ref, ck_ref, sk_ref,
                    qa_ref, ka_ref, va_ref, zb_ref, zc_ref):
    sub = x_ref.shape[0] // IN_CHAINS

    def rows(r):
        return slice(r * sub, (r + 1) * sub)

    def project(r, _):
        h = _rms(x_ref[rows(r), :], g1_ref[...]).astype(BF16)
        za = _dot(h, wa_ref[...])
        zb_ref[rows(r), :] = (_dot(h, wb_ref[...]) * bscale_ref[...]).astype(BF16)
        zc_ref[rows(r), :] = (_dot(h, wc_ref[...]) * cscale_ref[...]).astype(BF16)
        return za

    def expand(r, za):
        hq = _rms(za[:, :A_Q_LORA], qg_ref[...]).astype(BF16)
        hkv = _rms(za[:, A_Q_LORA:A_Q_LORA + A_KV_LORA], kvg_ref[...]).astype(BF16)
        kpe = za[:, A_Q_LORA + A_KV_LORA:]
        kpe_rot = pltpu.roll(kpe, LANES - A_ROPE, axis=1)
        k_rope = (kpe * ck_ref[rows(r), :] + kpe_rot * sk_ref[rows(r), :]).astype(BF16)
        q = _dot(hq, wq_ref[...]) * (A_SCALE * LOG2E)
        q_rot = _dot(hq, wqr_ref[...]) * (A_SCALE * LOG2E)
        kv = _dot(jnp.concatenate([hkv, k_rope], axis=1), wkv_ref[...])
        return q, q_rot, kv

    def store(r, state):
        q, q_rot, kv = state
        ka_ref[rows(r), :] = kv[:, :A_HEADS * LANES].astype(BF16)
        va_ref[rows(r), :] = (kv[:, A_HEADS * LANES:] + vones_ref[...]).astype(BF16)
        cq = cq_ref[rows(r), :]
        sq = sq_ref[rows(r), :]
        for hh in range(A_HEADS):
            sl = slice(hh * LANES, (hh + 1) * LANES)
            qa_ref[rows(r), sl] = (q[:, sl] * cq + q_rot[:, sl] * sq).astype(BF16)

    _software_pipeline(list(range(IN_CHAINS)), (project, expand, store))


def _in_proj(x2, g1, wa, wb, wc, qg, kvg, wq, wqr, wkv, bscale, cscale, vones, cq, sq, ck, sk):
    t = x2.shape[0]
    tm = IN_TM
    n_pos = SEQ // tm
    full = lambda a: pl.BlockSpec(a.shape, lambda i: (0,) * a.ndim)
    row = lambda w: pl.BlockSpec((tm, w), lambda i: (i, 0))
    pos = pl.BlockSpec((tm, LANES), lambda i: (i % n_pos, 0))
    return pl.pallas_call(
        _in_proj_kernel,
        out_shape=(jax.ShapeDtypeStruct((t, A_HEADS * LANES), BF16),
                   jax.ShapeDtypeStruct((t, A_HEADS * LANES), BF16),
                   jax.ShapeDtypeStruct((t, A_HEADS * LANES), BF16),
                   jax.ShapeDtypeStruct((t, wb.shape[1]), BF16),
                   jax.ShapeDtypeStruct((t, wc.shape[1]), BF16)),
        grid=(t // tm,),
        in_specs=[row(D_MODEL), full(g1), full(wa), full(wb), full(wc), full(qg), full(kvg), full(wq), full(wqr),
                  full(wkv), full(bscale), full(cscale), full(vones), pos, pos, pos, pos],
        out_specs=(row(A_HEADS * LANES), row(A_HEADS * LANES), row(A_HEADS * LANES), row(wb.shape[1]),
                   row(wc.shape[1])),
        compiler_params=_cparams(("parallel",)),
        name="in_proj",
    )(x2, g1, wa, wb, wc, qg, kvg, wq, wqr, wkv, bscale, cscale, vones, cq, sq, ck, sk)


A_TQ = 2048
A_SUB = 512
A_PAIRS = 2


def _mla_kernel(q_ref, k_ref, v_ref, o_ref):
    first = lax.broadcasted_iota(jnp.int32, (1, LANES), 1) < HALF
    tasks = [(r, head) for r in range(A_TQ // A_SUB) for head in range(2 * A_PAIRS)]
    done = {}

    def rows(r):
        return slice(r * A_SUB, (r + 1) * A_SUB)

    def lanes(head):
        return slice(head * LANES, (head + 1) * LANES)

    def scores(task, _):
        r, head = task
        return _dot_nt(q_ref[rows(r), lanes(head)], k_ref[:, lanes(head)])

    def values(task, s):
        p = jnp.exp2(s - jnp.max(s, axis=-1, keepdims=True))
        return _dot(p.astype(BF16), v_ref[:, lanes(task[1])])

    def store(task, a):
        r, head = task
        done[task] = a
        if head % 2 == 1:
            a0 = done.pop((r, head - 1))
            a1 = done.pop((r, head))
            l = pltpu.roll(jnp.where(first, a1, a0), HALF, axis=1)
            o_ref[rows(r), lanes(head // 2)] = (jnp.where(first, a0, a1) / l).astype(BF16)

    _software_pipeline(tasks, (scores, values, store))


def _mla_attention(qa, ka, va, batch):
    t = qa.shape[0]
    nq = SEQ // A_TQ
    return pl.pallas_call(
        _mla_kernel,
        out_shape=jax.ShapeDtypeStruct((t, A_HEADS * A_V), BF16),
        grid=(batch, A_HEADS // (2 * A_PAIRS), nq),
        in_specs=[pl.BlockSpec((A_TQ, 2 * A_PAIRS * LANES), lambda b, p, i: (b * nq + i, p)),
                  pl.BlockSpec((SEQ, 2 * A_PAIRS * LANES), lambda b, p, i: (b, p)),
                  pl.BlockSpec((SEQ, 2 * A_PAIRS * LANES), lambda b, p, i: (b, p))],
        out_specs=pl.BlockSpec((A_TQ, A_PAIRS * LANES), lambda b, p, i: (b * nq + i, p)),
        compiler_params=_cparams(("parallel", "parallel", "arbitrary")),
        name="mla_attention",
    )(qa, ka, va)


def _software_pipeline(tasks, stages):
    state = [None] * len(tasks)
    for step in range(len(tasks) + len(stages) - 1):
        for s, stage in enumerate(stages):
            i = step - s
            if 0 <= i < len(tasks):
                state[i] = stage(tasks[i], state[i])


def _first_half():
    return lax.broadcasted_iota(jnp.int32, (1, LANES), 1) < HALF


def _pair_scores(q2, k2, biases):
    first = _first_half()
    zero = jnp.zeros_like(q2)
    tq = q2.shape[0]
    stacked = jnp.concatenate([jnp.where(first, q2, zero), jnp.where(first, zero, q2)], axis=0)
    s = _dot_nt(stacked, k2)
    return [s[:tq] + biases[0], s[tq:] + biases[1]]


def _pair_values(scores, v2, sinks=None):
    first = _first_half()
    one = jnp.ones_like(v2)
    res = []
    for hh, s in enumerate(scores):
        own = first if hh == 0 else jnp.logical_not(first)
        m = jnp.max(s, axis=-1, keepdims=True)
        if sinks is not None:
            m = jnp.maximum(m, sinks[hh])
        a = _dot(jnp.exp2(s - m).astype(BF16), jnp.where(own, v2, one))
        if sinks is not None:
            a = a + jnp.where(own, 0.0, jnp.exp2(sinks[hh] - m))
        res.append((m, a))
    return res


def _pair_merge(res):
    first = _first_half()
    (m0, a0), (m1, a1) = res
    return jnp.where(first, m0, m1), jnp.where(first, a1, a0), jnp.where(first, a0, a1)


def _dilated_kernel(q1, k1, v1, bw_ref, bn_ref, o_ref, nat32, by4_32, by4, by16, stream, by4_out, merged):
    n_blocks = SEQ // QBLK
    d4, d16 = B_GROUPS[1][1], B_GROUPS[2][1]
    len4, len16 = SEQ // d4, SEQ // d16
    for t, ref in enumerate((q1, k1, v1)):
        nat32[t] = ref[...].astype(F32)
        for g in range(d4):
            run = nat32[t, pl.ds(g, len4, stride=d4), :]
            by4_32[t, g * len4:(g + 1) * len4, :] = run
            by4[t, g] = run.astype(BF16)
        for g in range(d16):
            start = (g % d4) * len4 + g // d4
            by16[t, g] = by4_32[t, pl.ds(start, len16, stride=d4), :].astype(BF16)
    sources = ((q1, k1, v1), tuple(by4.at[t] for t in range(3)), tuple(by16.at[t] for t in range(3)))

    def rows(j):
        return slice(j * QBLK, (j + 1) * QBLK)

    def view(ref, gi, sl):
        dil = B_GROUPS[gi][1]
        if dil == 1:
            return ref[sl, :]
        length = SEQ // dil
        g = sl.start // length
        return ref[g, sl.start - g * length:sl.stop - g * length, :]

    def keys(gi, j):
        length = SEQ // B_GROUPS[gi][1]
        per_stream = length // QBLK
        if per_stream == 1:
            return rows(j)
        i = j % per_stream
        start = (j - i) * QBLK + min(max(i * QBLK - B_RADIUS, 0), length - 2 * QBLK)
        return slice(start, start + 2 * QBLK)

    def scores(task, _):
        gi, j = task
        per_stream = SEQ // B_GROUPS[gi][1] // QBLK
        if per_stream == 1:
            biases = (bn_ref[0, 0], bn_ref[1, 0])
        else:
            i = j % per_stream
            var = gi * 3 + (0 if i == 0 else 2 if i == per_stream - 1 else 1)
            biases = (bw_ref[0, var], bw_ref[1, var])
        q_ref, k_ref, _ = sources[gi]
        return _pair_scores(view(q_ref, gi, rows(j)), view(k_ref, gi, keys(gi, j)), biases)

    def values(task, s):
        gi, j = task
        return _pair_values(s, view(sources[gi][2], gi, keys(gi, j)))

    def store(task, res):
        gi, j = task
        for slot, value in enumerate(_pair_merge(res)):
            if gi == 0:
                merged[0, slot, rows(j), :] = value
            else:
                stream[gi - 1, slot, rows(j), :] = value

    _software_pipeline([(gi, j) for gi in range(len(B_GROUPS)) for j in range(n_blocks)], (scores, values, store))
    for slot in range(3):
        for g in range(d16):
            start = (g % d4) * len4 + g // d4
            by4_out[slot, pl.ds(start, len16, stride=d4), :] = stream[1, slot, g * len16:(g + 1) * len16, :]
        for g in range(d4):
            run = slice(g * len4, (g + 1) * len4)
            merged[1, slot, pl.ds(g, len4, stride=d4), :] = stream[0, slot, run, :]
            merged[2, slot, pl.ds(g, len4, stride=d4), :] = by4_out[slot, run, :]

    def merge(c, carry):
        rows = pl.ds(pl.multiple_of(c * QBLK, QBLK), QBLK)
        m = [merged[gi, 0, rows, :] for gi in range(3)]
        top = jnp.maximum(jnp.maximum(m[0], m[1]), m[2])
        w = [jnp.exp2(mg - top) for mg in m]
        l = [pltpu.roll(merged[gi, 1, rows, :], HALF, axis=1) for gi in range(3)]
        num = w[0] * merged[0, 2, rows, :] + w[1] * merged[1, 2, rows, :] + w[2] * merged[2, 2, rows, :]
        den = w[0] * l[0] + w[1] * l[1] + w[2] * l[2]
        o_ref[rows, :] = (num / den).astype(BF16)
        return carry

    lax.fori_loop(0, n_blocks, merge, 0, unroll=4)


def _dilated_attention(zb, bias_wide, bias_narrow, batch):
    t = zb.shape[0]
    n_pairs = B_HEADS // 2
    nw = bias_wide.shape[1]
    d4, d16 = B_GROUPS[1][1], B_GROUPS[2][1]
    seq_block = lambda off: pl.BlockSpec((SEQ, LANES), lambda p, b: (b, off + p))
    return pl.pallas_call(
        _dilated_kernel,
        out_shape=jax.ShapeDtypeStruct((t, B_HEADS * B_HEAD_DIM), BF16),
        grid=(n_pairs, batch),
        in_specs=[seq_block(0), seq_block(n_pairs), seq_block(2 * n_pairs),
                  pl.BlockSpec((2, nw, QBLK, 2 * QBLK), lambda p, b: (p, 0, 0, 0)),
                  pl.BlockSpec((2, 1, QBLK, QBLK), lambda p, b: (p, 0, 0, 0))],
        out_specs=pl.BlockSpec((SEQ, LANES), lambda p, b: (b, p)),
        scratch_shapes=[pltpu.VMEM((3, SEQ, LANES), F32), pltpu.VMEM((3, SEQ, LANES), F32),
                        pltpu.VMEM((3, d4, SEQ // d4, LANES), BF16), pltpu.VMEM((3, d16, SEQ // d16, LANES), BF16),
                        pltpu.VMEM((2, 3, SEQ, LANES), F32), pltpu.VMEM((3, SEQ, LANES), F32),
                        pltpu.VMEM((3, 3, SEQ, LANES), F32)],
        compiler_params=_cparams(("arbitrary", "arbitrary")),
        name="dilated_attention",
    )(zb, zb, zb, bias_wide, bias_narrow)


C_SEQS = 2


def _window_kernel(sink_ref, q_ref, k_ref, v_ref, bias_ref, o_ref):
    pair = pl.program_id(0)
    sinks = (sink_ref[pair] * LOG2E, sink_ref[pair + C_HEADS // 2] * LOG2E)
    n_blocks = SEQ // QBLK

    def keys(i):
        base, local = (i // n_blocks) * SEQ, i % n_blocks
        start = base + min(max((local - 1) * QBLK, 0), SEQ - 3 * QBLK)
        return slice(start, start + 3 * QBLK)

    def scores(i, _):
        local = i % n_blocks
        var = 0 if local == 0 else 2 if local == n_blocks - 1 else 1
        return _pair_scores(q_ref[i * QBLK:(i + 1) * QBLK, :], k_ref[keys(i), :],
                            (bias_ref[0, var], bias_ref[1, var]))

    def values(i, s):
        return _pair_values(s, v_ref[keys(i), :], sinks)

    def store(i, res):
        _, l_swapped, o = _pair_merge(res)
        o_ref[i * QBLK:(i + 1) * QBLK, :] = (o / pltpu.roll(l_swapped, HALF, axis=1)).astype(BF16)

    _software_pipeline(list(range(C_SEQS * n_blocks)), (scores, values, store))


def _window_attention(zc, sink, bias, batch):
    assert batch % C_SEQS == 0
    t = zc.shape[0]
    n_pairs = C_HEADS // 2
    return pl.pallas_call(
        _window_kernel,
        out_shape=jax.ShapeDtypeStruct((t, C_HEADS * C_HEAD_DIM), BF16),
        grid=(n_pairs, batch // C_SEQS),
        in_specs=[pl.BlockSpec(memory_space=pltpu.SMEM),
                  pl.BlockSpec((C_SEQS * SEQ, LANES), lambda p, b: (b, p)),
                  pl.BlockSpec((C_SEQS * SEQ, LANES), lambda p, b: (b, n_pairs)),
                  pl.BlockSpec((C_SEQS * SEQ, LANES), lambda p, b: (b, n_pairs + 1)),
                  pl.BlockSpec((None, 2, 3, QBLK, 3 * QBLK), lambda p, b: (p, 0, 0, 0, 0))],
        out_specs=pl.BlockSpec((C_SEQS * SEQ, LANES), lambda p, b: (b, p)),
        compiler_params=_cparams(("arbitrary", "arbitrary")),
        name="window_attention",
    )(sink, zc, zc, zc, bias)


MERGE_TM = 1024
MERGE_CHAINS = 4


def _merge_kernel(x_ref, g1_ref, ya_ref, yb_ref, yc_ref, wg_ref, wbr_ref, wo_ref, o_ref):
    sub = x_ref.shape[0] // MERGE_CHAINS

    def rows(r):
        return slice(r * sub, (r + 1) * sub)

    def gated_sum(r, _):
        h = _rms(x_ref[rows(r), :], g1_ref[...]).astype(BF16)
        merged = None
        for b, y_ref in enumerate((ya_ref, yb_ref, yc_ref)):
            gate = jax.nn.sigmoid(_dot(h, wg_ref[:, b * D_MODEL:(b + 1) * D_MODEL]))
            term = gate * _dot(y_ref[rows(r), :], wbr_ref[b])
            merged = term if merged is None else merged + term
        return merged.astype(BF16)

    def project(r, merged):
        o_ref[rows(r), :] = x_ref[rows(r), :] + _dot(merged, wo_ref[...])

    _software_pipeline(list(range(MERGE_CHAINS)), (gated_sum, project))


def _merge(x2, g1, ya, yb, yc, wg, wbr, wo):
    t = x2.shape[0]
    tm = MERGE_TM
    full = lambda a: pl.BlockSpec(a.shape, lambda i: (0,) * a.ndim)
    row = lambda w: pl.BlockSpec((tm, w), lambda i: (i, 0))
    return pl.pallas_call(
        _merge_kernel,
        out_shape=jax.ShapeDtypeStruct((t, D_MODEL), F32),
        grid=(t // tm,),
        in_specs=[row(D_MODEL), full(g1), row(ya.shape[1]), row(yb.shape[1]), row(yc.shape[1]),
                  full(wg), full(wbr), full(wo)],
        out_specs=row(D_MODEL),
        compiler_params=_cparams(("parallel",)),
        name="gate_merge",
    )(x2, g1, ya, yb, yc, wg, wbr, wo)


TOP_K = 2
ROW_SUBLANES = D_MODEL // LANES
MOE_TM = 512
ROUTER_TM = 2048
ROUTER_CHAINS = 4
EXPERT_CHAINS = 2


def _router_kernel(x_ref, g_ref, wr_ref, eid_ref, rank_ref, gate_ref, cnt_ref, carry):
    @pl.when(pl.program_id(0) == 0)
    def _():
        carry[...] = jnp.zeros_like(carry)

    sub = x_ref.shape[0] // ROUTER_CHAINS
    w = wr_ref[...]
    w_hi = w.astype(BF16)
    w_lo = (w - w_hi.astype(F32)).astype(BF16)
    earlier = (lax.broadcasted_iota(jnp.int32, (sub, sub), 0) > lax.broadcasted_iota(jnp.int32, (sub, sub), 1))
    earlier = jnp.where(earlier, 1.0, 0.0).astype(BF16)

    def rows(r):
        return slice(r * sub, (r + 1) * sub)

    def logits(r, _):
        h = _rms(x_ref[rows(r), :], g_ref[...])
        h_hi = h.astype(BF16)
        h_lo = (h - h_hi.astype(F32)).astype(BF16)
        return _dot(h_hi, w_hi) + (_dot(h_lo, w_hi) + _dot(h_hi, w_lo))

    def select(r, lg):
        idx = lax.broadcasted_iota(jnp.int32, lg.shape, 1)
        m1 = jnp.max(lg, axis=-1, keepdims=True)
        i1 = jnp.min(jnp.where(lg == m1, idx, N_EXPERTS), axis=-1, keepdims=True)
        first = idx == i1
        rest = jnp.where(first, -jnp.inf, lg)
        m2 = jnp.max(rest, axis=-1, keepdims=True)
        i2 = jnp.min(jnp.where(rest == m2, idx, N_EXPERTS), axis=-1, keepdims=True)
        second = idx == i2
        e = jnp.exp(m2 - m1)
        chosen = jnp.where(jnp.logical_or(first, second), 1.0, 0.0)
        return i1, i2, first, second, e, chosen, _dot(earlier, chosen.astype(BF16))

    def emit(r, state):
        i1, i2, first, second, e, chosen, before = state
        before = before + carry[...]
        r1 = jnp.sum(jnp.where(first, before, 0.0), axis=-1, keepdims=True)
        r2 = jnp.sum(jnp.where(second, before, 0.0), axis=-1, keepdims=True)
        carry[...] += jnp.sum(chosen, axis=0, keepdims=True)
        den = 1.0 + e
        slot0 = lax.broadcasted_iota(jnp.int32, (sub, TOP_K), 1) == 0
        eid_ref[rows(r), :] = jnp.where(slot0, i1, i2)
        rank_ref[rows(r), :] = jnp.where(slot0, r1, r2).astype(jnp.int32)
        gate_ref[rows(r), :] = jnp.where(slot0, 1.0 / den, e / den)

    _software_pipeline(list(range(ROUTER_CHAINS)), (logits, select, emit))
    cnt_ref[...] = carry[...].astype(jnp.int32)


def _router(x2, g, wr):
    t = x2.shape[0]
    tm = ROUTER_TM
    pair = pl.BlockSpec((tm, TOP_K), lambda i: (i, 0))
    return pl.pallas_call(
        _router_kernel,
        out_shape=(jax.ShapeDtypeStruct((t, TOP_K), jnp.int32), jax.ShapeDtypeStruct((t, TOP_K), jnp.int32),
                   jax.ShapeDtypeStruct((t, TOP_K), F32), jax.ShapeDtypeStruct((1, N_EXPERTS), jnp.int32)),
        grid=(t // tm,),
        in_specs=[pl.BlockSpec((tm, D_MODEL), lambda i: (i, 0)),
                  pl.BlockSpec(g.shape, lambda i: (0, 0)),
                  pl.BlockSpec(wr.shape, lambda i: (0, 0))],
        out_specs=(pair, pair, pair, pl.BlockSpec((1, N_EXPERTS), lambda i: (0, 0))),
        scratch_shapes=[pltpu.VMEM((1, N_EXPERTS), F32)],
        compiler_params=_cparams(("arbitrary",)),
        name="router",
    )(x2, g, wr)


def _row_tile(ref, row):
    return ref.at[pl.ds(pl.multiple_of(row * ROW_SUBLANES, ROW_SUBLANES), ROW_SUBLANES), :]


def _dispatch_kernel(tail_start_ref, tail_len_ref, na_ref, dest_ref, x_ref, g_ref, hs_ref, stage, sems):
    i = pl.program_id(0)
    n_steps = pl.num_programs(0)
    tm = x_ref.shape[0]
    slot = i % 2
    rows_per_slot = stage.shape[1]

    def drain(s):
        for _ in range(TOP_K):
            pltpu.make_async_copy(stage.at[s], hs_ref.at[pl.ds(0, rows_per_slot), :], sems.at[s]).wait()

    @pl.when(i >= 2)
    def _():
        drain(slot)

    h = _rms(x_ref[...], g_ref[...])
    for s in range(ROW_SUBLANES):
        stage[slot, pl.ds(s, tm, stride=ROW_SUBLANES), :] = h[:, s * LANES:(s + 1) * LANES]

    def start(t, carry):
        for k in range(TOP_K):
            pltpu.make_async_copy(_row_tile(stage.at[slot], t), _row_tile(hs_ref, dest_ref[TOP_K * t + k]),
                                  sems.at[slot]).start(priority=k)
        return carry

    lax.fori_loop(0, tm, start, 0, unroll=8)

    @pl.when(i == n_steps - 1)
    def _():
        @pl.when(n_steps >= 2)
        def _():
            drain(1 - slot)

        drain(slot)
        zeros = stage.at[0]
        zeros[...] = jnp.zeros_like(zeros)
        for e in range(N_EXPERTS):
            def tail(r, e=e):
                return pltpu.make_async_copy(_row_tile(zeros, 0), _row_tile(hs_ref, tail_start_ref[e] + r), sems.at[0])

            lax.fori_loop(0, tail_len_ref[e], lambda r, c: (tail(r).start(), c)[1], 0)
            lax.fori_loop(0, tail_len_ref[e], lambda r, c: (tail(r).wait(), c)[1], 0)
        n_tiles = hs_ref.shape[0] // rows_per_slot

        def spare(j):
            rows = pl.ds(pl.multiple_of(j * rows_per_slot, rows_per_slot), rows_per_slot)
            return pltpu.make_async_copy(zeros, hs_ref.at[rows, :], sems.at[0])

        lax.fori_loop(na_ref[0], n_tiles, lambda j, c: (spare(j).start(), c)[1], 0)
        lax.fori_loop(na_ref[0], n_tiles, lambda j, c: (spare(j).wait(), c)[1], 0)


def _dispatch(x2, g, dest, tail_start, tail_len, n_active, n_rows):
    t = x2.shape[0]
    tm = MOE_TM
    return pl.pallas_call(
        _dispatch_kernel,
        out_shape=jax.ShapeDtypeStruct((n_rows * ROW_SUBLANES, LANES), F32),
        grid_spec=pltpu.PrefetchScalarGridSpec(
            num_scalar_prefetch=3,
            grid=(t // tm,),
            in_specs=[pl.BlockSpec((TOP_K * tm,), lambda i, ts, tl, na: (i,), memory_space=pltpu.SMEM),
                      pl.BlockSpec((tm, D_MODEL), lambda i, ts, tl, na: (i, 0)),
                      pl.BlockSpec(g.shape, lambda i, ts, tl, na: (0, 0))],
            out_specs=pl.BlockSpec(memory_space=pl.ANY),
            scratch_shapes=[pltpu.VMEM((2, tm * ROW_SUBLANES, LANES), F32), pltpu.SemaphoreType.DMA((2,))]),
        compiler_params=_cparams(("arbitrary",)),
        name="dispatch",
    )(tail_start, tail_len, n_active, dest, x2, g)


def _experts_kernel(te_ref, na_ref, hs_ref, wg_ref, wu_ref, wd_ref, ys_ref):
    active = pl.program_id(0) < na_ref[0]
    sub = hs_ref.shape[0] // ROW_SUBLANES // EXPERT_CHAINS

    def lane_groups(r):
        return [pl.ds(r * sub * ROW_SUBLANES + s, sub, stride=ROW_SUBLANES) for s in range(ROW_SUBLANES)]

    def up(r, _):
        h = jnp.concatenate([hs_ref[g, :].astype(BF16) for g in lane_groups(r)], axis=1)
        return (jax.nn.silu(_dot(h, wg_ref[...])) * _dot(h, wu_ref[...])).astype(BF16)

    def down(r, act):
        return _dot(act, wd_ref[...])

    def store(r, y):
        for s, g in enumerate(lane_groups(r)):
            ys_ref[g, :] = y[:, s * LANES:(s + 1) * LANES]

    @pl.when(active)
    def _():
        _software_pipeline(list(range(EXPERT_CHAINS)), (up, down, store))

    @pl.when(jnp.logical_not(active))
    def _():
        ys_ref[...] = jnp.zeros_like(ys_ref)


def _experts(hs, tile_expert, n_active, wg, wu, wd):
    tm = MOE_TM
    n_tiles = hs.shape[0] // (tm * ROW_SUBLANES)
    d_ff = wg.shape[2]
    once = pl.Buffered(1)
    return pl.pallas_call(
        _experts_kernel,
        out_shape=jax.ShapeDtypeStruct(hs.shape, F32),
        grid_spec=pltpu.PrefetchScalarGridSpec(
            num_scalar_prefetch=2,
            grid=(n_tiles,),
            in_specs=[pl.BlockSpec((tm * ROW_SUBLANES, LANES), lambda i, te, na: (jnp.minimum(i, na[0] - 1), 0)),
                      pl.BlockSpec((None, D_MODEL, d_ff), lambda i, te, na: (te[i], 0, 0), pipeline_mode=once),
                      pl.BlockSpec((None, D_MODEL, d_ff), lambda i, te, na: (te[i], 0, 0), pipeline_mode=once),
                      pl.BlockSpec((None, d_ff, D_MODEL), lambda i, te, na: (te[i], 0, 0), pipeline_mode=once)],
            out_specs=pl.BlockSpec((tm * ROW_SUBLANES, LANES), lambda i, te, na: (i, 0))),
        compiler_params=_cparams(("arbitrary",)),
        name="experts",
    )(tile_expert, n_active, hs, wg, wu, wd)


def _combine_kernel(dest_ref, next_dest_ref, x_ref, gate_ref, gf_ref, ys_ref, o_ref, bufs, sems, *, final):
    i = pl.program_id(0)
    tm = x_ref.shape[0]
    slot = i % 2

    def gather(idx_ref, s):
        def start(t, carry):
            for k in range(TOP_K):
                pltpu.make_async_copy(_row_tile(ys_ref, idx_ref[TOP_K * t + k]), _row_tile(bufs.at[s, k], t),
                                      sems.at[s]).start(priority=k)
            return carry

        lax.fori_loop(0, tm, start, 0, unroll=8)

    @pl.when(i == 0)
    def _():
        gather(dest_ref, slot)

    @pl.when(i + 1 < pl.num_programs(0))
    def _():
        gather(next_dest_ref, 1 - slot)

    for k in range(TOP_K):
        pltpu.make_async_copy(ys_ref.at[pl.ds(0, bufs.shape[2]), :], bufs.at[slot, k], sems.at[slot]).wait()
    gate = gate_ref[...]
    g0 = gate[:, 0:1]
    g1 = gate[:, 1:2]
    ss = jnp.zeros((tm, 1), F32)
    for s in range(ROW_SUBLANES):
        sl = slice(s * LANES, (s + 1) * LANES)
        rows = pl.ds(s, tm, stride=ROW_SUBLANES)
        y = x_ref[:, sl] + (g0 * bufs[slot, 0, rows, :] + g1 * bufs[slot, 1, rows, :])
        o_ref[:, sl] = y
        ss = ss + jnp.sum(y * y, axis=-1, keepdims=True)
    if final:
        o_ref[...] = o_ref[...] * lax.rsqrt(ss / D_MODEL + EPS) * gf_ref[...]


def _combine(x2, dest, gates, gf, ys, *, final):
    t = x2.shape[0]
    tm = MOE_TM
    return pl.pallas_call(
        functools.partial(_combine_kernel, final=final),
        out_shape=jax.ShapeDtypeStruct((t, D_MODEL), F32),
        grid=(t // tm,),
        in_specs=[pl.BlockSpec((TOP_K * tm,), lambda i: (i,), memory_space=pltpu.SMEM),
                  pl.BlockSpec((TOP_K * tm,), lambda i: (jnp.minimum(i + 1, t // tm - 1),),
                               memory_space=pltpu.SMEM),
                  pl.BlockSpec((tm, D_MODEL), lambda i: (i, 0)),
                  pl.BlockSpec((tm, TOP_K), lambda i: (i, 0)),
                  pl.BlockSpec(gf.shape, lambda i: (0, 0)),
                  pl.BlockSpec(memory_space=pl.ANY)],
        out_specs=pl.BlockSpec((tm, D_MODEL), lambda i: (i, 0)),
        scratch_shapes=[pltpu.VMEM((2, TOP_K, tm * ROW_SUBLANES, LANES), F32), pltpu.SemaphoreType.DMA((2,))],
        compiler_params=_cparams(("arbitrary",)),
        name="combine",
    )(dest, dest, x2, gates, gf, ys)


def _moe(x2, g, wr, wg, wu, wd, gf, *, final):
    t = x2.shape[0]
    tm = MOE_TM
    eid, rank, gates, counts = _router(x2, g, wr)
    counts = counts[0]
    padded = (counts + tm - 1) // tm * tm
    group_end = jnp.cumsum(padded)
    group_start = group_end - padded
    dest = (group_start[eid] + rank).reshape(-1)
    n_tiles = TOP_K * t // tm + N_EXPERTS
    n_active = (group_end[-1:] // tm).astype(jnp.int32)
    tile_row = jnp.arange(n_tiles, dtype=jnp.int32) * tm
    tile_row = jnp.minimum(tile_row, group_end[-1] - tm)
    tile_expert = jnp.sum(tile_row[:, None] >= group_end[None, :], axis=1).astype(jnp.int32)
    hs = _dispatch(x2, g, dest, group_start + counts, padded - counts, n_active, n_tiles * tm)
    ys = _experts(hs, tile_expert, n_active, wg, wu, wd)
    return _combine(x2, dest, gates, gf, ys, final=final)


FFN_TM = 1024
FFN_CHAINS = 4


def _ffn_kernel(x_ref, g_ref, wg_ref, wu_ref, wd_ref, gf_ref, o_ref, *, final):
    sub = x_ref.shape[0] // FFN_CHAINS

    def rows(r):
        return slice(r * sub, (r + 1) * sub)

    def up(r, _):
        h = _rms(x_ref[rows(r), :], g_ref[...]).astype(BF16)
        return (jax.nn.silu(_dot(h, wg_ref[...])) * _dot(h, wu_ref[...])).astype(BF16)

    def down(r, act):
        y = x_ref[rows(r), :] + _dot(act, wd_ref[...])
        if final:
            y = _rms(y, gf_ref[...])
        o_ref[rows(r), :] = y

    _software_pipeline(list(range(FFN_CHAINS)), (up, down))


def _ffn(x2, g, wg, wu, wd, gf, *, final):
    t = x2.shape[0]
    tm = FFN_TM
    full = lambda a: pl.BlockSpec(a.shape, lambda i: (0,) * a.ndim)
    return pl.pallas_call(
        functools.partial(_ffn_kernel, final=final),
        out_shape=jax.ShapeDtypeStruct((t, D_MODEL), F32),
        grid=(t // tm,),
        in_specs=[pl.BlockSpec((tm, D_MODEL), lambda i: (i, 0)), full(g), full(wg), full(wu), full(wd), full(gf)],
        out_specs=pl.BlockSpec((tm, D_MODEL), lambda i: (i, 0)),
        compiler_params=_cparams(("parallel",)),
        name="swiglu",
    )(x2, g, wg, wu, wd, gf)


def _rot_cols(w):
    half = A_ROPE // 2
    return jnp.concatenate([-w[..., half:], w[..., :half]], axis=-1)


def _prep_layer(w_in, w_uq, w_ukv, w_br_a, w_br_b, w_br_c):
    o = 0
    w_cq = w_in[:, o:o + A_Q_LORA]; o += A_Q_LORA
    w_ckv = w_in[:, o:o + A_KV_LORA]; o += A_KV_LORA
    w_kpe = w_in[:, o:o + A_ROPE]; o += A_ROPE
    nb = B_HEADS * B_HEAD_DIM
    w_b = w_in[:, o:o + 3 * nb]; o += 3 * nb
    nq = C_HEADS * C_HEAD_DIM
    nkv = C_KV_HEADS * C_HEAD_DIM
    w_c = w_in[:, o:o + nq + 2 * nkv]; o += nq + 2 * nkv
    w_g = w_in[:, o:]
    pad = jnp.zeros((D_MODEL, LANES - 2 * A_ROPE), F32)
    wa = jnp.concatenate([w_cq, w_ckv, w_kpe, _rot_cols(w_kpe), pad], axis=1)
    wb = jnp.concatenate([w_b[:, :nb] * (B_HEAD_DIM ** -0.5), w_b[:, nb:]], axis=1)
    q_c = w_c[:, :nq].reshape(D_MODEL, 2, C_HEADS // 2, C_HEAD_DIM).transpose(0, 2, 1, 3).reshape(D_MODEL, nq)
    wc = jnp.concatenate([q_c * (C_HEAD_DIM ** -0.5), w_c[:, nq:]], axis=1)
    uq = w_uq.reshape(A_Q_LORA, A_HEADS, A_NOPE + A_ROPE)
    z32 = jnp.zeros((A_Q_LORA, A_HEADS, LANES - A_NOPE - A_ROPE), F32)
    wq = jnp.concatenate([uq, z32], axis=-1).reshape(A_Q_LORA, A_HEADS * LANES)
    wqr = jnp.concatenate([jnp.zeros((A_Q_LORA, A_HEADS, A_NOPE), F32), _rot_cols(uq[..., A_NOPE:]), z32],
                          axis=-1).reshape(A_Q_LORA, A_HEADS * LANES)
    ukv = w_ukv.reshape(A_KV_LORA, A_HEADS, A_NOPE + A_V)
    wk = jnp.concatenate([ukv[..., :A_NOPE], jnp.zeros((A_KV_LORA, A_HEADS, LANES - A_NOPE), F32)],
                         axis=-1).reshape(A_KV_LORA, A_HEADS * LANES)
    uv = ukv[..., A_NOPE:].reshape(A_KV_LORA, A_HEADS // 2, 2, A_V)
    zv = jnp.zeros_like(uv[:, :, 0])
    wv = jnp.stack([jnp.concatenate([uv[:, :, 0], zv], axis=-1), jnp.concatenate([zv, uv[:, :, 1]], axis=-1)],
                   axis=2).reshape(A_KV_LORA, A_HEADS * LANES)
    w_br_c = w_br_c.reshape(2, C_HEADS // 2, C_HEAD_DIM, D_MODEL).transpose(1, 0, 2, 3).reshape(nq, D_MODEL)
    wbr = jnp.stack([w_br_a, w_br_b, w_br_c])
    wkv = jnp.concatenate([jnp.concatenate([wk, wv], axis=1),
                           jnp.concatenate([_rope_placement(), jnp.zeros((LANES, A_HEADS * LANES), F32)], axis=1)])
    cast = lambda a: a.astype(BF16)
    return dict(wa=cast(wa), wb=cast(wb), wc=cast(wc), wg=cast(w_g), wq=cast(wq), wqr=cast(wqr), wkv=cast(wkv),
                wbr=cast(wbr))


def _rope_tables():
    half = A_ROPE // 2
    inv = ROPE_THETA ** (-jnp.arange(half, dtype=F32) / half)
    ang = jnp.arange(SEQ, dtype=F32)[:, None] * inv[None, :]
    cos2 = jnp.tile(jnp.cos(ang), (1, 2))
    sin2 = jnp.tile(jnp.sin(ang), (1, 2))
    ones = jnp.ones((SEQ, A_NOPE), F32)
    z = lambda w: jnp.zeros((SEQ, w), F32)
    cq = jnp.concatenate([ones, cos2, z(LANES - A_NOPE - A_ROPE)], axis=1)
    sq = jnp.concatenate([z(A_NOPE), sin2, z(LANES - A_NOPE - A_ROPE)], axis=1)
    ck = jnp.concatenate([cos2, z(LANES - A_ROPE)], axis=1)
    sk = jnp.concatenate([sin2, z(LANES - A_ROPE)], axis=1)
    return cq, sq, ck, sk


def _rope_placement():
    place = np.zeros((LANES, A_HEADS * LANES), np.float32)
    for h in range(A_HEADS):
        for j in range(A_ROPE):
            place[j, h * LANES + A_NOPE + j] = 1.0
    return jnp.asarray(place)


def kernel(x, norm1_g, w_in, q_norm_g, w_uq, kv_norm_g, w_ukv, sink_logit, rel_bias, w_branch_a, w_branch_b,
           w_branch_c, w_out, norm2_g, ffn_w_gate, ffn_w_up, ffn_w_down, router_w, exp_w_gate, exp_w_up,
           exp_w_down, final_g):
    batch, seq, d = x.shape
    assert seq == SEQ and d == D_MODEL
    t = batch * seq
    x2 = x.reshape(t, d)

    wide_idx, narrow_idx = _dilated_index_tables()
    bias_wide = _bias_tables(rel_bias, wide_idx, 0, B_HEADS)
    bias_narrow = _bias_tables(rel_bias, narrow_idx, 0, B_HEADS)
    bias_c = _bias_tables(rel_bias, _window_index_tables(), B_HEADS, C_HEADS)
    bias_c = bias_c.reshape(2, C_HEADS // 2, *bias_c.shape[1:]).transpose(1, 0, 2, 3, 4)

    cq, sq, ck, sk = _rope_tables()
    row = lambda v: v.reshape(1, -1)
    q_then = lambda nq, rest: jnp.asarray(np.concatenate([np.full(nq, LOG2E), np.ones(rest)])[None], F32)
    bscale = q_then(B_HEADS * B_HEAD_DIM, 2 * B_HEADS * B_HEAD_DIM)
    cscale = q_then(C_HEADS * C_HEAD_DIM, 2 * C_KV_HEADS * C_HEAD_DIM)
    vones = jnp.asarray(np.tile(np.repeat([0.0, 1.0, 1.0, 0.0], A_V), A_HEADS // 2)[None], F32)

    for l in range(DEPTH):
        w = _prep_layer(w_in[l], w_uq[l], w_ukv[l], w_branch_a[l], w_branch_b[l], w_branch_c[l])
        g1 = row(norm1_g[l])
        qa, ka, va, zb, zc = _in_proj(x2, g1, w["wa"], w["wb"], w["wc"], row(q_norm_g[l]), row(kv_norm_g[l]),
                                      w["wq"], w["wqr"], w["wkv"], bscale, cscale, vones, cq, sq, ck, sk)
        ya = _mla_attention(qa, ka, va, batch)
        yb = _dilated_attention(zb, bias_wide, bias_narrow, batch)
        yc = _window_attention(zc, sink_logit[l], bias_c, batch)
        x2 = _merge(x2, g1, ya, yb, yc, w["wg"], w["wbr"], w_out[l].astype(BF16))
        g2 = row(norm2_g[l])
        final = l == DEPTH - 1
        i = l // 2
        if l % 2 == 0:
            x2 = _ffn(x2, g2, ffn_w_gate[i].astype(BF16), ffn_w_up[i].astype(BF16), ffn_w_down[i].astype(BF16),
                      row(final_g), final=final)
        else:
            x2 = _moe(x2, g2, router_w[i], exp_w_gate[i].astype(BF16), exp_w_up[i].astype(BF16),
                      exp_w_down[i].astype(BF16), row(final_g), final=final)
    return x2.reshape(batch, seq, d)
```

```python
import functools
import math

import numpy as np
import jax
import jax.numpy as jnp
from jax import lax
from jax.experimental import pallas as pl
from jax.experimental.pallas import tpu as pltpu

F32 = jnp.float32
BF16 = jnp.bfloat16

D_MODEL = 1024
SEQ = 2048
DEPTH = 2
A_HEADS = 8
A_NOPE = 64
A_ROPE = 32
A_V = 64
A_Q_LORA = 256
A_KV_LORA = 128
ROPE_THETA = 10000.0
B_HEADS = 8
B_HEAD_DIM = 64
B_GROUPS = ((128, 1), (512, 4), (2048, 16))
C_HEADS = 8
C_KV_HEADS = 2
C_HEAD_DIM = 64
C_RADIUS = 128
N_BUCKETS = 32
MAX_DISTANCE = 1024
D_FF_DENSE = 2816
N_EXPERTS = 8
D_FF_EXPERT = 3584
EPS = 1e-6
NEG_INF = -1e30

LANES = 128
HALF = 64
QBLK = 128
B_RADIUS = 64
VMEM_LIMIT = 56 * 1024 * 1024

assert all(w // (2 * d) == B_RADIUS and SEQ % (d * QBLK) == 0 for w, d in B_GROUPS)
assert B_GROUPS[0][1] == 1 and B_GROUPS[2][1] == B_GROUPS[1][1] ** 2
assert B_HEAD_DIM == C_HEAD_DIM == A_V == HALF and C_RADIUS == QBLK == 2 * B_RADIUS

A_SCALE = (A_NOPE + A_ROPE) ** -0.5
LOG2E = math.log2(math.e)


def _cparams(sem):
    return pltpu.CompilerParams(dimension_semantics=sem, vmem_limit_bytes=VMEM_LIMIT)


def _rms(x, g):
    return x * lax.rsqrt(jnp.mean(x * x, axis=-1, keepdims=True) + EPS) * g


def _dot(a, b):
    return jnp.dot(a, b, preferred_element_type=F32)


def _dot_nt(a, b):
    return lax.dot_general(a, b, (((1,), (1,)), ((), ())), preferred_element_type=F32)


def _t5_bucket_np(rel):
    nb = N_BUCKETS // 2
    max_exact = nb // 2
    ret = np.where(rel > 0, nb, 0)
    n = np.abs(rel)
    nf = np.maximum(n, 1).astype(np.float64)
    t = np.log(nf / max_exact) / math.log(MAX_DISTANCE / max_exact) * (nb - max_exact)
    frac = np.abs(t - np.round(t))
    assert np.all((frac > 1e-5) | (n <= max_exact) | (n >= MAX_DISTANCE))
    large = max_exact + np.floor(np.round(t, 9)).astype(np.int64)
    large = np.minimum(large, nb - 1)
    return (ret + np.where(n < max_exact, n, large)).astype(np.int32)


def _band_index(tq, tk, shift, radius, dil):
    rel = np.arange(tk)[None, :] - np.arange(tq)[:, None] + shift
    idx = _t5_bucket_np(rel * dil)
    return np.where(np.abs(rel) <= radius, idx, -1).astype(np.int32)


def _dilated_index_tables():
    wide = []
    for window, dil in B_GROUPS[:2]:
        for shift in (0, -B_RADIUS, -2 * B_RADIUS):
            wide.append(_band_index(QBLK, 2 * QBLK, shift, B_RADIUS, dil))
    narrow = [_band_index(QBLK, QBLK, 0, B_RADIUS, B_GROUPS[2][1])]
    return np.stack(wide), np.stack(narrow)


def _window_index_tables():
    return np.stack([_band_index(QBLK, 3 * QBLK, shift, C_RADIUS, 1) for shift in (0, -QBLK, -2 * QBLK)])


def _bias_kernel(tab_ref, idx_ref, o_ref, *, col0):
    col = pl.program_id(0) + col0
    idx = idx_ref[...]
    acc = jnp.full(idx.shape, NEG_INF, F32)
    for b in range(N_BUCKETS):
        acc = jnp.where(idx == b, tab_ref[b, col] * LOG2E, acc)
    o_ref[...] = acc


def _bias_tables(rel_bias, idx_np, col0, n_heads):
    nv, r, c = idx_np.shape
    return pl.pallas_call(
        functools.partial(_bias_kernel, col0=col0),
        out_shape=jax.ShapeDtypeStruct((n_heads, nv, r, c), F32),
        grid=(n_heads, nv),
        in_specs=[pl.BlockSpec(memory_space=pltpu.SMEM),
                  pl.BlockSpec((None, r, c), lambda h, v: (v, 0, 0))],
        out_specs=pl.BlockSpec((None, None, r, c), lambda h, v: (h, v, 0, 0)),
        compiler_params=_cparams(("arbitrary", "arbitrary")),
        name="bias_tables",
    )(rel_bias, jnp.asarray(idx_np))


IN_TM = 1024
IN_CHAINS = 4


def _in_proj_kernel(x_ref, g1_ref, wa_ref, wb_ref, wc_ref, qg_ref, kvg_ref, wq_ref, wqr_ref, wkv_ref,
                    bscale_ref, cscale_ref, vones_ref, cq_ref, sq_ref, ck_ref, sk_ref,
                    qa_ref, ka_ref, va_ref, zb_ref, zc_ref):
    sub = x_ref.shape[0] // IN_CHAINS

    def rows(r):
        return slice(r * sub, (r + 1) * sub)

    def project(r, _):
        h = _rms(x_ref[rows(r), :], g1_ref[...]).astype(BF16)
        za = _dot(h, wa_ref[...])
        zb_ref[rows(r), :] = (_dot(h, wb_ref[...]) * bscale_ref[...]).astype(BF16)
        zc_ref[rows(r), :] = (_dot(h, wc_ref[...]) * cscale_ref[...]).astype(BF16)
        return za

    def expand(r, za):
        hq = _rms(za[:, :A_Q_LORA], qg_ref[...]).astype(BF16)
        hkv = _rms(za[:, A_Q_LORA:A_Q_LORA + A_KV_LORA], kvg_ref[...]).astype(BF16)
        kpe = za[:, A_Q_LORA + A_KV_LORA:]
        kpe_rot = pltpu.roll(kpe, LANES - A_ROPE, axis=1)
        k_rope = (kpe * ck_ref[rows(r), :] + kpe_rot * sk_ref[rows(r), :]).astype(BF16)
        q = _dot(hq, wq_ref[...]) * (A_SCALE * LOG2E)
        q_rot = _dot(hq, wqr_ref[...]) * (A_SCALE * LOG2E)
        kv = _dot(jnp.concatenate([hkv, k_rope], axis=1), wkv_ref[...])
        return q, q_rot, kv

    def store(r, state):
        q, q_rot, kv = state
        ka_ref[rows(r), :] = kv[:, :A_HEADS * LANES].astype(BF16)
        va_ref[rows(r), :] = (kv[:, A_HEADS * LANES:] + vones_ref[...]).astype(BF16)
        cq = cq_ref[rows(r), :]
        sq = sq_ref[rows(r), :]
        for hh in range(A_HEADS):
            sl = slice(hh * LANES, (hh + 1) * LANES)
            qa_ref[rows(r), sl] = (q[:, sl] * cq + q_rot[:, sl] * sq).astype(BF16)

    _software_pipeline(list(range(IN_CHAINS)), (project, expand, store))


def _in_proj(x2, g1, wa, wb, wc, qg, kvg, wq, wqr, wkv, bscale, cscale, vones, cq, sq, ck, sk):
    t = x2.shape[0]
    tm = IN_TM
    n_pos = SEQ // tm
    full = lambda a: pl.BlockSpec(a.shape, lambda i: (0,) * a.ndim)
    row = lambda w: pl.BlockSpec((tm, w), lambda i: (i, 0))
    pos = pl.BlockSpec((tm, LANES), lambda i: (i % n_pos, 0))
    return pl.pallas_call(
        _in_proj_kernel,
        out_shape=(jax.ShapeDtypeStruct((t, A_HEADS * LANES), BF16),
                   jax.ShapeDtypeStruct((t, A_HEADS * LANES), BF16),
                   jax.ShapeDtypeStruct((t, A_HEADS * LANES), BF16),
                   jax.ShapeDtypeStruct((t, wb.shape[1]), BF16),
                   jax.ShapeDtypeStruct((t, wc.shape[1]), BF16)),
        grid=(t // tm,),
        in_specs=[row(D_MODEL), full(g1), full(wa), full(wb), full(wc), full(qg), full(kvg), full(wq), full(wqr),
                  full(wkv), full(bscale), full(cscale), full(vones), pos, pos, pos, pos],
        out_specs=(row(A_HEADS * LANES), row(A_HEADS * LANES), row(A_HEADS * LANES), row(wb.shape[1]),
                   row(wc.shape[1])),
        compiler_params=_cparams(("parallel",)),
        name="in_proj",
    )(x2, g1, wa, wb, wc, qg, kvg, wq, wqr, wkv, bscale, cscale, vones, cq, sq, ck, sk)


A_TQ = 2048
A_SUB = 512
A_PAIRS = 2


def _mla_kernel(q_ref, k_ref, v_ref, o_ref):
    first = lax.broadcasted_iota(jnp.int32, (1, LANES), 1) < HALF
    tasks = [(r, head) for r in range(A_TQ // A_SUB) for head in range(2 * A_PAIRS)]
    done = {}

    def rows(r):
        return slice(r * A_SUB, (r + 1) * A_SUB)

    def lanes(head):
        return slice(head * LANES, (head + 1) * LANES)

    def scores(task, _):
        r, head = task
        return _dot_nt(q_ref[rows(r), lanes(head)], k_ref[:, lanes(head)])

    def values(task, s):
        p = jnp.exp2(s - jnp.max(s, axis=-1, keepdims=True))
        return _dot(p.astype(BF16), v_ref[:, lanes(task[1])])

    def store(task, a):
        r, head = task
        done[task] = a
        if head % 2 == 1:
            a0 = done.pop((r, head - 1))
            a1 = done.pop((r, head))
            l = pltpu.roll(jnp.where(first, a1, a0), HALF, axis=1)
            o_ref[rows(r), lanes(head // 2)] = (jnp.where(first, a0, a1) / l).astype(BF16)

    _software_pipeline(tasks, (scores, values, store))


def _mla_attention(qa, ka, va, batch):
    t = qa.shape[0]
    nq = SEQ // A_TQ
    return pl.pallas_call(
        _mla_kernel,
        out_shape=jax.ShapeDtypeStruct((t, A_HEADS * A_V), BF16),
        grid=(batch, A_HEADS // (2 * A_PAIRS), nq),
        in_specs=[pl.BlockSpec((A_TQ, 2 * A_PAIRS * LANES), lambda b, p, i: (b * nq + i, p)),
                  pl.BlockSpec((SEQ, 2 * A_PAIRS * LANES), lambda b, p, i: (b, p)),
                  pl.BlockSpec((SEQ, 2 * A_PAIRS * LANES), lambda b, p, i: (b, p))],
        out_specs=pl.BlockSpec((A_TQ, A_PAIRS * LANES), lambda b, p, i: (b * nq + i, p)),
        compiler_params=_cparams(("parallel", "parallel", "arbitrary")),
        name="mla_attention",
    )(qa, ka, va)


def _software_pipeline(tasks, stages):
    state = [None] * len(tasks)
    for step in range(len(tasks) + len(stages) - 1):
        for s, stage in enumerate(stages):
            i = step - s
            if 0 <= i < len(tasks):
                state[i] = stage(tasks[i], state[i])


def _first_half():
    return lax.broadcasted_iota(jnp.int32, (1, LANES), 1) < HALF


def _pair_scores(q2, k2, biases):
    first = _first_half()
    zero = jnp.zeros_like(q2)
    tq = q2.shape[0]
    stacked = jnp.concatenate([jnp.where(first, q2, zero), jnp.where(first, zero, q2)], axis=0)
    s = _dot_nt(stacked, k2)
    return [s[:tq] + biases[0], s[tq:] + biases[1]]


def _pair_values(scores, v2, sinks=None):
    first = _first_half()
    one = jnp.ones_like(v2)
    res = []
    for hh, s in enumerate(scores):
        own = first if hh == 0 else jnp.logical_not(first)
        m = jnp.max(s, axis=-1, keepdims=True)
        if sinks is not None:
            m = jnp.maximum(m, sinks[hh])
        a = _dot(jnp.exp2(s - m).astype(BF16), jnp.where(own, v2, one))
        if sinks is not None:
            a = a + jnp.where(own, 0.0, jnp.exp2(sinks[hh] - m))
        res.append((m, a))
    return res


def _pair_merge(res):
    first = _first_half()
    (m0, a0), (m1, a1) = res
    return jnp.where(first, m0, m1), jnp.where(first, a1, a0), jnp.where(first, a0, a1)


def _dilated_kernel(q1, k1, v1, bw_ref, bn_ref, o_ref, nat32, by4_32, by4, by16, stream, by4_out, merged):
    n_blocks = SEQ // QBLK
    d4, d16 = B_GROUPS[1][1], B_GROUPS[2][1]
    len4, len16 = SEQ // d4, SEQ // d16
    for t, ref in enumerate((q1, k1, v1)):
        nat32[t] = ref[...].astype(F32)
        for g in range(d4):
            run = nat32[t, pl.ds(g, len4, stride=d4), :]
            by4_32[t, g * len4:(g + 1) * len4, :] = run
            by4[t, g] = run.astype(BF16)
        for g in range(d16):
            start = (g % d4) * len4 + g // d4
            by16[t, g] = by4_32[t, pl.ds(start, len16, stride=d4), :].astype(BF16)
    sources = ((q1, k1, v1), tuple(by4.at[t] for t in range(3)), tuple(by16.at[t] for t in range(3)))

    def rows(j):
        return slice(j * QBLK, (j + 1) * QBLK)

    def view(ref, gi, sl):
        dil = B_GROUPS[gi][1]
        if dil == 1:
            return ref[sl, :]
        length = SEQ // dil
        g = sl.start // length
        return ref[g, sl.start - g * length:sl.stop - g * length, :]

    def keys(gi, j):
        length = SEQ // B_GROUPS[gi][1]
        per_stream = length // QBLK
        if per_stream == 1:
            return rows(j)
        i = j % per_stream
        start = (j - i) * QBLK + min(max(i * QBLK - B_RADIUS, 0), length - 2 * QBLK)
        return slice(start, start + 2 * QBLK)

    def scores(task, _):
        gi, j = task
        per_stream = SEQ // B_GROUPS[gi][1] // QBLK
        if per_stream == 1:
            biases = (bn_ref[0, 0], bn_ref[1, 0])
        else:
            i = j % per_stream
            var = gi * 3 + (0 if i == 0 else 2 if i == per_stream - 1 else 1)
            biases = (bw_ref[0, var], bw_ref[1, var])
        q_ref, k_ref, _ = sources[gi]
        return _pair_scores(view(q_ref, gi, rows(j)), view(k_ref, gi, keys(gi, j)), biases)

    def values(task, s):
        gi, j = task
        return _pair_values(s, view(sources[gi][2], gi, keys(gi, j)))

    def store(task, res):
        gi, j = task
        for slot, value in enumerate(_pair_merge(res)):
            if gi == 0:
                merged[0, slot, rows(j), :] = value
            else:
                stream[gi - 1, slot, rows(j), :] = value

    _software_pipeline([(gi, j) for gi in range(len(B_GROUPS)) for j in range(n_blocks)], (scores, values, store))
    for slot in range(3):
        for g in range(d16):
            start = (g % d4) * len4 + g // d4
            by4_out[slot, pl.ds(start, len16, stride=d4), :] = stream[1, slot, g * len16:(g + 1) * len16, :]
        for g in range(d4):
            run = slice(g * len4, (g + 1) * len4)
            merged[1, slot, pl.ds(g, len4, stride=d4), :] = stream[0, slot, run, :]
            merged[2, slot, pl.ds(g, len4, stride=d4), :] = by4_out[slot, run, :]

    def merge(c, carry):
        rows = pl.ds(pl.multiple_of(c * QBLK, QBLK), QBLK)
        m = [merged[gi, 0, rows, :] for gi in range(3)]
        top = jnp.maximum(jnp.maximum(m[0], m[1]), m[2])
        w = [jnp.exp2(mg - top) for mg in m]
        l = [pltpu.roll(merged[gi, 1, rows, :], HALF, axis=1) for gi in range(3)]
        num = w[0] * merged[0, 2, rows, :] + w[1] * merged[1, 2, rows, :] + w[2] * merged[2, 2, rows, :]
        den = w[0] * l[0] + w[1] * l[1] + w[2] * l[2]
        o_ref[rows, :] = (num / den).astype(BF16)
        return carry

    lax.fori_loop(0, n_blocks, merge, 0, unroll=4)


def _dilated_attention(zb, bias_wide, bias_narrow, batch):
    t = zb.shape[0]
    n_pairs = B_HEADS // 2
    nw = bias_wide.shape[1]
    d4, d16 = B_GROUPS[1][1], B_GROUPS[2][1]
    seq_block = lambda off: pl.BlockSpec((SEQ, LANES), lambda p, b: (b, off + p))
    return pl.pallas_call(
        _dilated_kernel,
        out_shape=jax.ShapeDtypeStruct((t, B_HEADS * B_HEAD_DIM), BF16),
        grid=(n_pairs, batch),
        in_specs=[seq_block(0), seq_block(n_pairs), seq_block(2 * n_pairs),
                  pl.BlockSpec((2, nw, QBLK, 2 * QBLK), lambda p, b: (p, 0, 0, 0)),
                  pl.BlockSpec((2, 1, QBLK, QBLK), lambda p, b: (p, 0, 0, 0))],
        out_specs=pl.BlockSpec((SEQ, LANES), lambda p, b: (b, p)),
        scratch_shapes=[pltpu.VMEM((3, SEQ, LANES), F32), pltpu.VMEM((3, SEQ, LANES), F32),
                        pltpu.VMEM((3, d4, SEQ // d4, LANES), BF16), pltpu.VMEM((3, d16, SEQ // d16, LANES), BF16),
                        pltpu.VMEM((2, 3, SEQ, LANES), F32), pltpu.VMEM((3, SEQ, LANES), F32),
                        pltpu.VMEM((3, 3, SEQ, LANES), F32)],
        compiler_params=_cparams(("arbitrary", "arbitrary")),
        name="dilated_attention",
    )(zb, zb, zb, bias_wide, bias_narrow)


C_SEQS = 2


def _window_kernel(sink_ref, q_ref, k_ref, v_ref, bias_ref, o_ref):
    pair = pl.program_id(0)
    sinks = (sink_ref[pair] * LOG2E, sink_ref[pair + C_HEADS // 2] * LOG2E)
    n_blocks = SEQ // QBLK

    def keys(i):
        base, local = (i // n_blocks) * SEQ, i % n_blocks
        start = base + min(max((local - 1) * QBLK, 0), SEQ - 3 * QBLK)
        return slice(start, start + 3 * QBLK)

    def scores(i, _):
        local = i % n_blocks
        var = 0 if local == 0 else 2 if local == n_blocks - 1 else 1
        return _pair_scores(q_ref[i * QBLK:(i + 1) * QBLK, :], k_ref[keys(i), :],
                            (bias_ref[0, var], bias_ref[1, var]))

    def values(i, s):
        return _pair_values(s, v_ref[keys(i), :], sinks)

    def store(i, res):
        _, l_swapped, o = _pair_merge(res)
        o_ref[i * QBLK:(i + 1) * QBLK, :] = (o / pltpu.roll(l_swapped, HALF, axis=1)).astype(BF16)

    _software_pipeline(list(range(C_SEQS * n_blocks)), (scores, values, store))


def _window_attention(zc, sink, bias, batch):
    assert batch % C_SEQS == 0
    t = zc.shape[0]
    n_pairs = C_HEADS // 2
    return pl.pallas_call(
        _window_kernel,
        out_shape=jax.ShapeDtypeStruct((t, C_HEADS * C_HEAD_DIM), BF16),
        grid=(n_pairs, batch // C_SEQS),
        in_specs=[pl.BlockSpec(memory_space=pltpu.SMEM),
                  pl.BlockSpec((C_SEQS * SEQ, LANES), lambda p, b: (b, p)),
                  pl.BlockSpec((C_SEQS * SEQ, LANES), lambda p, b: (b, n_pairs)),
                  pl.BlockSpec((C_SEQS * SEQ, LANES), lambda p, b: (b, n_pairs + 1)),
                  pl.BlockSpec((None, 2, 3, QBLK, 3 * QBLK), lambda p, b: (p, 0, 0, 0, 0))],
        out_specs=pl.BlockSpec((C_SEQS * SEQ, LANES), lambda p, b: (b, p)),
        compiler_params=_cparams(("arbitrary", "arbitrary")),
        name="window_attention",
    )(sink, zc, zc, zc, bias)


MERGE_TM = 1024
MERGE_CHAINS = 4


def _merge_kernel(x_ref, g1_ref, ya_ref, yb_ref, yc_ref, wg_ref, wbr_ref, wo_ref, o_ref):
    sub = x_ref.shape[0] // MERGE_CHAINS

    def rows(r):
        return slice(r * sub, (r + 1) * sub)

    def gated_sum(r, _):
        h = _rms(x_ref[rows(r), :], g1_ref[...]).astype(BF16)
        merged = None
        for b, y_ref in enumerate((ya_ref, yb_ref, yc_ref)):
            gate = jax.nn.sigmoid(_dot(h, wg_ref[:, b * D_MODEL:(b + 1) * D_MODEL]))
            term = gate * _dot(y_ref[rows(r), :], wbr_ref[b])
            merged = term if merged is None else merged + term
        return merged.astype(BF16)

    def project(r, merged):
        o_ref[rows(r), :] = x_ref[rows(r), :] + _dot(merged, wo_ref[...])

    _software_pipeline(list(range(MERGE_CHAINS)), (gated_sum, project))


def _merge(x2, g1, ya, yb, yc, wg, wbr, wo):
    t = x2.shape[0]
    tm = MERGE_TM
    full = lambda a: pl.BlockSpec(a.shape, lambda i: (0,) * a.ndim)
    row = lambda w: pl.BlockSpec((tm, w), lambda i: (i, 0))
    return pl.pallas_call(
        _merge_kernel,
        out_shape=jax.ShapeDtypeStruct((t, D_MODEL), F32),
        grid=(t // tm,),
        in_specs=[row(D_MODEL), full(g1), row(ya.shape[1]), row(yb.shape[1]), row(yc.shape[1]),
                  full(wg), full(wbr), full(wo)],
        out_specs=row(D_MODEL),
        compiler_params=_cparams(("parallel",)),
        name="gate_merge",
    )(x2, g1, ya, yb, yc, wg, wbr, wo)


MF_TM = 512
MF_CHAINS = 2


def _merge_ffn_kernel(x_ref, g1_ref, ya_ref, yb_ref, yc_ref, wgate_ref, wbr_ref, wo_ref,
                      g2_ref, wg_ref, wu_ref, wd_ref, o_ref):
    sub = x_ref.shape[0] // MF_CHAINS

    def rows(r):
        return slice(r * sub, (r + 1) * sub)

    def gated_sum(r, _):
        h = _rms(x_ref[rows(r), :], g1_ref[...]).astype(BF16)
        merged = None
        for b, y_ref in enumerate((ya_ref, yb_ref, yc_ref)):
            gate = jax.nn.sigmoid(_dot(h, wgate_ref[:, b * D_MODEL:(b + 1) * D_MODEL]))
            term = gate * _dot(y_ref[rows(r), :], wbr_ref[b])
            merged = term if merged is None else merged + term
        return merged.astype(BF16)

    def project(r, merged):
        return x_ref[rows(r), :] + _dot(merged, wo_ref[...])

    def up(r, x1):
        h = _rms(x1, g2_ref[...]).astype(BF16)
        return x1, (jax.nn.silu(_dot(h, wg_ref[...])) * _dot(h, wu_ref[...])).astype(BF16)

    def down(r, state):
        x1, act = state
        o_ref[rows(r), :] = x1 + _dot(act, wd_ref[...])

    _software_pipeline(list(range(MF_CHAINS)), (gated_sum, project, up, down))


def _merge_ffn(x2, g1, ya, yb, yc, wgate, wbr, wo, g2, wg, wu, wd):
    t = x2.shape[0]
    tm = MF_TM
    full = lambda a: pl.BlockSpec(a.shape, lambda i: (0,) * a.ndim)
    row = lambda w: pl.BlockSpec((tm, w), lambda i: (i, 0))
    return pl.pallas_call(
        _merge_ffn_kernel,
        out_shape=jax.ShapeDtypeStruct((t, D_MODEL), F32),
        grid=(t // tm,),
        in_specs=[row(D_MODEL), full(g1), row(ya.shape[1]), row(yb.shape[1]), row(yc.shape[1]),
                  full(wgate), full(wbr), full(wo), full(g2), full(wg), full(wu), full(wd)],
        out_specs=row(D_MODEL),
        compiler_params=_cparams(("parallel",)),
        name="merge_swiglu",
    )(x2, g1, ya, yb, yc, wgate, wbr, wo, g2, wg, wu, wd)


TOP_K = 2
ROW_SUBLANES = D_MODEL // LANES
MOE_TM = 512
ROUTER_TM = 2048
ROUTER_CHAINS = 4
EXPERT_CHAINS = 2


def _router_kernel(x_ref, g_ref, wr_ref, eid_ref, rank_ref, gate_ref, cnt_ref, carry):
    @pl.when(pl.program_id(0) == 0)
    def _():
        carry[...] = jnp.zeros_like(carry)

    sub = x_ref.shape[0] // ROUTER_CHAINS
    w = wr_ref[...]
    w_hi = w.astype(BF16)
    w_lo = (w - w_hi.astype(F32)).astype(BF16)
    earlier = (lax.broadcasted_iota(jnp.int32, (sub, sub), 0) > lax.broadcasted_iota(jnp.int32, (sub, sub), 1))
    earlier = jnp.where(earlier, 1.0, 0.0).astype(BF16)

    def rows(r):
        return slice(r * sub, (r + 1) * sub)

    def logits(r, _):
        h = _rms(x_ref[rows(r), :], g_ref[...])
        h_hi = h.astype(BF16)
        h_lo = (h - h_hi.astype(F32)).astype(BF16)
        return _dot(h_hi, w_hi) + (_dot(h_lo, w_hi) + _dot(h_hi, w_lo))

    def select(r, lg):
        idx = lax.broadcasted_iota(jnp.int32, lg.shape, 1)
        m1 = jnp.max(lg, axis=-1, keepdims=True)
        i1 = jnp.min(jnp.where(lg == m1, idx, N_EXPERTS), axis=-1, keepdims=True)
        first = idx == i1
        rest = jnp.where(first, -jnp.inf, lg)
        m2 = jnp.max(rest, axis=-1, keepdims=True)
        i2 = jnp.min(jnp.where(rest == m2, idx, N_EXPERTS), axis=-1, keepdims=True)
        second = idx == i2
        e = jnp.exp(m2 - m1)
        chosen = jnp.where(jnp.logical_or(first, second), 1.0, 0.0)
        return i1, i2, first, second, e, chosen, _dot(earlier, chosen.astype(BF16))

    def emit(r, state):
        i1, i2, first, second, e, chosen, before = state
        before = before + carry[...]
        r1 = jnp.sum(jnp.where(first, before, 0.0), axis=-1, keepdims=True)
        r2 = jnp.sum(jnp.where(second, before, 0.0), axis=-1, keepdims=True)
        carry[...] += jnp.sum(chosen, axis=0, keepdims=True)
        den = 1.0 + e
        slot0 = lax.broadcasted_iota(jnp.int32, (sub, TOP_K), 1) == 0
        eid_ref[rows(r), :] = jnp.where(slot0, i1, i2)
        rank_ref[rows(r), :] = jnp.where(slot0, r1, r2).astype(jnp.int32)
        gate_ref[rows(r), :] = jnp.where(slot0, 1.0 / den, e / den)

    _software_pipeline(list(range(ROUTER_CHAINS)), (logits, select, emit))
    cnt_ref[...] = carry[...].astype(jnp.int32)


def _router(x2, g, wr):
    t = x2.shape[0]
    tm = ROUTER_TM
    pair = pl.BlockSpec((tm, TOP_K), lambda i: (i, 0))
    return pl.pallas_call(
        _router_kernel,
        out_shape=(jax.ShapeDtypeStruct((t, TOP_K), jnp.int32), jax.ShapeDtypeStruct((t, TOP_K), jnp.int32),
                   jax.ShapeDtypeStruct((t, TOP_K), F32), jax.ShapeDtypeStruct((1, N_EXPERTS), jnp.int32)),
        grid=(t // tm,),
        in_specs=[pl.BlockSpec((tm, D_MODEL), lambda i: (i, 0)),
                  pl.BlockSpec(g.shape, lambda i: (0, 0)),
                  pl.BlockSpec(wr.shape, lambda i: (0, 0))],
        out_specs=(pair, pair, pair, pl.BlockSpec((1, N_EXPERTS), lambda i: (0, 0))),
        scratch_shapes=[pltpu.VMEM((1, N_EXPERTS), F32)],
        compiler_params=_cparams(("arbitrary",)),
        name="router",
    )(x2, g, wr)


def _row_tile(ref, row):
    return ref.at[pl.ds(pl.multiple_of(row * ROW_SUBLANES, ROW_SUBLANES), ROW_SUBLANES), :]


def _dispatch_kernel(tail_start_ref, tail_len_ref, na_ref, dest_ref, x_ref, g_ref, hs_ref, stage, sems):
    i = pl.program_id(0)
    n_steps = pl.num_programs(0)
    tm = x_ref.shape[0]
    slot = i % 2
    rows_per_slot = stage.shape[1]

    def drain(s):
        for _ in range(TOP_K):
            pltpu.make_async_copy(stage.at[s], hs_ref.at[pl.ds(0, rows_per_slot), :], sems.at[s]).wait()

    @pl.when(i >= 2)
    def _():
        drain(slot)

    h = _rms(x_ref[...], g_ref[...])
    for s in range(ROW_SUBLANES):
        stage[slot, pl.ds(s, tm, stride=ROW_SUBLANES), :] = h[:, s * LANES:(s + 1) * LANES]

    def start(t, carry):
        for k in range(TOP_K):
            pltpu.make_async_copy(_row_tile(stage.at[slot], t), _row_tile(hs_ref, dest_ref[TOP_K * t + k]),
                                  sems.at[slot]).start(priority=k)
        return carry

    lax.fori_loop(0, tm, start, 0, unroll=8)

    @pl.when(i == n_steps - 1)
    def _():
        @pl.when(n_steps >= 2)
        def _():
            drain(1 - slot)

        drain(slot)
        zeros = stage.at[0]
        zeros[...] = jnp.zeros_like(zeros)
        for e in range(N_EXPERTS):
            def tail(r, e=e):
                return pltpu.make_async_copy(_row_tile(zeros, 0), _row_tile(hs_ref, tail_start_ref[e] + r), sems.at[0])

            lax.fori_loop(0, tail_len_ref[e], lambda r, c: (tail(r).start(), c)[1], 0)
            lax.fori_loop(0, tail_len_ref[e], lambda r, c: (tail(r).wait(), c)[1], 0)
        n_tiles = hs_ref.shape[0] // rows_per_slot

        def spare(j):
            rows = pl.ds(pl.multiple_of(j * rows_per_slot, rows_per_slot), rows_per_slot)
            return pltpu.make_async_copy(zeros, hs_ref.at[rows, :], sems.at[0])

        lax.fori_loop(na_ref[0], n_tiles, lambda j, c: (spare(j).start(), c)[1], 0)
        lax.fori_loop(na_ref[0], n_tiles, lambda j, c: (spare(j).wait(), c)[1], 0)


def _dispatch(x2, g, dest, tail_start, tail_len, n_active, n_rows):
    t = x2.shape[0]
    tm = MOE_TM
    return pl.pallas_call(
        _dispatch_kernel,
        out_shape=jax.ShapeDtypeStruct((n_rows * ROW_SUBLANES, LANES), F32),
        grid_spec=pltpu.PrefetchScalarGridSpec(
            num_scalar_prefetch=3,
            grid=(t // tm,),
            in_specs=[pl.BlockSpec((TOP_K * tm,), lambda i, ts, tl, na: (i,), memory_space=pltpu.SMEM),
                      pl.BlockSpec((tm, D_MODEL), lambda i, ts, tl, na: (i, 0)),
                      pl.BlockSpec(g.shape, lambda i, ts, tl, na: (0, 0))],
            out_specs=pl.BlockSpec(memory_space=pl.ANY),
            scratch_shapes=[pltpu.VMEM((2, tm * ROW_SUBLANES, LANES), F32), pltpu.SemaphoreType.DMA((2,))]),
        compiler_params=_cparams(("arbitrary",)),
        name="dispatch",
    )(tail_start, tail_len, n_active, dest, x2, g)


def _experts_kernel(te_ref, na_ref, hs_ref, wg_ref, wu_ref, wd_ref, ys_ref):
    active = pl.program_id(0) < na_ref[0]
    sub = hs_ref.shape[0] // ROW_SUBLANES // EXPERT_CHAINS

    def lane_groups(r):
        return [pl.ds(r * sub * ROW_SUBLANES + s, sub, stride=ROW_SUBLANES) for s in range(ROW_SUBLANES)]

    def up(r, _):
        h = jnp.concatenate([hs_ref[g, :].astype(BF16) for g in lane_groups(r)], axis=1)
        return (jax.nn.silu(_dot(h, wg_ref[...])) * _dot(h, wu_ref[...])).astype(BF16)

    def down(r, act):
        return _dot(act, wd_ref[...])

    def store(r, y):
        for s, g in enumerate(lane_groups(r)):
            ys_ref[g, :] = y[:, s * LANES:(s + 1) * LANES]

    @pl.when(active)
    def _():
        _software_pipeline(list(range(EXPERT_CHAINS)), (up, down, store))

    @pl.when(jnp.logical_not(active))
    def _():
        ys_ref[...] = jnp.zeros_like(ys_ref)


def _experts(hs, tile_expert, n_active, wg, wu, wd):
    tm = MOE_TM
    n_tiles = hs.shape[0] // (tm * ROW_SUBLANES)
    d_ff = wg.shape[2]
    once = pl.Buffered(1)
    return pl.pallas_call(
        _experts_kernel,
        out_shape=jax.ShapeDtypeStruct(hs.shape, F32),
        grid_spec=pltpu.PrefetchScalarGridSpec(
            num_scalar_prefetch=2,
            grid=(n_tiles,),
            in_specs=[pl.BlockSpec((tm * ROW_SUBLANES, LANES), lambda i, te, na: (jnp.minimum(i, na[0] - 1), 0)),
                      pl.BlockSpec((None, D_MODEL, d_ff), lambda i, te, na: (te[i], 0, 0), pipeline_mode=once),
                      pl.BlockSpec((None, D_MODEL, d_ff), lambda i, te, na: (te[i], 0, 0), pipeline_mode=once),
                      pl.BlockSpec((None, d_ff, D_MODEL), lambda i, te, na: (te[i], 0, 0), pipeline_mode=once)],
            out_specs=pl.BlockSpec((tm * ROW_SUBLANES, LANES), lambda i, te, na: (i, 0))),
        compiler_params=_cparams(("arbitrary",)),
        name="experts",
    )(tile_expert, n_active, hs, wg, wu, wd)


def _combine_kernel(dest_ref, next_dest_ref, x_ref, gate_ref, gf_ref, ys_ref, o_ref, bufs, sems, *, final):
    i = pl.program_id(0)
    tm = x_ref.shape[0]
    slot = i % 2

    def gather(idx_ref, s):
        def start(t, carry):
            for k in range(TOP_K):
                pltpu.make_async_copy(_row_tile(ys_ref, idx_ref[TOP_K * t + k]), _row_tile(bufs.at[s, k], t),
                                      sems.at[s]).start(priority=k)
            return carry

        lax.fori_loop(0, tm, start, 0, unroll=8)

    @pl.when(i == 0)
    def _():
        gather(dest_ref, slot)

    @pl.when(i + 1 < pl.num_programs(0))
    def _():
        gather(next_dest_ref, 1 - slot)

    for k in range(TOP_K):
        pltpu.make_async_copy(ys_ref.at[pl.ds(0, bufs.shape[2]), :], bufs.at[slot, k], sems.at[slot]).wait()
    gate = gate_ref[...]
    g0 = gate[:, 0:1]
    g1 = gate[:, 1:2]
    ss = jnp.zeros((tm, 1), F32)
    for s in range(ROW_SUBLANES):
        sl = slice(s * LANES, (s + 1) * LANES)
        rows = pl.ds(s, tm, stride=ROW_SUBLANES)
        y = x_ref[:, sl] + (g0 * bufs[slot, 0, rows, :] + g1 * bufs[slot, 1, rows, :])
        o_ref[:, sl] = y
        ss = ss + jnp.sum(y * y, axis=-1, keepdims=True)
    if final:
        o_ref[...] = o_ref[...] * lax.rsqrt(ss / D_MODEL + EPS) * gf_ref[...]


def _combine(x2, dest, gates, gf, ys, *, final):
    t = x2.shape[0]
    tm = MOE_TM
    return pl.pallas_call(
        functools.partial(_combine_kernel, final=final),
        out_shape=jax.ShapeDtypeStruct((t, D_MODEL), F32),
        grid=(t // tm,),
        in_specs=[pl.BlockSpec((TOP_K * tm,), lambda i: (i,), memory_space=pltpu.SMEM),
                  pl.BlockSpec((TOP_K * tm,), lambda i: (jnp.minimum(i + 1, t // tm - 1),),
                               memory_space=pltpu.SMEM),
                  pl.BlockSpec((tm, D_MODEL), lambda i: (i, 0)),
                  pl.BlockSpec((tm, TOP_K), lambda i: (i, 0)),
                  pl.BlockSpec(gf.shape, lambda i: (0, 0)),
                  pl.BlockSpec(memory_space=pl.ANY)],
        out_specs=pl.BlockSpec((tm, D_MODEL), lambda i: (i, 0)),
        scratch_shapes=[pltpu.VMEM((2, TOP_K, tm * ROW_SUBLANES, LANES), F32), pltpu.SemaphoreType.DMA((2,))],
        compiler_params=_cparams(("arbitrary",)),
        name="combine",
    )(dest, dest, x2, gates, gf, ys)


def _moe(x2, g, wr, wg, wu, wd, gf, *, final):
    t = x2.shape[0]
    tm = MOE_TM
    eid, rank, gates, counts = _router(x2, g, wr)
    counts = counts[0]
    padded = (counts + tm - 1) // tm * tm
    group_end = jnp.cumsum(padded)
    group_start = group_end - padded
    dest = (group_start[eid] + rank).reshape(-1)
    n_tiles = TOP_K * t // tm + N_EXPERTS
    n_active = (group_end[-1:] // tm).astype(jnp.int32)
    tile_row = jnp.arange(n_tiles, dtype=jnp.int32) * tm
    tile_row = jnp.minimum(tile_row, group_end[-1] - tm)
    tile_expert = jnp.sum(tile_row[:, None] >= group_end[None, :], axis=1).astype(jnp.int32)
    hs = _dispatch(x2, g, dest, group_start + counts, padded - counts, n_active, n_tiles * tm)
    ys = _experts(hs, tile_expert, n_active, wg, wu, wd)
    return _combine(x2, dest, gates, gf, ys, final=final)


FFN_TM = 1024
FFN_CHAINS = 4


def _ffn_kernel(x_ref, g_ref, wg_ref, wu_ref, wd_ref, gf_ref, o_ref, *, final):
    sub = x_ref.shape[0] // FFN_CHAINS

    def rows(r):
        return slice(r * sub, (r + 1) * sub)

    def up(r, _):
        h = _rms(x_ref[rows(r), :], g_ref[...]).astype(BF16)
        return (jax.nn.silu(_dot(h, wg_ref[...])) * _dot(h, wu_ref[...])).astype(BF16)

    def down(r, act):
        y = x_ref[rows(r), :] + _dot(act, wd_ref[...])
        if final:
            y = _rms(y, gf_ref[...])
        o_ref[rows(r), :] = y

    _software_pipeline(list(range(FFN_CHAINS)), (up, down))


def _ffn(x2, g, wg, wu, wd, gf, *, final):
    t = x2.shape[0]
    tm = FFN_TM
    full = lambda a: pl.BlockSpec(a.shape, lambda i: (0,) * a.ndim)
    return pl.pallas_call(
        functools.partial(_ffn_kernel, final=final),
        out_shape=jax.ShapeDtypeStruct((t, D_MODEL), F32),
        grid=(t // tm,),
        in_specs=[pl.BlockSpec((tm, D_MODEL), lambda i: (i, 0)), full(g), full(wg), full(wu), full(wd), full(gf)],
        out_specs=pl.BlockSpec((tm, D_MODEL), lambda i: (i, 0)),
        compiler_params=_cparams(("parallel",)),
        name="swiglu",
    )(x2, g, wg, wu, wd, gf)


def _rot_cols(w):
    half = A_ROPE // 2
    return jnp.concatenate([-w[..., half:], w[..., :half]], axis=-1)


def _prep_layer(w_in, w_uq, w_ukv, w_br_a, w_br_b, w_br_c):
    o = 0
    w_cq = w_in[:, o:o + A_Q_LORA]; o += A_Q_LORA
    w_ckv = w_in[:, o:o + A_KV_LORA]; o += A_KV_LORA
    w_kpe = w_in[:, o:o + A_ROPE]; o += A_ROPE
    nb = B_HEADS * B_HEAD_DIM
    w_b = w_in[:, o:o + 3 * nb]; o += 3 * nb
    nq = C_HEADS * C_HEAD_DIM
    nkv = C_KV_HEADS * C_HEAD_DIM
    w_c = w_in[:, o:o + nq + 2 * nkv]; o += nq + 2 * nkv
    w_g = w_in[:, o:]
    pad = jnp.zeros((D_MODEL, LANES - 2 * A_ROPE), F32)
    wa = jnp.concatenate([w_cq, w_ckv, w_kpe, _rot_cols(w_kpe), pad], axis=1)
    wb = jnp.concatenate([w_b[:, :nb] * (B_HEAD_DIM ** -0.5), w_b[:, nb:]], axis=1)
    q_c = w_c[:, :nq].reshape(D_MODEL, 2, C_HEADS // 2, C_HEAD_DIM).transpose(0, 2, 1, 3).reshape(D_MODEL, nq)
    wc = jnp.concatenate([q_c * (C_HEAD_DIM ** -0.5), w_c[:, nq:]], axis=1)
    uq = w_uq.reshape(A_Q_LORA, A_HEADS, A_NOPE + A_ROPE)
    z32 = jnp.zeros((A_Q_LORA, A_HEADS, LANES - A_NOPE - A_ROPE), F32)
    wq = jnp.concatenate([uq, z32], axis=-1).reshape(A_Q_LORA, A_HEADS * LANES)
    wqr = jnp.concatenate([jnp.zeros((A_Q_LORA, A_HEADS, A_NOPE), F32), _rot_cols(uq[..., A_NOPE:]), z32],
                          axis=-1).reshape(A_Q_LORA, A_HEADS * LANES)
    ukv = w_ukv.reshape(A_KV_LORA, A_HEADS, A_NOPE + A_V)
    wk = jnp.concatenate([ukv[..., :A_NOPE], jnp.zeros((A_KV_LORA, A_HEADS, LANES - A_NOPE), F32)],
                         axis=-1).reshape(A_KV_LORA, A_HEADS * LANES)
    uv = ukv[..., A_NOPE:].reshape(A_KV_LORA, A_HEADS // 2, 2, A_V)
    zv = jnp.zeros_like(uv[:, :, 0])
    wv = jnp.stack([jnp.concatenate([uv[:, :, 0], zv], axis=-1), jnp.concatenate([zv, uv[:, :, 1]], axis=-1)],
                   axis=2).reshape(A_KV_LORA, A_HEADS * LANES)
    w_br_c = w_br_c.reshape(2, C_HEADS // 2, C_HEAD_DIM, D_MODEL).transpose(1, 0, 2, 3).reshape(nq, D_MODEL)
    wbr = jnp.stack([w_br_a, w_br_b, w_br_c])
    wkv = jnp.concatenate([jnp.concatenate([wk, wv], axis=1),
                           jnp.concatenate([_rope_placement(), jnp.zeros((LANES, A_HEADS * LANES), F32)], axis=1)])
    cast = lambda a: a.astype(BF16)
    return dict(wa=cast(wa), wb=cast(wb), wc=cast(wc), wg=cast(w_g), wq=cast(wq), wqr=cast(wqr), wkv=cast(wkv),
                wbr=cast(wbr))


def _rope_tables():
    half = A_ROPE // 2
    inv = ROPE_THETA ** (-jnp.arange(half, dtype=F32) / half)
    ang = jnp.arange(SEQ, dtype=F32)[:, None] * inv[None, :]
    cos2 = jnp.tile(jnp.cos(ang), (1, 2))
    sin2 = jnp.tile(jnp.sin(ang), (1, 2))
    ones = jnp.ones((SEQ, A_NOPE), F32)
    z = lambda w: jnp.zeros((SEQ, w), F32)
    cq = jnp.concatenate([ones, cos2, z(LANES - A_NOPE - A_ROPE)], axis=1)
    sq = jnp.concatenate([z(A_NOPE), sin2, z(LANES - A_NOPE - A_ROPE)], axis=1)
    ck = jnp.concatenate([cos2, z(LANES - A_ROPE)], axis=1)
    sk = jnp.concatenate([sin2, z(LANES - A_ROPE)], axis=1)
    return cq, sq, ck, sk


def _rope_placement():
    place = np.zeros((LANES, A_HEADS * LANES), np.float32)
    for h in range(A_HEADS):
        for j in range(A_ROPE):
            place[j, h * LANES + A_NOPE + j] = 1.0
    return jnp.asarray(place)


def kernel(x, norm1_g, w_in, q_norm_g, w_uq, kv_norm_g, w_ukv, sink_logit, rel_bias, w_branch_a, w_branch_b,
           w_branch_c, w_out, norm2_g, ffn_w_gate, ffn_w_up, ffn_w_down, router_w, exp_w_gate, exp_w_up,
           exp_w_down, final_g):
    batch, seq, d = x.shape
    assert seq == SEQ and d == D_MODEL
    t = batch * seq
    x2 = x.reshape(t, d)

    wide_idx, narrow_idx = _dilated_index_tables()
    bias_wide = _bias_tables(rel_bias, wide_idx, 0, B_HEADS)
    bias_narrow = _bias_tables(rel_bias, narrow_idx, 0, B_HEADS)
    bias_c = _bias_tables(rel_bias, _window_index_tables(), B_HEADS, C_HEADS)
    bias_c = bias_c.reshape(2, C_HEADS // 2, *bias_c.shape[1:]).transpose(1, 0, 2, 3, 4)

    cq, sq, ck, sk = _rope_tables()
    row = lambda v: v.reshape(1, -1)
    q_then = lambda nq, rest: jnp.asarray(np.concatenate([np.full(nq, LOG2E), np.ones(rest)])[None], F32)
    bscale = q_then(B_HEADS * B_HEAD_DIM, 2 * B_HEADS * B_HEAD_DIM)
    cscale = q_then(C_HEADS * C_HEAD_DIM, 2 * C_KV_HEADS * C_HEAD_DIM)
    vones = jnp.asarray(np.tile(np.repeat([0.0, 1.0, 1.0, 0.0], A_V), A_HEADS // 2)[None], F32)

    for l in range(DEPTH):
        w = _prep_layer(w_in[l], w_uq[l], w_ukv[l], w_branch_a[l], w_branch_b[l], w_branch_c[l])
        g1 = row(norm1_g[l])
        qa, ka, va, zb, zc = _in_proj(x2, g1, w["wa"], w["wb"], w["wc"], row(q_norm_g[l]), row(kv_norm_g[l]),
                                      w["wq"], w["wqr"], w["wkv"], bscale, cscale, vones, cq, sq, ck, sk)
        ya = _mla_attention(qa, ka, va, batch)
        yb = _dilated_attention(zb, bias_wide, bias_narrow, batch)
        yc = _window_attention(zc, sink_logit[l], bias_c, batch)
        g2 = row(norm2_g[l])
        final = l == DEPTH - 1
        i = l // 2
        if l % 2 == 0 and not final:
            x2 = _merge_ffn(x2, g1, ya, yb, yc, w["wg"], w["wbr"], w_out[l].astype(BF16), g2,
                            ffn_w_gate[i].astype(BF16), ffn_w_up[i].astype(BF16), ffn_w_down[i].astype(BF16))
            continue
        x2 = _merge(x2, g1, ya, yb, yc, w["wg"], w["wbr"], w_out[l].astype(BF16))
        if l % 2 == 0:
            x2 = _ffn(x2, g2, ffn_w_gate[i].astype(BF16), ffn_w_up[i].astype(BF16), ffn_w_down[i].astype(BF16),
                      row(final_g), final=final)
        else:
            x2 = _moe(x2, g2, router_w[i], exp_w_gate[i].astype(BF16), exp_w_up[i].astype(BF16),
                      exp_w_down[i].astype(BF16), row(final_g), final=final)
    return x2.reshape(batch, seq, d)
```
